```python
import jax, jax.numpy as jnp
from jax import lax
import numpy as np

D_MODEL = 4096
BATCH = 2
SEQ = 4096
DEPTH = 2

NSA_HEADS = 16
NSA_KV_GROUPS = 4
NSA_HPG = NSA_HEADS // NSA_KV_GROUPS
HEAD_DIM = 128
CMP_LEN = 32
CMP_STRIDE = 16
SLC_BLOCK = 64
SLC_TOPK = 16
SLC_Q_BLOCK = 64
WINDOW = 512
WIN_Q_BLOCK = 128
ATTN_SCALE = HEAD_DIM ** -0.5
FORCED_BONUS = 1e6
NEG_INF = -1e30
HGRN_HEADS = 16
HGRN_DK = 128
HGRN_DV = 128
HGRN_CHUNK = 64
ROPE_THETA = 500000.0
ROT_DIM = HEAD_DIM // 4
D_FF = 4 * D_MODEL
NORM_EPS = 1e-6

NSA_WIDTH = NSA_HEADS * HEAD_DIM
KV_WIDTH = NSA_KV_GROUPS * HEAD_DIM
HGRN_KW = HGRN_HEADS * HGRN_DK
HGRN_VW = HGRN_HEADS * HGRN_DV
IN_SIZES = (NSA_WIDTH, KV_WIDTH, KV_WIDTH, KV_WIDTH, KV_WIDTH, KV_WIDTH, KV_WIDTH, 3 * NSA_HEADS,
            HGRN_KW, HGRN_KW, HGRN_VW, HGRN_VW, D_MODEL, D_MODEL)
IN_WIDTH = sum(IN_SIZES)
IN_SPLITS = tuple(int(v) for v in np.cumsum(IN_SIZES)[:-1])

kernel_name = 'nsa_hgrn2_hybrid_block'


def rms_norm(x, w):
    xf = x.astype(jnp.float32)
    y = xf * lax.rsqrt(jnp.mean(xf * xf, axis=-1, keepdims=True) + NORM_EPS)
    return (y * w.astype(jnp.float32)).astype(x.dtype)


def partial_rope(x, positions):
    inv = ROPE_THETA ** (-jnp.arange(0, ROT_DIM, 2, dtype=jnp.float32) / ROT_DIM)
    ang = positions.astype(jnp.float32)[..., None] * inv
    cos = jnp.cos(ang)[:, :, None, :].astype(x.dtype)
    sin = jnp.sin(ang)[:, :, None, :].astype(x.dtype)
    half = ROT_DIM // 2
    x1, x2, xp = x[..., :half], x[..., half:ROT_DIM], x[..., ROT_DIM:]
    return jnp.concatenate([x1 * cos - x2 * sin, x2 * cos + x1 * sin, xp], axis=-1)


def compress_blocks(k_raw, pos, w1, w2):
    S = k_raw.shape[1]
    nc = (S - CMP_LEN) // CMP_STRIDE + 1
    idx = np.arange(nc)[:, None] * CMP_STRIDE + np.arange(CMP_LEN)[None, :]
    kb = k_raw[:, idx] + pos[None, None, :, None, :]
    hid = jax.nn.gelu(jnp.einsum('bnlgd,lde->bnge', kb, w1))
    return jnp.einsum('bnge,ef->bgnf', hid, w2)


def compressed_attention(q, k_cmp, v_cmp):
    S = q.shape[3]
    nc = k_cmp.shape[2]
    s = jnp.einsum('bgjtd,bgnd->bgjtn', q, k_cmp).astype(jnp.float32) * ATTN_SCALE
    t = np.arange(S)
    blk_end = np.arange(nc) * CMP_STRIDE + CMP_LEN - 1
    mask = blk_end[None, :] <= t[:, None]
    p = jnp.where(mask, jax.nn.softmax(jnp.where(mask, s, NEG_INF), axis=-1), 0.0)
    o = jnp.einsum('bgjtn,bgnd->bgjtd', p.astype(v_cmp.dtype), v_cmp)
    return o, p.sum(axis=2)


def cmp_slc_overlap(S):
    nc = (S - CMP_LEN) // CMP_STRIDE + 1
    nsel = S // SLC_BLOCK
    cs = np.arange(nc) * CMP_STRIDE
    ce = cs + CMP_LEN - 1
    ss = np.arange(nsel) * SLC_BLOCK
    se = ss + SLC_BLOCK - 1
    ov = np.minimum(ce[:, None], se[None, :]) - np.maximum(cs[:, None], ss[None, :]) + 1
    return np.maximum(ov, 0).astype(np.float32)


def select_blocks(imp):
    S = imp.shape[2]
    nsel = S // SLC_BLOCK
    p_slc = jnp.einsum('bgtn,nm->bgtm', imp, jnp.asarray(cmp_slc_overlap(S)))
    t = np.arange(S)
    blk = np.arange(nsel)
    cur = t // SLC_BLOCK
    valid = blk[None, :] * SLC_BLOCK <= t[:, None]
    forced = (blk[None, :] == 0) | (blk[None, :] == cur[:, None]) | (blk[None, :] == cur[:, None] - 1)
    bonus = np.where(forced, FORCED_BONUS, 0.0).astype(np.float32)
    score = jnp.where(valid, p_slc + bonus, -jnp.inf)
    vals, idx = lax.top_k(score, min(SLC_TOPK, nsel))
    return idx, jnp.isfinite(vals)


def selected_attention(q, k, v, idx, ok):
    B, G, J, S, HD = q.shape
    nsel = S // SLC_BLOCK
    kb = k.reshape(B, G, nsel, SLC_BLOCK, HD)
    vb = v.reshape(B, G, nsel, SLC_BLOCK, HD)
    bi = jnp.arange(B)[:, None, None, None]
    gi = jnp.arange(G)[None, :, None, None]
    n_topk = idx.shape[-1]

    def one_block(n):
        start = n * SLC_Q_BLOCK
        qc = lax.dynamic_slice_in_dim(q, start, SLC_Q_BLOCK, axis=3)
        ic = lax.dynamic_slice_in_dim(idx, start, SLC_Q_BLOCK, axis=2)
        okc = lax.dynamic_slice_in_dim(ok, start, SLC_Q_BLOCK, axis=2)
        k_sel = kb[bi, gi, ic]
        v_sel = vb[bi, gi, ic]
        s = jnp.einsum('bgjqd,bgqkld->bgjqkl', qc, k_sel).astype(jnp.float32) * ATTN_SCALE
        kpos = ic[..., None] * SLC_BLOCK + jnp.arange(SLC_BLOCK)
        tq = start + jnp.arange(SLC_Q_BLOCK)
        mask = (okc[..., None] & (kpos <= tq[None, None, :, None, None]))[:, :, None]
        s = jnp.where(mask, s, NEG_INF).reshape(B, G, J, SLC_Q_BLOCK, n_topk * SLC_BLOCK)
        p = jax.nn.softmax(s, axis=-1).reshape(B, G, J, SLC_Q_BLOCK, n_topk, SLC_BLOCK)
        return jnp.einsum('bgjqkl,bgqkld->bgjqd', p.astype(v_sel.dtype), v_sel)

    out = lax.map(one_block, jnp.arange(S // SLC_Q_BLOCK))
    return jnp.moveaxis(out, 0, 3).reshape(B, G, J, S, HD)


def window_attention(q, k, v):
    B, G, J, S, HD = q.shape
    nqb = S // WIN_Q_BLOCK
    nband = WINDOW // WIN_Q_BLOCK + 1
    pad = ((0, 0), (0, 0), (WINDOW, 0), (0, 0))
    kc = jnp.pad(k, pad).reshape(B, G, nqb + nband - 1, WIN_Q_BLOCK, HD)
    vc = jnp.pad(v, pad).reshape(B, G, nqb + nband - 1, WIN_Q_BLOCK, HD)
    kband = jnp.stack([kc[:, :, j:j + nqb] for j in range(nband)], axis=3).reshape(B, G, nqb, nband * WIN_Q_BLOCK, HD)
    vband = jnp.stack([vc[:, :, j:j + nqb] for j in range(nband)], axis=3).reshape(B, G, nqb, nband * WIN_Q_BLOCK, HD)
    qb = q.reshape(B, G, J, nqb, WIN_Q_BLOCK, HD)
    s = jnp.einsum('bgjnqd,bgnkd->bgjnqk', qb, kband).astype(jnp.float32) * ATTN_SCALE
    tpos = np.arange(nqb)[:, None] * WIN_Q_BLOCK + np.arange(WIN_Q_BLOCK)[None, :]
    kpos = np.arange(nqb)[:, None] * WIN_Q_BLOCK - WINDOW + np.arange(nband * WIN_Q_BLOCK)[None, :]
    mask = ((kpos[:, None, :] <= tpos[:, :, None]) & (kpos[:, None, :] > tpos[:, :, None] - WINDOW)
            & (kpos[:, None, :] >= 0))
    p = jax.nn.softmax(jnp.where(mask, s, NEG_INF), axis=-1)
    o = jnp.einsum('bgjnqk,bgnkd->bgjnqd', p.astype(vband.dtype), vband)
    return o.reshape(B, G, J, S, HD)


def hgrn2(q, z, i, g, lb, norm_w):
    B, S, H, _ = q.shape
    zf = z.astype(jnp.float32)
    log_f = jnp.logaddexp(jnp.log(lb), jnp.log1p(-lb) + jax.nn.log_sigmoid(zf))
    k = (1.0 - lb) * jax.nn.sigmoid(-zf)
    C = HGRN_CHUNK
    nch = S // C

    def chunks(t):
        return t.astype(jnp.float32).reshape(B, nch, C, H, t.shape[-1]).transpose(1, 0, 3, 2, 4)

    causal = np.tril(np.ones((C, C), dtype=bool))[:, :, None]

    def step(state, inp):
        qc, kc, vc, lfc = inp
        b = jnp.cumsum(lfc, axis=2)
        diff = b[:, :, :, None, :] - b[:, :, None, :, :]
        decay = jnp.where(causal, jnp.exp(jnp.where(causal, diff, 0.0)), 0.0)
        att = jnp.einsum('bhtd,bhsd,bhtsd->bhts', qc, kc, decay)
        o = jnp.einsum('bhts,bhse->bhte', att, vc) + jnp.einsum('bhtd,bhde->bhte', qc * jnp.exp(b), state)
        b_last = b[:, :, -1:, :]
        state = (jnp.exp(b_last[:, :, 0, :])[..., None] * state
                 + jnp.einsum('bhsd,bhse->bhde', kc * jnp.exp(b_last - b), vc))
        return state, o

    state0 = jnp.zeros((B, H, HGRN_DK, HGRN_DV), jnp.float32)
    _, o = lax.scan(step, state0, (chunks(q), chunks(k), chunks(i), chunks(log_f)))
    o = o.transpose(1, 0, 3, 2, 4).reshape(B, S, H, HGRN_DV)
    o = rms_norm(o, norm_w) * jax.nn.silu(g.astype(jnp.float32))
    return o.astype(g.dtype).reshape(B, S, H * HGRN_DV)


def hybrid_mixer(h, positions, w_in, cmp_pos, cmp_w1, cmp_w2, lb, g_norm_w, w_up_a, w_up_b, w_out):
    B, S, _ = h.shape
    G, J, HD = NSA_KV_GROUPS, NSA_HPG, HEAD_DIM
    (q, kc, vc, ks, vs, kw, vw, ng, hq, hf, hi, hg, ga, gb) = jnp.split(h @ w_in, IN_SPLITS, axis=-1)
    q = q.reshape(B, S, NSA_HEADS, HD)

    def grp_q(t):
        return t.reshape(B, S, G, J, HD).transpose(0, 2, 3, 1, 4)

    def kv(t):
        return t.reshape(B, S, G, HD)

    def grp_k(t):
        return t.transpose(0, 2, 1, 3)

    q_plain = grp_q(q)
    q_rot = grp_q(partial_rope(q, positions))
    k_cmp = compress_blocks(kv(kc), cmp_pos[0], cmp_w1[0], cmp_w2[0])
    v_cmp = compress_blocks(kv(vc), cmp_pos[1], cmp_w1[1], cmp_w2[1])
    o_cmp, imp = compressed_attention(q_plain, k_cmp, v_cmp)
    blk_idx, blk_ok = select_blocks(imp)
    o_slc = selected_attention(q_rot, grp_k(partial_rope(kv(ks), positions)), grp_k(kv(vs)), blk_idx, blk_ok)
    o_win = window_attention(q_rot, grp_k(partial_rope(kv(kw), positions)), grp_k(kv(vw)))
    gates = jax.nn.sigmoid(ng).reshape(B, S, 3, G, J).transpose(2, 0, 3, 4, 1)[..., None]
    o_nsa = gates[0] * o_cmp + gates[1] * o_slc + gates[2] * o_win
    o_nsa = o_nsa.transpose(0, 3, 1, 2, 4).reshape(B, S, NSA_WIDTH)
    o_hgrn = hgrn2(hq.reshape(B, S, HGRN_HEADS, HGRN_DK), hf.reshape(B, S, HGRN_HEADS, HGRN_DK),
                   hi.reshape(B, S, HGRN_HEADS, HGRN_DV), hg.reshape(B, S, HGRN_HEADS, HGRN_DV),
                   lb.reshape(HGRN_HEADS, HGRN_DK), g_norm_w)
    y = jax.nn.sigmoid(ga) * (o_nsa @ w_up_a) + jax.nn.sigmoid(gb) * (o_hgrn @ w_up_b)
    return y @ w_out


def setup_inputs(seed: int = 0) -> dict:
    key = jax.random.key(seed)
    ks = jax.random.split(key, 20)
    f32 = jnp.float32
    nrm = lambda k, shape, scale: jax.random.normal(k, shape, f32) * scale
    offset = jax.random.randint(ks[2], (BATCH, 1), 0, 1024)
    positions = (jnp.arange(SEQ, dtype=jnp.int32)[None, :] + offset).astype(jnp.int32)
    return {
        'x': nrm(ks[0], (BATCH, SEQ, D_MODEL), 1.0),
        'c': nrm(ks[1], (BATCH, D_MODEL), 1.0),
        'positions': positions,
        'ada_w': nrm(ks[3], (DEPTH, D_MODEL, 6 * D_MODEL), 0.5 * D_MODEL ** -0.5),
        'ada_b': nrm(ks[4], (DEPTH, 6 * D_MODEL), 0.02),
        'norm_mix_w': 1.0 + nrm(ks[5], (DEPTH, D_MODEL), 0.02),
        'w_in': nrm(ks[6], (DEPTH, D_MODEL, IN_WIDTH), D_MODEL ** -0.5),
        'nsa_cmp_pos': nrm(ks[7], (DEPTH, 2, CMP_LEN, HEAD_DIM), 0.02),
        'nsa_cmp_w1': nrm(ks[8], (DEPTH, 2, CMP_LEN, HEAD_DIM, HEAD_DIM), (CMP_LEN * HEAD_DIM) ** -0.5),
        'nsa_cmp_w2': nrm(ks[9], (DEPTH, 2, HEAD_DIM, HEAD_DIM), HEAD_DIM ** -0.5),
        'hgrn_lb_logits': nrm(ks[10], (DEPTH, HGRN_KW), 0.5),
        'hgrn_norm_w': 1.0 + nrm(ks[11], (DEPTH, HGRN_DV), 0.02),
        'w_up_a': nrm(ks[12], (DEPTH, NSA_WIDTH, D_MODEL), NSA_WIDTH ** -0.5),
        'w_up_b': nrm(ks[13], (DEPTH, HGRN_VW, D_MODEL), HGRN_VW ** -0.5),
        'w_out': nrm(ks[14], (DEPTH, D_MODEL, D_MODEL), D_MODEL ** -0.5),
        'norm_mlp_w': 1.0 + nrm(ks[15], (DEPTH, D_MODEL), 0.02),
        'w_mlp1': nrm(ks[16], (DEPTH, D_MODEL, D_FF), D_MODEL ** -0.5),
        'w_mlp2': nrm(ks[17], (DEPTH, D_FF, D_MODEL), D_FF ** -0.5),
        'final_norm_w': 1.0 + nrm(ks[18], (D_MODEL,), 0.02),
    }


def reference(x, c, positions, ada_w, ada_b, norm_mix_w, w_in, nsa_cmp_pos, nsa_cmp_w1, nsa_cmp_w2,
              hgrn_lb_logits, hgrn_norm_w, w_up_a, w_up_b, w_out, norm_mlp_w, w_mlp1, w_mlp2, final_norm_w):
    lb_all = jnp.cumsum(jax.nn.softmax(hgrn_lb_logits.astype(jnp.float32), axis=0), axis=0)
    lb_all = lb_all - lb_all[0:1]
    c_act = jax.nn.silu(c)
    for l in range(DEPTH):
        mod = c_act @ ada_w[l] + ada_b[l]
        sh1, sc1, g1, sh2, sc2, g2 = [m[:, None, :] for m in jnp.split(mod, 6, axis=-1)]
        h = rms_norm(x, norm_mix_w[l]) * (1.0 + sc1) + sh1
        x = x + g1 * hybrid_mixer(h, positions, w_in[l], nsa_cmp_pos[l], nsa_cmp_w1[l], nsa_cmp_w2[l],
                                  lb_all[l], hgrn_norm_w[l], w_up_a[l], w_up_b[l], w_out[l])
        h = rms_norm(x, norm_mlp_w[l]) * (1.0 + sc2) + sh2
        x = x + g2 * (jnp.square(jax.nn.relu(h @ w_mlp1[l])) @ w_mlp2[l])
    return rms_norm(x, final_norm_w)
```

```python
import functools
import math

import numpy as np
import jax
import jax.numpy as jnp
from jax import lax
from jax.experimental import pallas as pl
from jax.experimental.pallas import tpu as pltpu

D_MODEL = 4096
DEPTH = 2
NSA_HEADS = 16
NSA_KV_GROUPS = 4
NSA_HPG = NSA_HEADS // NSA_KV_GROUPS
HEAD_DIM = 128
CMP_LEN = 32
CMP_STRIDE = 16
SLC_BLOCK = 64
SLC_TOPK = 16
WINDOW = 512
ATTN_SCALE = HEAD_DIM ** -0.5
FORCED_BONUS = 1e6
NEG_INF = -1e30
HGRN_HEADS = 16
HGRN_DK = 128
HGRN_DV = 128
ROPE_THETA = 500000.0
ROT_DIM = HEAD_DIM // 4
D_FF = 4 * D_MODEL
NORM_EPS = 1e-6

NSA_WIDTH = NSA_HEADS * HEAD_DIM
KV_WIDTH = NSA_KV_GROUPS * HEAD_DIM
HGRN_KW = HGRN_HEADS * HGRN_DK
HGRN_VW = HGRN_HEADS * HGRN_DV
SEG_A = NSA_WIDTH + 6 * KV_WIDTH
SEG_G = 3 * NSA_HEADS
SEG_B = 2 * HGRN_KW + 2 * HGRN_VW + 2 * D_MODEL

LANES = 128
VMEM_LIMIT = 56 * 1024 * 1024

BF16 = jnp.bfloat16
F32 = jnp.float32


def _params(sem):
    return pltpu.CompilerParams(dimension_semantics=sem, vmem_limit_bytes=VMEM_LIMIT)


def _nt_dot(a, b, precision=None):
    return lax.dot_general(a, b, (((1,), (1,)), ((), ())), preferred_element_type=F32, precision=precision)


def _tn_dot(a, b, precision=None):
    return lax.dot_general(a, b, (((0,), (0,)), ((), ())), preferred_element_type=F32, precision=precision)


def _ada_kernel(cb_ref, w_ref, b_ref, o_ref, *, batch, tn):
    for b in range(batch):
        cb = cb_ref[b]
        cact = cb * jax.nn.sigmoid(cb)
        rows = []
        for s in range(tn // LANES):
            w = w_ref[:, s * LANES:(s + 1) * LANES]
            rows.append(jnp.sum(w * cact, axis=0, keepdims=True))
        o_ref[b:b + 1, :] = jnp.concatenate(rows, axis=1) + b_ref[...]


def _ada_mod(c, ada_w, ada_b):
    depth, k, n = ada_w.shape
    batch = c.shape[0]
    tn = 512
    cb = jnp.broadcast_to(c[:, :, None], (batch, k, LANES))
    out = pl.pallas_call(
        functools.partial(_ada_kernel, batch=batch, tn=tn),
        grid=(depth, n // tn),
        in_specs=[
            pl.BlockSpec((batch, k, LANES), lambda l, j: (0, 0, 0)),
            pl.BlockSpec((None, k, tn), lambda l, j: (l, 0, j)),
            pl.BlockSpec((None, 1, tn), lambda l, j: (l, 0, j)),
        ],
        out_specs=pl.BlockSpec((None, batch, tn), lambda l, j: (l, 0, j)),
        out_shape=jax.ShapeDtypeStruct((depth, batch, n), F32),
        compiler_params=_params(("parallel", "parallel")),
        name="ada_mod",
    )(cb, ada_w, ada_b.reshape(depth, 1, n))
    return out


def _norm_kernel(x_ref, w_ref, sc_ref, sh_ref, o_ref):
    x = x_ref[...]
    y = x * lax.rsqrt(jnp.mean(x * x, axis=-1, keepdims=True) + NORM_EPS)
    y = y * w_ref[...]
    y = y * (1.0 + sc_ref[...]) + sh_ref[...]
    o_ref[...] = y.astype(o_ref.dtype)


def _norm_mod(x2, w, sc, sh, seq, out_dtype):
    m, d = x2.shape
    batch = m // seq
    tm = 256
    per = seq // tm
    return pl.pallas_call(
        _norm_kernel,
        grid=(m // tm,),
        in_specs=[
            pl.BlockSpec((tm, d), lambda i: (i, 0)),
            pl.BlockSpec((1, d), lambda i: (0, 0)),
            pl.BlockSpec((None, 1, d), lambda i: (i // per, 0, 0)),
            pl.BlockSpec((None, 1, d), lambda i: (i // per, 0, 0)),
        ],
        out_specs=pl.BlockSpec((tm, d), lambda i: (i, 0)),
        out_shape=jax.ShapeDtypeStruct((m, d), out_dtype),
        compiler_params=_params(("parallel",)),
        name="norm_mod",
    )(x2, w.reshape(1, d), sc.reshape(batch, 1, d), sh.reshape(batch, 1, d))


def _mm_kernel(a_ref, w_ref, o_ref, acc_ref, *, relu2):
    k = pl.program_id(2)

    @pl.when(k == 0)
    def _():
        acc_ref[...] = jnp.zeros_like(acc_ref)

    acc_ref[...] += jnp.dot(a_ref[...].astype(BF16), w_ref[...].astype(BF16), preferred_element_type=F32)

    @pl.when(k == pl.num_programs(2) - 1)
    def _():
        r = acc_ref[...]
        if relu2:
            r = jnp.square(jnp.maximum(r, 0.0))
        o_ref[...] = r.astype(o_ref.dtype)


def _tile(n, pref):
    t = pref
    while n % t:
        t //= 2
    return t


def _matmul(a, w, out_dtype, relu2=False, tm=1024, tn=1024, tk=512):
    m, kd = a.shape
    n = w.shape[1]
    tm, tn, tk = _tile(m, tm), _tile(n, tn), _tile(kd, tk)
    return pl.pallas_call(
        functools.partial(_mm_kernel, relu2=relu2),
        grid=(m // tm, n // tn, kd // tk),
        in_specs=[
            pl.BlockSpec((tm, tk), lambda i, j, k: (i, k)),
            pl.BlockSpec((tk, tn), lambda i, j, k: (k, j)),
        ],
        out_specs=pl.BlockSpec((tm, tn), lambda i, j, k: (i, j)),
        out_shape=jax.ShapeDtypeStruct((m, n), out_dtype),
        scratch_shapes=[pltpu.VMEM((tm, tn), F32)],
        compiler_params=_params(("parallel", "parallel", "arbitrary")),
        name="matmul",
    )(a, w)


def _mm_resid_kernel(a_ref, w_ref, x_ref, g_ref, o_ref, acc_ref):
    k = pl.program_id(2)

    @pl.when(k == 0)
    def _():
        acc_ref[...] = jnp.zeros_like(acc_ref)

    acc_ref[...] += jnp.dot(a_ref[...].astype(BF16), w_ref[...].astype(BF16), preferred_element_type=F32)

    @pl.when(k == pl.num_programs(2) - 1)
    def _():
        o_ref[...] = x_ref[...] + g_ref[...] * acc_ref[...]


def _matmul_resid(a, w, x2, gate, seq, tm=1024, tn=1024, tk=512):
    m, kd = a.shape
    n = w.shape[1]
    tm, tn, tk = _tile(min(m, seq), tm), _tile(n, tn), _tile(kd, tk)
    per = seq // tm
    batch = m // seq
    return pl.pallas_call(
        _mm_resid_kernel,
        grid=(m // tm, n // tn, kd // tk),
        in_specs=[
            pl.BlockSpec((tm, tk), lambda i, j, k: (i, k)),
            pl.BlockSpec((tk, tn), lambda i, j, k: (k, j)),
            pl.BlockSpec((tm, tn), lambda i, j, k: (i, j)),
            pl.BlockSpec((None, 1, tn), lambda i, j, k: (i // per, 0, j)),
        ],
        out_specs=pl.BlockSpec((tm, tn), lambda i, j, k: (i, j)),
        out_shape=jax.ShapeDtypeStruct((m, n), F32),
        scratch_shapes=[pltpu.VMEM((tm, tn), F32)],
        compiler_params=_params(("parallel", "parallel", "arbitrary")),
        name="matmul_resid",
    )(a, w, x2, gate.reshape(batch, 1, n))


def _mm_merge_kernel(a_ref, b_ref, wa_ref, wb_ref, ga_ref, gb_ref, o_ref, acca_ref, accb_ref):
    k = pl.program_id(2)

    @pl.when(k == 0)
    def _():
        acca_ref[...] = jnp.zeros_like(acca_ref)
        accb_ref[...] = jnp.zeros_like(accb_ref)

    acca_ref[...] += jnp.dot(a_ref[...].astype(BF16), wa_ref[...].astype(BF16), preferred_element_type=F32)
    accb_ref[...] += jnp.dot(b_ref[...].astype(BF16), wb_ref[...].astype(BF16), preferred_element_type=F32)

    @pl.when(k == pl.num_programs(2) - 1)
    def _():
        y = jax.nn.sigmoid(ga_ref[...]) * acca_ref[...] + jax.nn.sigmoid(gb_ref[...]) * accb_ref[...]
        o_ref[...] = y.astype(o_ref.dtype)


def _matmul_merge(a, b, wa, wb, pb, ga_col, gb_col, tm=1024, tn=1024, tk=512):
    m, kd = a.shape
    n = wa.shape[1]
    tm, tn, tk = _tile(m, tm), _tile(n, tn), _tile(kd, tk)
    return pl.pallas_call(
        _mm_merge_kernel,
        grid=(m // tm, n // tn, kd // tk),
        in_specs=[
            pl.BlockSpec((tm, tk), lambda i, j, k: (i, k)),
            pl.BlockSpec((tm, tk), lambda i, j, k: (i, k)),
            pl.BlockSpec((tk, tn), lambda i, j, k: (k, j)),
            pl.BlockSpec((tk, tn), lambda i, j, k: (k, j)),
            pl.BlockSpec((tm, tn), lambda i, j, k: (i, ga_col // tn + j)),
            pl.BlockSpec((tm, tn), lambda i, j, k: (i, gb_col // tn + j)),
        ],
        out_specs=pl.BlockSpec((tm, tn), lambda i, j, k: (i, j)),
        out_shape=jax.ShapeDtypeStruct((m, n), BF16),
        scratch_shapes=[pltpu.VMEM((tm, tn), F32), pltpu.VMEM((tm, tn), F32)],
        compiler_params=_params(("parallel", "parallel", "arbitrary")),
        name="matmul_merge",
    )(a, b, wa, wb, pb, pb)


def _gelu_tanh(x):
    c = math.sqrt(2.0 / math.pi)
    return 0.5 * x * (1.0 + jnp.tanh(c * (x + 0.044715 * (x * x * x))))


def _compress_kernel(x_ref, pos_ref, w1_ref, w2_ref, o_ref, *, nhalf):
    hp = lax.Precision.HIGHEST
    half = CMP_LEN // 2
    acc_a = jnp.zeros((nhalf, HEAD_DIM), F32)
    acc_b = jnp.zeros((nhalf, HEAD_DIM), F32)
    for l in range(half):
        xl = x_ref[pl.ds(l, nhalf, stride=CMP_STRIDE), :]
        acc_a = acc_a + jnp.dot(xl + pos_ref[l:l + 1, :], w1_ref[l], precision=hp, preferred_element_type=F32)
        acc_b = acc_b + jnp.dot(xl + pos_ref[half + l:half + l + 1, :], w1_ref[half + l], precision=hp,
                                preferred_element_type=F32)
    hid = acc_a + pltpu.roll(acc_b, nhalf - 1, 0)
    hid = _gelu_tanh(hid)
    o_ref[...] = jnp.dot(hid, w2_ref[...], precision=hp, preferred_element_type=F32)


def _compress(pa, pos, w1, w2, batch, seq):
    nhalf = seq // CMP_STRIDE
    g = NSA_KV_GROUPS
    col0 = NSA_WIDTH // HEAD_DIM
    return pl.pallas_call(
        functools.partial(_compress_kernel, nhalf=nhalf),
        grid=(batch, g, 2),
        in_specs=[
            pl.BlockSpec((seq, HEAD_DIM), lambda b, gi, kv: (b, col0 + kv * g + gi)),
            pl.BlockSpec((None, CMP_LEN, HEAD_DIM), lambda b, gi, kv: (kv, 0, 0)),
            pl.BlockSpec((None, CMP_LEN, HEAD_DIM, HEAD_DIM), lambda b, gi, kv: (kv, 0, 0, 0)),
            pl.BlockSpec((None, HEAD_DIM, HEAD_DIM), lambda b, gi, kv: (kv, 0, 0)),
        ],
        out_specs=pl.BlockSpec((None, None, None, nhalf, HEAD_DIM), lambda b, gi, kv: (b, gi, kv, 0, 0)),
        out_shape=jax.ShapeDtypeStruct((batch, g, 2, nhalf, HEAD_DIM), F32),
        compiler_params=_params(("parallel", "parallel", "parallel")),
        name="nsa_compress",
    )(pa, pos, w1, w2)


def _cmp_slc_overlap_t(seq, ncp):
    nc = (seq - CMP_LEN) // CMP_STRIDE + 1
    nsel = seq // SLC_BLOCK
    cs = np.arange(nc) * CMP_STRIDE
    ce = cs + CMP_LEN - 1
    ss = np.arange(nsel) * SLC_BLOCK
    se = ss + SLC_BLOCK - 1
    ov = np.minimum(ce[:, None], se[None, :]) - np.maximum(cs[:, None], ss[None, :]) + 1
    ov = np.maximum(ov, 0).astype(np.float32)
    out = np.zeros((nsel, ncp), np.float32)
    out[:, :nc] = ov.T
    return out


def _cmp_attn_kernel(q_ref, kc_ref, vc_ref, ovt_ref, o_ref, sel_ref, *, tq, ncp, nsel):
    i = pl.program_id(2)
    t0 = i * tq
    tpos = t0 + lax.broadcasted_iota(jnp.int32, (tq, ncp), 0)
    nidx = lax.broadcasted_iota(jnp.int32, (tq, ncp), 1)
    vis = (nidx * CMP_STRIDE + (CMP_LEN - 1)) <= tpos
    kc = kc_ref[...].astype(BF16)
    vc = vc_ref[...].astype(BF16)
    imp = jnp.zeros((tq, ncp), F32)
    for j in range(NSA_HPG):
        qj = q_ref[:, j * HEAD_DIM:(j + 1) * HEAD_DIM].astype(BF16)
        s = _nt_dot(qj, kc) * ATTN_SCALE
        s = jnp.where(vis, s, NEG_INF)
        e = jnp.exp(s - jnp.max(s, axis=-1, keepdims=True))
        p = e / jnp.sum(e, axis=-1, keepdims=True)
        p = jnp.where(vis, p, 0.0)
        o_ref[:, j * HEAD_DIM:(j + 1) * HEAD_DIM] = jnp.dot(p.astype(BF16), vc, preferred_element_type=F32).astype(o_ref.dtype)
        imp = imp + p
    pslc = _nt_dot(ovt_ref[...], imp, precision=lax.Precision.HIGHEST)
    tl = t0 + lax.broadcasted_iota(jnp.int32, (nsel, tq), 1)
    mi = lax.broadcasted_iota(jnp.int32, (nsel, tq), 0)
    cur = tl // SLC_BLOCK
    valid = mi * SLC_BLOCK <= tl
    forced = (mi == 0) | (mi == cur) | (mi == cur - 1)
    score = jnp.where(valid, pslc + jnp.where(forced, FORCED_BONUS, 0.0), -jnp.inf)
    rank = jnp.zeros((nsel, tq), F32)
    for mp in range(nsel):
        row = score[mp:mp + 1, :]
        before = (row > score) | ((row == score) & (mi > mp))
        rank = rank + jnp.where(before, 1.0, 0.0)
    sel_t = jnp.where(valid & (rank < float(min(SLC_TOPK, nsel))), 1.0, 0.0).astype(BF16)
    eye = jnp.where(lax.broadcasted_iota(jnp.int32, (nsel, LANES), 0) == lax.broadcasted_iota(jnp.int32, (nsel, LANES), 1),
                    1.0, 0.0).astype(BF16)
    sel_ref[...] = _tn_dot(sel_t, eye).astype(sel_ref.dtype)


def _cmp_attn(pa, kvc, batch, seq, tq=256):
    ncp = seq // CMP_STRIDE
    nsel = seq // SLC_BLOCK
    g = NSA_KV_GROUPS
    nq = seq // tq
    gw = NSA_HPG * HEAD_DIM
    ovt = jnp.asarray(_cmp_slc_overlap_t(seq, ncp))
    return pl.pallas_call(
        functools.partial(_cmp_attn_kernel, tq=tq, ncp=ncp, nsel=nsel),
        grid=(batch, g, nq),
        in_specs=[
            pl.BlockSpec((tq, gw), lambda b, gi, i: (b * nq + i, gi)),
            pl.BlockSpec((None, None, None, ncp, HEAD_DIM), lambda b, gi, i: (b, gi, 0, 0, 0)),
            pl.BlockSpec((None, None, None, ncp, HEAD_DIM), lambda b, gi, i: (b, gi, 1, 0, 0)),
            pl.BlockSpec((nsel, ncp), lambda b, gi, i: (0, 0)),
        ],
        out_specs=[
            pl.BlockSpec((tq, gw), lambda b, gi, i: (b * nq + i, gi)),
            pl.BlockSpec((None, None, tq, LANES), lambda b, gi, i: (b, gi, i, 0)),
        ],
        out_shape=[
            jax.ShapeDtypeStruct((batch * seq, NSA_WIDTH), F32),
            jax.ShapeDtypeStruct((batch, g, seq, LANES), BF16),
        ],
        compiler_params=_params(("parallel", "parallel", "parallel")),
        name="nsa_cmp_attn",
    )(pa, kvc, kvc, ovt)


def _rope_tables(positions):
    half = ROT_DIM // 2
    inv = ROPE_THETA ** (-jnp.arange(0, ROT_DIM, 2, dtype=F32) / ROT_DIM)
    ang = positions.astype(F32)[..., None] * inv
    cos, sin = jnp.cos(ang), jnp.sin(ang)
    rest = HEAD_DIM - ROT_DIM
    b, s = positions.shape
    one = jnp.ones((b, s, rest), F32)
    zero = jnp.zeros((b, s, rest), F32)
    zh = jnp.zeros((b, s, half), F32)
    c = jnp.concatenate([cos, cos, one], axis=-1)
    s1 = jnp.concatenate([-sin, zh, zero], axis=-1)
    s2 = jnp.concatenate([zh, sin, zero], axis=-1)
    return [t.reshape(b * s, HEAD_DIM) for t in (c, s1, s2)]


def _rope_kernel(x_ref, c_ref, s1_ref, s2_ref, q_ref, kv_ref):
    c, s1, s2 = c_ref[...], s1_ref[...], s2_ref[...]
    half = ROT_DIM // 2

    def rot(x):
        return x * c + pltpu.roll(x, HEAD_DIM - half, 1) * s1 + pltpu.roll(x, half, 1) * s2

    nh = NSA_WIDTH // HEAD_DIM
    g = NSA_KV_GROUPS
    for h in range(nh):
        q_ref[:, h * HEAD_DIM:(h + 1) * HEAD_DIM] = rot(x_ref[:, h * HEAD_DIM:(h + 1) * HEAD_DIM]).astype(q_ref.dtype)
    base = NSA_WIDTH + 2 * KV_WIDTH
    for part in range(4):
        for gi in range(g):
            src = base + part * KV_WIDTH + gi * HEAD_DIM
            dst = part * KV_WIDTH + gi * HEAD_DIM
            x = x_ref[:, src:src + HEAD_DIM]
            if part % 2 == 0:
                x = rot(x)
            kv_ref[:, dst:dst + HEAD_DIM] = x.astype(kv_ref.dtype)


def _rope(pa, tables, tm=256):
    m = pa.shape[0]
    return pl.pallas_call(
        _rope_kernel,
        grid=(m // tm,),
        in_specs=[pl.BlockSpec((tm, SEG_A), lambda i: (i, 0))] + [pl.BlockSpec((tm, HEAD_DIM), lambda i: (i, 0))] * 3,
        out_specs=[pl.BlockSpec((tm, NSA_WIDTH), lambda i: (i, 0)), pl.BlockSpec((tm, 4 * KV_WIDTH), lambda i: (i, 0))],
        out_shape=[jax.ShapeDtypeStruct((m, NSA_WIDTH), BF16), jax.ShapeDtypeStruct((m, 4 * KV_WIDTH), BF16)],
        compiler_params=_params(("parallel",)),
        name="nsa_rope",
    )(pa, *tables)


def _flash_tiles(qs, k_ref, v_ref, lo, hi, mask_fn, m_ref, l_ref, acc_ref, *, tq, tk):
    m_ref[...] = jnp.full(m_ref.shape, NEG_INF, F32)
    l_ref[...] = jnp.zeros(l_ref.shape, F32)
    acc_ref[...] = jnp.zeros(acc_ref.shape, F32)
    nj = qs.shape[0] // tq

    def body(kt, carry):
        k0 = pl.multiple_of(kt * tk, tk)
        k = k_ref[pl.ds(k0, tk), :]
        v = v_ref[pl.ds(k0, tk), :]
        s = _nt_dot(qs, k) * ATTN_SCALE
        mask = mask_fn(kt)
        s3 = jnp.where(mask[None], s.reshape(nj, tq, tk), NEG_INF)
        m_old = m_ref[...]
        m_new = jnp.maximum(m_old, jnp.max(s3, axis=-1, keepdims=True))
        p = jnp.where(mask[None], jnp.exp(s3 - m_new), 0.0)
        alpha = jnp.exp(m_old - m_new)
        l_ref[...] = alpha * l_ref[...] + jnp.sum(p, axis=-1, keepdims=True)
        pv = jnp.dot(p.reshape(nj * tq, tk).astype(BF16), v, preferred_element_type=F32)
        acc_ref[...] = alpha * acc_ref[...] + pv.reshape(nj, tq, HEAD_DIM)
        m_ref[...] = m_new
        return carry

    lax.fori_loop(lo, hi, body, 0)
    return acc_ref[...] / l_ref[...]


def _sel_attn_kernel(q_ref, ks_ref, vs_ref, kw_ref, vw_ref, sel_ref, oc_ref, ng_ref, o_ref,
                     m_ref, l_ref, acc_ref, *, tq, tk, gate_cols):
    gi = pl.program_id(1)
    i = pl.program_id(2)
    t0 = i * tq
    nj = NSA_HPG
    qs = jnp.concatenate([q_ref[:, j * HEAD_DIM:(j + 1) * HEAD_DIM] for j in range(nj)], axis=0)
    row = t0 + lax.broadcasted_iota(jnp.int32, (tq, tk), 0)
    col = lax.broadcasted_iota(jnp.int32, (tq, tk), 1)
    sel = sel_ref[...]
    blk = lax.broadcasted_iota(jnp.int32, (LANES, tk), 0)
    kcol = lax.broadcasted_iota(jnp.int32, (LANES, tk), 1)

    def sel_mask(kt):
        expand = jnp.where((kt * tk + kcol) // SLC_BLOCK == blk, 1.0, 0.0).astype(BF16)
        chosen = jnp.dot(sel, expand, preferred_element_type=F32) > 0.5
        return chosen & ((kt * tk + col) <= row)

    def win_mask(kt):
        kpos = kt * tk + col
        return (kpos <= row) & (kpos > row - WINDOW)

    last = (t0 + tq - 1) // tk + 1
    o_slc = _flash_tiles(qs, ks_ref, vs_ref, 0, last, sel_mask, m_ref, l_ref, acc_ref, tq=tq, tk=tk)
    first = jnp.maximum(t0 - (WINDOW - 1), 0) // tk
    o_win = _flash_tiles(qs, kw_ref, vw_ref, first, last, win_mask, m_ref, l_ref, acc_ref, tq=tq, tk=tk)
    ng = jax.nn.sigmoid(ng_ref[...])
    lane = lax.broadcasted_iota(jnp.int32, (tq, LANES), 1)
    for j in range(nj):
        gates = []
        for br in range(3):
            c = br * NSA_HEADS + gi * nj + j
            gates.append(jnp.sum(jnp.where(lane == c, ng, 0.0), axis=-1, keepdims=True))
        o = (gates[0] * oc_ref[:, j * HEAD_DIM:(j + 1) * HEAD_DIM] + gates[1] * o_slc[j] + gates[2] * o_win[j])
        o_ref[:, j * HEAD_DIM:(j + 1) * HEAD_DIM] = o.astype(o_ref.dtype)


def _sel_attn(qr, kvr, sel, o_cmp, png, batch, seq, tq=128, tk=512):
    g = NSA_KV_GROUPS
    nq = seq // tq
    gw = NSA_HPG * HEAD_DIM
    tk = min(tk, seq)
    kv_spec = lambda part: pl.BlockSpec((seq, HEAD_DIM), lambda b, gi, i: (b, part * g + gi))
    return pl.pallas_call(
        functools.partial(_sel_attn_kernel, tq=tq, tk=tk, gate_cols=None),
        grid=(batch, g, nq),
        in_specs=[
            pl.BlockSpec((tq, gw), lambda b, gi, i: (b * nq + i, gi)),
            kv_spec(0), kv_spec(1), kv_spec(2), kv_spec(3),
            pl.BlockSpec((None, None, tq, LANES), lambda b, gi, i: (b, gi, i, 0)),
            pl.BlockSpec((tq, gw), lambda b, gi, i: (b * nq + i, gi)),
            pl.BlockSpec((tq, LANES), lambda b, gi, i: (b * nq + i, 0)),
        ],
        out_specs=pl.BlockSpec((tq, gw), lambda b, gi, i: (b * nq + i, gi)),
        out_shape=jax.ShapeDtypeStruct((batch * seq, NSA_WIDTH), BF16),
        scratch_shapes=[
            pltpu.VMEM((NSA_HPG, tq, 1), F32),
            pltpu.VMEM((NSA_HPG, tq, 1), F32),
            pltpu.VMEM((NSA_HPG, tq, HEAD_DIM), F32),
        ],
        compiler_params=_params(("parallel", "parallel", "parallel")),
        name="nsa_sel_win_attn",
    )(qr, kvr, kvr, kvr, kvr, sel, o_cmp, png)


HG_TILE = 128
HG_GROUP = 16
HG_LEVELS = (64, 32, 16)


def _hgrn_cum_matrix():
    t = HG_TILE
    low = np.tril(np.ones((t, t), np.float32))
    mats = [low]
    idx = np.arange(t)
    for h in HG_LEVELS:
        ref = (idx // (2 * h)) * (2 * h) + h - 1
        mats.append(low[ref])
    return np.concatenate(mats, axis=0)


def _split3(x):
    hi = x.astype(BF16)
    r = x - hi.astype(F32)
    mid = r.astype(BF16)
    lo = (r - mid.astype(F32)).astype(BF16)
    return hi, mid, lo


def _hgrn_kernel(q_ref, z_ref, v_ref, g_ref, lb_ref, nw_ref, cm_ref, o_ref, st_ref):
    t = HG_TILE

    @pl.when(pl.program_id(2) == 0)
    def _():
        st_ref[...] = jnp.zeros_like(st_ref)

    q = q_ref[...]
    z = z_ref[...]
    v = v_ref[...]
    lb = lb_ref[...]
    log_sig = jnp.minimum(z, 0.0) - jnp.log1p(jnp.exp(-jnp.abs(z)))
    a = jnp.log(lb)
    c = jnp.log1p(-lb) + log_sig
    log_f = jnp.maximum(a, c) + jnp.log1p(jnp.exp(-jnp.abs(a - c)))
    kk = (1.0 - lb) * jax.nn.sigmoid(-z)
    cm = cm_ref[...]
    hi, mid, lo = _split3(log_f)
    ball = (jnp.dot(cm, hi, preferred_element_type=F32) + jnp.dot(cm, mid, preferred_element_type=F32)
            + jnp.dot(cm, lo, preferred_element_type=F32))
    b = ball[0:t]
    ti = lax.broadcasted_iota(jnp.int32, (t, t), 0)
    si = lax.broadcasted_iota(jnp.int32, (t, t), 1)
    trow = lax.broadcasted_iota(jnp.int32, (t, HGRN_DK), 0)
    att = jnp.zeros((t, t), F32)
    for lv, h in enumerate(HG_LEVELS):
        bref = ball[(lv + 1) * t:(lv + 2) * t]
        is_q = (trow % (2 * h)) >= h
        e = jnp.exp(jnp.where(is_q, b - bref, bref - b))
        qt = jnp.where(is_q, q * e, 0.0).astype(BF16)
        kt = jnp.where(is_q, 0.0, kk * e).astype(BF16)
        att = att + jnp.where((ti // (2 * h)) == (si // (2 * h)), _nt_dot(qt, kt), 0.0)
    ng = t // HG_GROUP
    lane_d = lax.broadcasted_iota(jnp.int32, (HGRN_DK, t), 1)
    b3 = b.reshape(ng, HG_GROUP, HGRN_DK)
    k3 = kk.reshape(ng, HG_GROUP, HGRN_DK)
    diag = jnp.zeros((t, t), F32)
    for sg in range(HG_GROUP):
        bs = jnp.broadcast_to(b3[:, sg:sg + 1, :], b3.shape).reshape(t, HGRN_DK)
        ks = jnp.broadcast_to(k3[:, sg:sg + 1, :], k3.shape).reshape(t, HGRN_DK)
        p = (q * jnp.exp(jnp.minimum(b - bs, 0.0))) * ks
        spread = jnp.where((lane_d % HG_GROUP) == sg, 1.0, 0.0).astype(BF16)
        diag = diag + jnp.dot(p.astype(BF16), spread, preferred_element_type=F32)
    att = att + jnp.where(((ti // HG_GROUP) == (si // HG_GROUP)) & (si <= ti), diag, 0.0)
    o = jnp.dot(att.astype(BF16), v.astype(BF16), preferred_element_type=F32)
    st = st_ref[...]
    o = o + _nt_dot((q * jnp.exp(b)).astype(BF16), st.astype(BF16))
    b_last = b[t - 1:t, :]
    kdec = (kk * jnp.exp(b_last - b)).astype(BF16)
    st_ref[...] = st * jnp.exp(b_last) + _tn_dot(v.astype(BF16), kdec)
    y = o * lax.rsqrt(jnp.mean(o * o, axis=-1, keepdims=True) + NORM_EPS) * nw_ref[...]
    g = g_ref[...]
    o_ref[...] = (y * (g * jax.nn.sigmoid(g))).astype(o_ref.dtype)


def _hgrn(pb, lb, norm_w, batch, seq):
    hh = HGRN_HEADS
    nt = seq // HG_TILE
    cm = jnp.asarray(_hgrn_cum_matrix(), dtype=BF16)
    spec = lambda part: pl.BlockSpec((HG_TILE, HGRN_DK), lambda b, h, i: (b * nt + i, part * hh + h))
    return pl.pallas_call(
        _hgrn_kernel,
        grid=(batch, hh, nt),
        in_specs=[
            spec(0), spec(1), spec(2), spec(3),
            pl.BlockSpec((None, 1, HGRN_DK), lambda b, h, i: (h, 0, 0)),
            pl.BlockSpec((1, HGRN_DV), lambda b, h, i: (0, 0)),
            pl.BlockSpec(cm.shape, lambda b, h, i: (0, 0)),
        ],
        out_specs=pl.BlockSpec((HG_TILE, HGRN_DV), lambda b, h, i: (b * nt + i, h)),
        out_shape=jax.ShapeDtypeStruct((batch * seq, HGRN_VW), BF16),
        scratch_shapes=[pltpu.VMEM((HGRN_DV, HGRN_DK), F32)],
        compiler_params=_params(("parallel", "parallel", "arbitrary")),
        name="hgrn2",
    )(pb, pb, pb, pb, lb.reshape(hh, 1, HGRN_DK), norm_w.reshape(1, HGRN_DV), cm)


def _hybrid_mixer(h, tables, w_in, cmp_pos, cmp_w1, cmp_w2, lb, g_norm_w, w_up_a, w_up_b, batch, seq):
    pa = _matmul(h, w_in[:, :SEG_A], F32)
    png = _matmul(h, jnp.pad(w_in[:, SEG_A:SEG_A + SEG_G], ((0, 0), (0, LANES - SEG_G))), F32)
    pb = _matmul(h, w_in[:, SEG_A + SEG_G:], F32)
    kvc = _compress(pa, cmp_pos, cmp_w1, cmp_w2, batch, seq)
    o_cmp, sel = _cmp_attn(pa, kvc, batch, seq)
    qr, kvr = _rope(pa, tables)
    o_nsa = _sel_attn(qr, kvr, sel, o_cmp, png, batch, seq)
    o_hgrn = _hgrn(pb, lb, g_norm_w, batch, seq)
    return _matmul_merge(o_nsa, o_hgrn, w_up_a, w_up_b, pb, 2 * HGRN_KW + 2 * HGRN_VW,
                         2 * HGRN_KW + 2 * HGRN_VW + D_MODEL)


def kernel(x, c, positions, ada_w, ada_b, norm_mix_w, w_in, nsa_cmp_pos, nsa_cmp_w1, nsa_cmp_w2, hgrn_lb_logits,
           hgrn_norm_w, w_up_a, w_up_b, w_out, norm_mlp_w, w_mlp1, w_mlp2, final_norm_w):
    batch, seq, d = x.shape
    depth = ada_w.shape[0]
    lb_all = jnp.cumsum(jax.nn.softmax(hgrn_lb_logits.astype(F32), axis=0), axis=0)
    lb_all = lb_all - lb_all[0:1]
    mod = _ada_mod(c, ada_w, ada_b)
    tables = _rope_tables(positions)
    x2 = x.reshape(batch * seq, d)
    zeros = jnp.zeros((batch, d), F32)
    for l in range(depth):
        sh1, sc1, g1, sh2, sc2, g2 = [mod[l][:, k * d:(k + 1) * d] for k in range(6)]
        h = _norm_mod(x2, norm_mix_w[l], sc1, sh1, seq, BF16)
        y = _hybrid_mixer(h, tables, w_in[l], nsa_cmp_pos[l], nsa_cmp_w1[l], nsa_cmp_w2[l], lb_all[l],
                          hgrn_norm_w[l], w_up_a[l], w_up_b[l], batch, seq)
        x2 = _matmul_resid(y, w_out[l], x2, g1, seq)
        h = _norm_mod(x2, norm_mlp_w[l], sc2, sh2, seq, BF16)
        u = _matmul(h, w_mlp1[l], BF16, relu2=True)
        x2 = _matmul_resid(u, w_mlp2[l], x2, g2, seq)
    out = _norm_mod(x2, final_norm_w, zeros, zeros, seq, F32)
    return out.reshape(batch, seq, d)
```

```python
import functools
import math

import numpy as np
import jax
import jax.numpy as jnp
from jax import lax
from jax.experimental import pallas as pl
from jax.experimental.pallas import tpu as pltpu

D_MODEL = 4096
DEPTH = 2
NSA_HEADS = 16
NSA_KV_GROUPS = 4
NSA_HPG = NSA_HEADS // NSA_KV_GROUPS
HEAD_DIM = 128
CMP_LEN = 32
CMP_STRIDE = 16
SLC_BLOCK = 64
SLC_TOPK = 16
WINDOW = 512
ATTN_SCALE = HEAD_DIM ** -0.5
FORCED_BONUS = 1e6
NEG_INF = -1e30
HGRN_HEADS = 16
HGRN_DK = 128
HGRN_DV = 128
ROPE_THETA = 500000.0
ROT_DIM = HEAD_DIM // 4
D_FF = 4 * D_MODEL
NORM_EPS = 1e-6

NSA_WIDTH = NSA_HEADS * HEAD_DIM
KV_WIDTH = NSA_KV_GROUPS * HEAD_DIM
HGRN_KW = HGRN_HEADS * HGRN_DK
HGRN_VW = HGRN_HEADS * HGRN_DV
SEG_A = NSA_WIDTH + 6 * KV_WIDTH
SEG_G = 3 * NSA_HEADS
SEG_B = 2 * HGRN_KW + 2 * HGRN_VW + 2 * D_MODEL

LANES = 128
VMEM_LIMIT = 56 * 1024 * 1024

BF16 = jnp.bfloat16
F32 = jnp.float32


def _params(sem):
    return pltpu.CompilerParams(dimension_semantics=sem, vmem_limit_bytes=VMEM_LIMIT)


def _nt_dot(a, b, precision=None):
    return lax.dot_general(a, b, (((1,), (1,)), ((), ())), preferred_element_type=F32, precision=precision)


def _tn_dot(a, b, precision=None):
    return lax.dot_general(a, b, (((0,), (0,)), ((), ())), preferred_element_type=F32, precision=precision)


def _ada_kernel(cb_ref, w_ref, b_ref, o_ref, cact_ref, *, batch, tn):
    @pl.when((pl.program_id(0) == 0) & (pl.program_id(1) == 0))
    def _():
        cb = cb_ref[...]
        cact_ref[...] = cb * jax.nn.sigmoid(cb)

    for b in range(batch):
        cact = cact_ref[b]
        rows = []
        for s in range(tn // LANES):
            w = w_ref[:, s * LANES:(s + 1) * LANES]
            rows.append(jnp.sum(w * cact, axis=0, keepdims=True))
        o_ref[b:b + 1, :] = jnp.concatenate(rows, axis=1) + b_ref[...]


def _ada_mod(c, ada_w, ada_b):
    depth, k, n = ada_w.shape
    batch = c.shape[0]
    tn = 512
    cb = jnp.broadcast_to(c[:, :, None], (batch, k, LANES))
    out = pl.pallas_call(
        functools.partial(_ada_kernel, batch=batch, tn=tn),
        grid=(depth, n // tn),
        in_specs=[
            pl.BlockSpec((batch, k, LANES), lambda l, j: (0, 0, 0)),
            pl.BlockSpec((None, k, tn), lambda l, j: (l, 0, j)),
            pl.BlockSpec((None, 1, tn), lambda l, j: (l, 0, j)),
        ],
        out_specs=pl.BlockSpec((None, batch, tn), lambda l, j: (l, 0, j)),
        out_shape=jax.ShapeDtypeStruct((depth, batch, n), F32),
        scratch_shapes=[pltpu.VMEM((batch, k, LANES), F32)],
        compiler_params=_params(("arbitrary", "arbitrary")),
        name="ada_mod",
    )(cb, ada_w, ada_b.reshape(depth, 1, n))
    return out


def _norm_kernel(x_ref, w_ref, sc_ref, sh_ref, o_ref):
    x = x_ref[...]
    y = x * lax.rsqrt(jnp.mean(x * x, axis=-1, keepdims=True) + NORM_EPS)
    y = y * w_ref[...]
    y = y * (1.0 + sc_ref[...]) + sh_ref[...]
    o_ref[...] = y.astype(o_ref.dtype)


def _norm_mod(x2, w, sc, sh, seq, out_dtype):
    m, d = x2.shape
    batch = m // seq
    tm = 256
    per = seq // tm
    return pl.pallas_call(
        _norm_kernel,
        grid=(m // tm,),
        in_specs=[
            pl.BlockSpec((tm, d), lambda i: (i, 0)),
            pl.BlockSpec((1, d), lambda i: (0, 0)),
            pl.BlockSpec((None, 1, d), lambda i: (i // per, 0, 0)),
            pl.BlockSpec((None, 1, d), lambda i: (i // per, 0, 0)),
        ],
        out_specs=pl.BlockSpec((tm, d), lambda i: (i, 0)),
        out_shape=jax.ShapeDtypeStruct((m, d), out_dtype),
        compiler_params=_params(("parallel",)),
        name="norm_mod",
    )(x2, w.reshape(1, d), sc.reshape(batch, 1, d), sh.reshape(batch, 1, d))


MM_TM = 1024
MM_TN = 512
MM_TK = 4096


def _stage_weight(w_ref, wn_ref, wb_ref, shift):
    if shift == 0:
        wb_ref[...] = w_ref[...].astype(BF16)
        return
    tn = w_ref.shape[1]
    keep = lax.broadcasted_iota(jnp.int32, (w_ref.shape[0], LANES), 1) < (LANES - shift)
    for p in range(tn // LANES):
        cur = w_ref[:, p * LANES:(p + 1) * LANES]
        nxt = w_ref[:, (p + 1) * LANES:(p + 2) * LANES] if (p + 1) * LANES < tn else wn_ref[...]
        piece = jnp.where(keep, pltpu.roll(cur, LANES - shift, 1), pltpu.roll(nxt, LANES - shift, 1))
        wb_ref[:, p * LANES:(p + 1) * LANES] = piece.astype(BF16)


def _mm_kernel(*refs, shift, relu2):
    if shift:
        a_ref, w_ref, wn_ref, o_ref, wb_ref = refs
    else:
        a_ref, w_ref, o_ref, wb_ref = refs
        wn_ref = None

    @pl.when(pl.program_id(1) == 0)
    def _():
        _stage_weight(w_ref, wn_ref, wb_ref, shift)

    r = jnp.dot(a_ref[...], wb_ref[...], preferred_element_type=F32)
    if relu2:
        r = jnp.square(jnp.maximum(r, 0.0))
    o_ref[...] = r.astype(o_ref.dtype)


def _matmul(a, w, layer, col0, ncols, out_dtype, relu2=False, tn=MM_TN):
    m, kd = a.shape
    tm = min(MM_TM, m)
    shift = col0 % LANES
    base = col0 - shift
    assert base % tn == 0 and ncols % tn == 0 and m % tm == 0
    in_specs = [
        pl.BlockSpec((tm, kd), lambda j, i: (i, 0)),
        pl.BlockSpec((None, kd, tn), lambda j, i: (layer, 0, base // tn + j)),
    ]
    args = [a, w]
    if shift:
        in_specs.append(pl.BlockSpec((None, kd, LANES), lambda j, i: (layer, 0, (base + tn * (j + 1)) // LANES)))
        args.append(w)
    return pl.pallas_call(
        functools.partial(_mm_kernel, shift=shift, relu2=relu2),
        grid=(ncols // tn, m // tm),
        in_specs=in_specs,
        out_specs=pl.BlockSpec((tm, tn), lambda j, i: (i, j)),
        out_shape=jax.ShapeDtypeStruct((m, ncols), out_dtype),
        scratch_shapes=[pltpu.VMEM((kd, tn), BF16)],
        compiler_params=_params(("parallel", "arbitrary")),
        name="matmul",
    )(*args)


def _mm_resid_kernel(a_ref, w_ref, x_ref, g_ref, o_ref, wb_ref):
    @pl.when(pl.program_id(1) == 0)
    def _():
        _stage_weight(w_ref, None, wb_ref, 0)

    o_ref[...] = x_ref[...] + g_ref[...] * jnp.dot(a_ref[...], wb_ref[...], preferred_element_type=F32)


def _matmul_resid(a, w, layer, x2, gate, seq, tn=MM_TN):
    m, kd = a.shape
    n = w.shape[2]
    tm = min(MM_TM, seq)
    per = seq // tm
    batch = m // seq
    return pl.pallas_call(
        _mm_resid_kernel,
        grid=(n // tn, m // tm),
        in_specs=[
            pl.BlockSpec((tm, kd), lambda j, i: (i, 0)),
            pl.BlockSpec((None, kd, tn), lambda j, i: (layer, 0, j)),
            pl.BlockSpec((tm, tn), lambda j, i: (i, j)),
            pl.BlockSpec((None, 1, tn), lambda j, i: (i // per, 0, j)),
        ],
        out_specs=pl.BlockSpec((tm, tn), lambda j, i: (i, j)),
        out_shape=jax.ShapeDtypeStruct((m, n), F32),
        scratch_shapes=[pltpu.VMEM((kd, tn), BF16)],
        compiler_params=_params(("parallel", "arbitrary")),
        name="matmul_resid",
    )(a, w, x2, gate.reshape(batch, 1, n))


def _mm_ksplit_resid_kernel(a_ref, w_ref, x_ref, g_ref, o_ref, acc_ref):
    k = pl.program_id(2)
    part = jnp.dot(a_ref[...], w_ref[...].astype(BF16), preferred_element_type=F32)

    @pl.when(k == 0)
    def _():
        acc_ref[...] = part

    @pl.when(k > 0)
    def _():
        acc_ref[...] += part

    @pl.when(k == pl.num_programs(2) - 1)
    def _():
        o_ref[...] = x_ref[...] + g_ref[...] * acc_ref[...]


def _matmul_ksplit_resid(a, w, layer, x2, gate, seq, tn=MM_TN, tk=MM_TK):
    m, kd = a.shape
    n = w.shape[2]
    tm = min(MM_TM, seq)
    per = seq // tm
    batch = m // seq
    return pl.pallas_call(
        _mm_ksplit_resid_kernel,
        grid=(m // tm, n // tn, kd // tk),
        in_specs=[
            pl.BlockSpec((tm, tk), lambda i, j, k: (i, k)),
            pl.BlockSpec((None, tk, tn), lambda i, j, k: (layer, k, j)),
            pl.BlockSpec((tm, tn), lambda i, j, k: (i, j)),
            pl.BlockSpec((None, 1, tn), lambda i, j, k: (i // per, 0, j)),
        ],
        out_specs=pl.BlockSpec((tm, tn), lambda i, j, k: (i, j)),
        out_shape=jax.ShapeDtypeStruct((m, n), F32),
        scratch_shapes=[pltpu.VMEM((tm, tn), F32)],
        compiler_params=_params(("parallel", "parallel", "arbitrary")),
        name="matmul_ksplit_resid",
    )(a, w, x2, gate.reshape(batch, 1, n))


def _mm_merge_kernel(a_ref, b_ref, wa_ref, wb_ref, ga_ref, gb_ref, o_ref, wab_ref, wbb_ref):
    @pl.when(pl.program_id(1) == 0)
    def _():
        _stage_weight(wa_ref, None, wab_ref, 0)
        _stage_weight(wb_ref, None, wbb_ref, 0)

    ya = jnp.dot(a_ref[...], wab_ref[...], preferred_element_type=F32)
    yb = jnp.dot(b_ref[...], wbb_ref[...], preferred_element_type=F32)
    o_ref[...] = (jax.nn.sigmoid(ga_ref[...]) * ya + jax.nn.sigmoid(gb_ref[...]) * yb).astype(o_ref.dtype)


def _matmul_merge(a, b, wa, wb, layer, pb, ga_col, gb_col, tn=MM_TN):
    m, kd = a.shape
    n = wa.shape[2]
    tm = min(MM_TM, m)
    return pl.pallas_call(
        _mm_merge_kernel,
        grid=(n // tn, m // tm),
        in_specs=[
            pl.BlockSpec((tm, kd), lambda j, i: (i, 0)),
            pl.BlockSpec((tm, kd), lambda j, i: (i, 0)),
            pl.BlockSpec((None, kd, tn), lambda j, i: (layer, 0, j)),
            pl.BlockSpec((None, kd, tn), lambda j, i: (layer, 0, j)),
            pl.BlockSpec((tm, tn), lambda j, i: (i, ga_col // tn + j)),
            pl.BlockSpec((tm, tn), lambda j, i: (i, gb_col // tn + j)),
        ],
        out_specs=pl.BlockSpec((tm, tn), lambda j, i: (i, j)),
        out_shape=jax.ShapeDtypeStruct((m, n), BF16),
        scratch_shapes=[pltpu.VMEM((kd, tn), BF16), pltpu.VMEM((kd, tn), BF16)],
        compiler_params=_params(("parallel", "arbitrary")),
        name="matmul_merge",
    )(a, b, wa, wb, pb, pb)


def _gelu_tanh(x):
    c = math.sqrt(2.0 / math.pi)
    return 0.5 * x * (1.0 + jnp.tanh(c * (x + 0.044715 * (x * x * x))))


def _compress_one(x_ref, pos_ref, w1_ref, w2_ref, kv, nhalf):
    hp = lax.Precision.HIGHEST
    half = CMP_LEN // 2
    acc_a = jnp.zeros((nhalf, HEAD_DIM), F32)
    acc_b = jnp.zeros((nhalf, HEAD_DIM), F32)
    for l in range(half):
        xl = x_ref[pl.ds(l, nhalf, stride=CMP_STRIDE), :]
        acc_a = acc_a + jnp.dot(xl + pos_ref[kv, l:l + 1, :], w1_ref[kv, l], precision=hp, preferred_element_type=F32)
        acc_b = acc_b + jnp.dot(xl + pos_ref[kv, half + l:half + l + 1, :], w1_ref[kv, half + l], precision=hp,
                                preferred_element_type=F32)
    hid = acc_a + pltpu.roll(acc_b, nhalf - 1, 0)
    hid = _gelu_tanh(hid)
    return jnp.dot(hid, w2_ref[kv], precision=hp, preferred_element_type=F32)


def _compress_kernel(xk_ref, xv_ref, pos_ref, w1_ref, w2_ref, kc_ref, vct_ref, *, nhalf):
    kc_ref[...] = _compress_one(xk_ref, pos_ref, w1_ref, w2_ref, 0, nhalf)
    vct_ref[...] = _compress_one(xv_ref, pos_ref, w1_ref, w2_ref, 1, nhalf).T


def _compress(pa, pos, w1, w2, batch, seq):
    nhalf = seq // CMP_STRIDE
    g = NSA_KV_GROUPS
    col0 = NSA_WIDTH // HEAD_DIM
    full = lambda shape: pl.BlockSpec(shape, lambda b, gi: (0,) * len(shape))
    return pl.pallas_call(
        functools.partial(_compress_kernel, nhalf=nhalf),
        grid=(batch, g),
        in_specs=[
            pl.BlockSpec((seq, HEAD_DIM), lambda b, gi: (b, col0 + gi)),
            pl.BlockSpec((seq, HEAD_DIM), lambda b, gi: (b, col0 + g + gi)),
            full(pos.shape), full(w1.shape), full(w2.shape),
        ],
        out_specs=[
            pl.BlockSpec((None, None, nhalf, HEAD_DIM), lambda b, gi: (b, gi, 0, 0)),
            pl.BlockSpec((None, None, HEAD_DIM, nhalf), lambda b, gi: (b, gi, 0, 0)),
        ],
        out_shape=[
            jax.ShapeDtypeStruct((batch, g, nhalf, HEAD_DIM), F32),
            jax.ShapeDtypeStruct((batch, g, HEAD_DIM, nhalf), F32),
        ],
        compiler_params=_params(("parallel", "parallel")),
        name="nsa_compress",
    )(pa, pa, pos, w1, w2)


def _cmp_slc_overlap_t(seq, ncp):
    nc = (seq - CMP_LEN) // CMP_STRIDE + 1
    nsel = seq // SLC_BLOCK
    cs = np.arange(nc) * CMP_STRIDE
    ce = cs + CMP_LEN - 1
    ss = np.arange(nsel) * SLC_BLOCK
    se = ss + SLC_BLOCK - 1
    ov = np.minimum(ce[:, None], se[None, :]) - np.maximum(cs[:, None], ss[None, :]) + 1
    ov = np.maximum(ov, 0).astype(np.float32)
    out = np.zeros((nsel, ncp), np.float32)
    out[:, :nc] = ov.T
    return out


def _heads_t(x_ref, fn=None):
    parts = []
    for j in range(NSA_HPG):
        x = x_ref[:, j * HEAD_DIM:(j + 1) * HEAD_DIM]
        if fn is not None:
            x = fn(x)
        parts.append(x.T.astype(BF16))
    return jnp.concatenate(parts, axis=1)


def _cmp_attn_kernel(q_ref, kc_ref, vct_ref, ovt_ref, o_ref, sel_ref, *, tq, ncp, nsel):
    i = pl.program_id(2)
    t0 = i * tq
    tpos = t0 + lax.broadcasted_iota(jnp.int32, (ncp, tq), 1)
    nidx = lax.broadcasted_iota(jnp.int32, (ncp, tq), 0)
    vis = (nidx * CMP_STRIDE + (CMP_LEN - 1)) <= tpos
    qt = _heads_t(q_ref)
    st = jnp.dot(kc_ref[...].astype(BF16), qt, preferred_element_type=F32) * ATTN_SCALE
    vct = vct_ref[...].astype(BF16)
    imp = jnp.zeros((ncp, tq), F32)
    for j in range(NSA_HPG):
        s = jnp.where(vis, st[:, j * tq:(j + 1) * tq], NEG_INF)
        e = jnp.exp(s - jnp.max(s, axis=0, keepdims=True))
        p = e / jnp.sum(e, axis=0, keepdims=True)
        p = jnp.where(vis, p, 0.0)
        o_ref[j * HEAD_DIM:(j + 1) * HEAD_DIM, :] = jnp.dot(vct, p.astype(BF16), preferred_element_type=F32)
        imp = imp + p
    pslc = jnp.dot(ovt_ref[...], imp, precision=lax.Precision.HIGHEST, preferred_element_type=F32)
    tl = t0 + lax.broadcasted_iota(jnp.int32, (nsel, tq), 1)
    mi = lax.broadcasted_iota(jnp.int32, (nsel, tq), 0)
    cur = tl // SLC_BLOCK
    valid = mi * SLC_BLOCK <= tl
    forced = (mi == 0) | (mi == cur) | (mi == cur - 1)
    score = jnp.where(valid, pslc + jnp.where(forced, FORCED_BONUS, 0.0), -jnp.inf)
    rank = jnp.zeros((nsel, tq), F32)
    for mp in range(nsel):
        row = score[mp:mp + 1, :]
        before = (row > score) | ((row == score) & (mi > mp))
        rank = rank + jnp.where(before, 1.0, 0.0)
    sel_ref[...] = jnp.where(valid & (rank < float(min(SLC_TOPK, nsel))), 1.0, 0.0)


def _cmp_attn(pa, kc, vct, batch, seq, tq=256):
    ncp = seq // CMP_STRIDE
    nsel = seq // SLC_BLOCK
    g = NSA_KV_GROUPS
    nq = seq // tq
    gw = NSA_HPG * HEAD_DIM
    ovt = jnp.asarray(_cmp_slc_overlap_t(seq, ncp))
    return pl.pallas_call(
        functools.partial(_cmp_attn_kernel, tq=tq, ncp=ncp, nsel=nsel),
        grid=(batch, g, nq),
        in_specs=[
            pl.BlockSpec((tq, gw), lambda b, gi, i: (b * nq + i, gi)),
            pl.BlockSpec((None, None, ncp, HEAD_DIM), lambda b, gi, i: (b, gi, 0, 0)),
            pl.BlockSpec((None, None, HEAD_DIM, ncp), lambda b, gi, i: (b, gi, 0, 0)),
            pl.BlockSpec((nsel, ncp), lambda b, gi, i: (0, 0)),
        ],
        out_specs=[
            pl.BlockSpec((None, gw, tq), lambda b, gi, i: (b, gi, i)),
            pl.BlockSpec((None, None, nsel, tq), lambda b, gi, i: (b, gi, 0, i)),
        ],
        out_shape=[
            jax.ShapeDtypeStruct((batch, NSA_WIDTH, seq), F32),
            jax.ShapeDtypeStruct((batch, g, nsel, seq), F32),
        ],
        compiler_params=_params(("parallel", "parallel", "parallel")),
        name="nsa_cmp_attn",
    )(pa, kc, vct, ovt)


def _rope_tables(positions):
    half = ROT_DIM // 2
    inv = ROPE_THETA ** (-jnp.arange(0, ROT_DIM, 2, dtype=F32) / ROT_DIM)
    ang = positions.astype(F32)[..., None] * inv
    cos, sin = jnp.cos(ang), jnp.sin(ang)
    rest = HEAD_DIM - ROT_DIM
    b, s = positions.shape
    one = jnp.ones((b, s, rest), F32)
    zero = jnp.zeros((b, s, rest), F32)
    zh = jnp.zeros((b, s, half), F32)
    c = jnp.concatenate([cos, cos, one], axis=-1)
    s1 = jnp.concatenate([-sin, zh, zero], axis=-1)
    s2 = jnp.concatenate([zh, sin, zero], axis=-1)
    return [t.reshape(b * s, HEAD_DIM) for t in (c, s1, s2)]


def _rot(x, c, s1, s2):
    half = ROT_DIM // 2
    return x * c + pltpu.roll(x, HEAD_DIM - half, 1) * s1 + pltpu.roll(x, half, 1) * s2


def _kv_prep_kernel(xs_ref, xw_ref, c_ref, s1_ref, s2_ref, k_ref, vt_ref):
    c, s1, s2 = c_ref[...], s1_ref[...], s2_ref[...]
    for br, x_ref in enumerate((xs_ref, xw_ref)):
        for gi in range(NSA_KV_GROUPS):
            src = gi * HEAD_DIM
            dst = br * KV_WIDTH + gi * HEAD_DIM
            k_ref[:, dst:dst + HEAD_DIM] = _rot(x_ref[:, src:src + HEAD_DIM], c, s1, s2).astype(k_ref.dtype)
            vt_ref[dst:dst + HEAD_DIM, :] = x_ref[:, KV_WIDTH + src:KV_WIDTH + src + HEAD_DIM].T.astype(vt_ref.dtype)


def _kv_prep(pa, tables, batch, seq, tm=256):
    nt = seq // tm
    w = 2 * KV_WIDTH
    col = (NSA_WIDTH + 2 * KV_WIDTH) // w
    assert col * w == NSA_WIDTH + 2 * KV_WIDTH
    return pl.pallas_call(
        _kv_prep_kernel,
        grid=(batch, nt),
        in_specs=[pl.BlockSpec((tm, w), lambda b, i: (b * nt + i, col)),
                  pl.BlockSpec((tm, w), lambda b, i: (b * nt + i, col + 1))]
        + [pl.BlockSpec((tm, HEAD_DIM), lambda b, i: (b * nt + i, 0))] * 3,
        out_specs=[
            pl.BlockSpec((tm, 2 * KV_WIDTH), lambda b, i: (b * nt + i, 0)),
            pl.BlockSpec((None, 2 * KV_WIDTH, tm), lambda b, i: (b, 0, i)),
        ],
        out_shape=[
            jax.ShapeDtypeStruct((batch * seq, 2 * KV_WIDTH), BF16),
            jax.ShapeDtypeStruct((batch, 2 * KV_WIDTH, seq), BF16),
        ],
        compiler_params=_params(("parallel", "parallel")),
        name="nsa_kv_prep",
    )(pa, pa, *tables)


def _flash_tiles_t(qt, k_ref, vt_ref, lo, hi, mask_fn, m_ref, l_ref, acc_ref, *, tk):
    m_ref[...] = jnp.full(m_ref.shape, NEG_INF, F32)
    l_ref[...] = jnp.zeros(l_ref.shape, F32)
    acc_ref[...] = jnp.zeros(acc_ref.shape, F32)
    scale2 = ATTN_SCALE * math.log2(math.e)

    def body(kt, carry):
        k0 = pl.multiple_of(kt * tk, tk)
        k = k_ref[pl.ds(k0, tk), :]
        vt = vt_ref[:, pl.ds(k0, tk)]
        s = jnp.dot(k, qt, preferred_element_type=F32) * scale2
        mask = mask_fn(kt)
        mask = jnp.concatenate([mask] * NSA_HPG, axis=1)
        s = jnp.where(mask, s, NEG_INF)
        m_old = m_ref[...]
        m_new = jnp.maximum(m_old, jnp.max(s, axis=0, keepdims=True))
        m_use = jnp.where(m_new > 0.5 * NEG_INF, m_new, 0.0)
        p = jnp.exp2(s - m_use)
        alpha = jnp.exp2(m_old - m_new)
        l_ref[...] = alpha * l_ref[...] + jnp.sum(p, axis=0, keepdims=True)
        acc_ref[...] = alpha * acc_ref[...] + jnp.dot(vt, p.astype(BF16), preferred_element_type=F32)
        m_ref[...] = m_new
        return carry

    lax.fori_loop(lo, hi, body, 0)
    return acc_ref[...] / l_ref[...]


def _sel_attn_kernel(q_ref, c_ref, s1_ref, s2_ref, ks_ref, kw_ref, vst_ref, vwt_ref, sel_ref, oct_ref, ng_ref, o_ref,
                     m_ref, l_ref, acc_ref, gt_ref, *, tq, tk):
    gi = pl.program_id(1)
    i = pl.program_id(2)
    t0 = i * tq
    nj = NSA_HPG
    c, s1, s2 = c_ref[...], s1_ref[...], s2_ref[...]
    qt = _heads_t(q_ref, lambda x: _rot(x, c, s1, s2))
    kpos = lax.broadcasted_iota(jnp.int32, (tk, tq), 0)
    tpos = t0 + lax.broadcasted_iota(jnp.int32, (tk, tq), 1)
    nb = tk // SLC_BLOCK

    def sel_mask(kt):
        rows = sel_ref[pl.ds(pl.multiple_of(kt * nb, nb), nb), :]
        chosen = jnp.broadcast_to(rows[:, None, :], (nb, SLC_BLOCK, tq)).reshape(tk, tq) > 0.5
        return chosen & ((kt * tk + kpos) <= tpos)

    def win_mask(kt):
        kp = kt * tk + kpos
        return (kp <= tpos) & (kp > tpos - WINDOW)

    last = (t0 + tq - 1) // tk + 1
    o_slc = _flash_tiles_t(qt, ks_ref, vst_ref, 0, last, sel_mask, m_ref, l_ref, acc_ref, tk=tk)
    first = jnp.maximum(t0 - (WINDOW - 1), 0) // tk
    o_win = _flash_tiles_t(qt, kw_ref, vwt_ref, first, last, win_mask, m_ref, l_ref, acc_ref, tk=tk)
    gt_ref[...] = jax.nn.sigmoid(ng_ref[...].T)
    for j in range(nj):
        gate = [gt_ref[pl.ds(br * NSA_HEADS + gi * nj + j, 1), :] for br in range(3)]
        ot = (gate[0] * oct_ref[j * HEAD_DIM:(j + 1) * HEAD_DIM, :] + gate[1] * o_slc[:, j * tq:(j + 1) * tq]
              + gate[2] * o_win[:, j * tq:(j + 1) * tq])
        o_ref[:, j * HEAD_DIM:(j + 1) * HEAD_DIM] = ot.T.astype(o_ref.dtype)


def _sel_attn(pa, tables, kn, vt, sel, o_cmp_t, png, batch, seq, tq=256, tk=512):
    g = NSA_KV_GROUPS
    nq = seq // tq
    gw = NSA_HPG * HEAD_DIM
    tk = min(tk, seq)
    nsel = seq // SLC_BLOCK
    n = NSA_HPG * tq
    tab = pl.BlockSpec((tq, HEAD_DIM), lambda b, gi, i: (b * nq + i, 0))
    return pl.pallas_call(
        functools.partial(_sel_attn_kernel, tq=tq, tk=tk),
        grid=(batch, g, nq),
        in_specs=[
            pl.BlockSpec((tq, gw), lambda b, gi, i: (b * nq + i, gi)),
            tab, tab, tab,
            pl.BlockSpec((seq, HEAD_DIM), lambda b, gi, i: (b, gi)),
            pl.BlockSpec((seq, HEAD_DIM), lambda b, gi, i: (b, g + gi)),
            pl.BlockSpec((None, HEAD_DIM, seq), lambda b, gi, i: (b, gi, 0)),
            pl.BlockSpec((None, HEAD_DIM, seq), lambda b, gi, i: (b, g + gi, 0)),
            pl.BlockSpec((None, None, nsel, tq), lambda b, gi, i: (b, gi, 0, i)),
            pl.BlockSpec((None, gw, tq), lambda b, gi, i: (b, gi, i)),
            pl.BlockSpec((tq, LANES), lambda b, gi, i: (b * nq + i, 0)),
        ],
        out_specs=pl.BlockSpec((tq, gw), lambda b, gi, i: (b * nq + i, gi)),
        out_shape=jax.ShapeDtypeStruct((batch * seq, NSA_WIDTH), BF16),
        scratch_shapes=[
            pltpu.VMEM((1, n), F32),
            pltpu.VMEM((1, n), F32),
            pltpu.VMEM((HEAD_DIM, n), F32),
            pltpu.VMEM((LANES, tq), F32),
        ],
        compiler_params=_params(("parallel", "parallel", "parallel")),
        name="nsa_sel_win_attn",
    )(pa, *tables, kn, kn, vt, vt, sel, o_cmp_t, png)


HG_TILE = 128
HG_GROUP = 8
HG_LEVELS = (64, 32, 16, 8)
HG_HEADS_PER_STEP = 4


def _split3(x):
    hi = x.astype(BF16)
    r = x - hi.astype(F32)
    mid = r.astype(BF16)
    lo = (r - mid.astype(F32)).astype(BF16)
    return hi, mid, lo


def _group_row(x, size, row):
    t, d = x.shape
    x3 = x.reshape(t // size, size, d)
    return jnp.broadcast_to(x3[:, row:row + 1, :], x3.shape).reshape(t, d)


def _hgrn_head(q, z, v, g, lb, nw, low, st):
    t = HG_TILE
    log_sig = jnp.minimum(z, 0.0) - jnp.log1p(jnp.exp(-jnp.abs(z)))
    a = jnp.log(lb)
    c = jnp.log1p(-lb) + log_sig
    log_f = jnp.maximum(a, c) + jnp.log1p(jnp.exp(-jnp.abs(a - c)))
    kk = (1.0 - lb) * jax.nn.sigmoid(-z)
    hi, mid, lo = _split3(log_f)
    b = (jnp.dot(low, hi, preferred_element_type=F32) + jnp.dot(low, mid, preferred_element_type=F32)
         + jnp.dot(low, lo, preferred_element_type=F32))
    ti = lax.broadcasted_iota(jnp.int32, (t, t), 0)
    si = lax.broadcasted_iota(jnp.int32, (t, t), 1)
    trow = lax.broadcasted_iota(jnp.int32, (t, HGRN_DK), 0)
    att = jnp.zeros((t, t), F32)
    for h in HG_LEVELS:
        bref = _group_row(b, 2 * h, h - 1)
        is_q = (trow % (2 * h)) >= h
        e = jnp.exp(jnp.where(is_q, b - bref, bref - b))
        qt = jnp.where(is_q, q * e, 0.0).astype(BF16)
        kt = jnp.where(is_q, 0.0, kk * e).astype(BF16)
        att = att + jnp.where((ti // (2 * h)) == (si // (2 * h)), _nt_dot(qt, kt), 0.0)
    lane_d = lax.broadcasted_iota(jnp.int32, (HGRN_DK, t), 1)
    diag = jnp.zeros((t, t), F32)
    for sg in range(HG_GROUP):
        bs = _group_row(b, HG_GROUP, sg)
        ks = _group_row(kk, HG_GROUP, sg)
        p = (q * jnp.exp(jnp.minimum(b - bs, 0.0))) * ks
        spread = jnp.where((lane_d % HG_GROUP) == sg, 1.0, 0.0).astype(BF16)
        diag = diag + jnp.dot(p.astype(BF16), spread, preferred_element_type=F32)
    att = att + jnp.where(((ti // HG_GROUP) == (si // HG_GROUP)) & (si <= ti), diag, 0.0)
    vb = v.astype(BF16)
    o = jnp.dot(att.astype(BF16), vb, preferred_element_type=F32)
    o = o + _nt_dot((q * jnp.exp(b)).astype(BF16), st.astype(BF16))
    b_last = b[t - 1:t, :]
    kdec = (kk * jnp.exp(b_last - b)).astype(BF16)
    st_new = st * jnp.exp(b_last) + _tn_dot(vb, kdec)
    y = o * lax.rsqrt(jnp.mean(o * o, axis=-1, keepdims=True) + NORM_EPS) * nw
    return y * (g * jax.nn.sigmoid(g)), st_new


def _hgrn_kernel(q_ref, z_ref, v_ref, g_ref, lb_ref, nw_ref, low_ref, o_ref, st_ref, *, nh):
    @pl.when(pl.program_id(2) == 0)
    def _():
        st_ref[...] = jnp.zeros_like(st_ref)

    low = low_ref[...]
    nw = nw_ref[...]
    for hd in range(nh):
        cs = slice(hd * HGRN_DK, (hd + 1) * HGRN_DK)
        y, st_new = _hgrn_head(q_ref[:, cs], z_ref[:, cs], v_ref[:, cs], g_ref[:, cs], lb_ref[hd:hd + 1, :], nw, low,
                               st_ref[hd])
        st_ref[hd] = st_new
        o_ref[:, cs] = y.astype(o_ref.dtype)


def _hgrn(pb, lb, norm_w, batch, seq):
    hh = HGRN_HEADS
    nh = HG_HEADS_PER_STEP
    nt = seq // HG_TILE
    low = jnp.asarray(np.tril(np.ones((HG_TILE, HG_TILE), np.float32)), dtype=BF16)
    spec = lambda part: pl.BlockSpec((HG_TILE, nh * HGRN_DK), lambda b, h, i: (b * nt + i, part * (hh // nh) + h))
    return pl.pallas_call(
        functools.partial(_hgrn_kernel, nh=nh),
        grid=(batch, hh // nh, nt),
        in_specs=[
            spec(0), spec(1), spec(2), spec(3),
            pl.BlockSpec((None, nh, HGRN_DK), lambda b, h, i: (h, 0, 0)),
            pl.BlockSpec((1, HGRN_DV), lambda b, h, i: (0, 0)),
            pl.BlockSpec(low.shape, lambda b, h, i: (0, 0)),
        ],
        out_specs=pl.BlockSpec((HG_TILE, nh * HGRN_DV), lambda b, h, i: (b * nt + i, h)),
        out_shape=jax.ShapeDtypeStruct((batch * seq, HGRN_VW), BF16),
        scratch_shapes=[pltpu.VMEM((nh, HGRN_DV, HGRN_DK), F32)],
        compiler_params=_params(("parallel", "parallel", "arbitrary")),
        name="hgrn2",
    )(pb, pb, pb, pb, lb.reshape(hh // nh, nh, HGRN_DK), norm_w.reshape(1, HGRN_DV), low)


def _hybrid_mixer(h, tables, layer, w_in, cmp_pos, cmp_w1, cmp_w2, lb, g_norm_w, w_up_a, w_up_b, batch, seq):
    pa = _matmul(h, w_in, layer, 0, SEG_A, F32)
    png = _matmul(h, w_in, layer, SEG_A, LANES, F32, tn=LANES)
    pb = _matmul(h, w_in, layer, SEG_A + SEG_G, SEG_B, F32)
    kc, vct = _compress(pa, cmp_pos, cmp_w1, cmp_w2, batch, seq)
    o_cmp_t, sel = _cmp_attn(pa, kc, vct, batch, seq)
    kn, vt = _kv_prep(pa, tables, batch, seq)
    o_nsa = _sel_attn(pa, tables, kn, vt, sel, o_cmp_t, png, batch, seq)
    o_hgrn = _hgrn(pb, lb, g_norm_w, batch, seq)
    return _matmul_merge(o_nsa, o_hgrn, w_up_a, w_up_b, layer, pb, 2 * HGRN_KW + 2 * HGRN_VW,
                         2 * HGRN_KW + 2 * HGRN_VW + D_MODEL)


def kernel(x, c, positions, ada_w, ada_b, norm_mix_w, w_in, nsa_cmp_pos, nsa_cmp_w1, nsa_cmp_w2, hgrn_lb_logits,
           hgrn_norm_w, w_up_a, w_up_b, w_out, norm_mlp_w, w_mlp1, w_mlp2, final_norm_w):
    batch, seq, d = x.shape
    depth = ada_w.shape[0]
    lb_all = jnp.cumsum(jax.nn.softmax(hgrn_lb_logits.astype(F32), axis=0), axis=0)
    lb_all = lb_all - lb_all[0:1]
    mod = _ada_mod(c, ada_w, ada_b)
    tables = _rope_tables(positions)
    x2 = x.reshape(batch * seq, d)
    zeros = jnp.zeros((batch, d), F32)
    for l in range(depth):
        sh1, sc1, g1, sh2, sc2, g2 = [mod[l][:, k * d:(k + 1) * d] for k in range(6)]
        h = _norm_mod(x2, norm_mix_w[l], sc1, sh1, seq, BF16)
        y = _hybrid_mixer(h, tables, l, w_in, nsa_cmp_pos[l], nsa_cmp_w1[l], nsa_cmp_w2[l], lb_all[l],
                          hgrn_norm_w[l], w_up_a, w_up_b, batch, seq)
        x2 = _matmul_resid(y, w_out, l, x2, g1, seq)
        h = _norm_mod(x2, norm_mlp_w[l], sc2, sh2, seq, BF16)
        u = _matmul(h, w_mlp1, l, 0, D_FF, BF16, relu2=True)
        x2 = _matmul_ksplit_resid(u, w_mlp2, l, x2, g2, seq)
    out = _norm_mod(x2, final_norm_w, zeros, zeros, seq, F32)
    return out.reshape(batch, seq, d)
```

```python
import functools
import math

import numpy as np
import jax
import jax.numpy as jnp
from jax import lax
from jax.experimental import pallas as pl
from jax.experimental.pallas import tpu as pltpu

D_MODEL = 4096
DEPTH = 2
NSA_HEADS = 16
NSA_KV_GROUPS = 4
NSA_HPG = NSA_HEADS // NSA_KV_GROUPS
HEAD_DIM = 128
CMP_LEN = 32
CMP_STRIDE = 16
SLC_BLOCK = 64
SLC_TOPK = 16
WINDOW = 512
ATTN_SCALE = HEAD_DIM ** -0.5
FORCED_BONUS = 1e6
NEG_INF = -1e30
HGRN_HEADS = 16
HGRN_DK = 128
HGRN_DV = 128
ROPE_THETA = 500000.0
ROT_DIM = HEAD_DIM // 4
D_FF = 4 * D_MODEL
NORM_EPS = 1e-6

NSA_WIDTH = NSA_HEADS * HEAD_DIM
KV_WIDTH = NSA_KV_GROUPS * HEAD_DIM
HGRN_KW = HGRN_HEADS * HGRN_DK
HGRN_VW = HGRN_HEADS * HGRN_DV
SEG_A = NSA_WIDTH + 6 * KV_WIDTH
SEG_G = 3 * NSA_HEADS
SEG_B = 2 * HGRN_KW + 2 * HGRN_VW + 2 * D_MODEL

LANES = 128
VMEM_LIMIT = 56 * 1024 * 1024

BF16 = jnp.bfloat16
F32 = jnp.float32


def _params(sem):
    return pltpu.CompilerParams(dimension_semantics=sem, vmem_limit_bytes=VMEM_LIMIT)


def _nt_dot(a, b, precision=None):
    return lax.dot_general(a, b, (((1,), (1,)), ((), ())), preferred_element_type=F32, precision=precision)


def _tn_dot(a, b, precision=None):
    return lax.dot_general(a, b, (((0,), (0,)), ((), ())), preferred_element_type=F32, precision=precision)


def _ada_kernel(cb_ref, w_ref, b_ref, o_ref, cact_ref, *, batch, tn):
    @pl.when((pl.program_id(0) == 0) & (pl.program_id(1) == 0))
    def _():
        cb = cb_ref[...]
        cact_ref[...] = cb * jax.nn.sigmoid(cb)

    for b in range(batch):
        cact = cact_ref[b]
        rows = []
        for s in range(tn // LANES):
            w = w_ref[:, s * LANES:(s + 1) * LANES]
            rows.append(jnp.sum(w * cact, axis=0, keepdims=True))
        o_ref[b:b + 1, :] = jnp.concatenate(rows, axis=1) + b_ref[...]


def _ada_mod(c, ada_w, ada_b):
    depth, k, n = ada_w.shape
    batch = c.shape[0]
    tn = 512
    cb = jnp.broadcast_to(c[:, :, None], (batch, k, LANES))
    out = pl.pallas_call(
        functools.partial(_ada_kernel, batch=batch, tn=tn),
        grid=(depth, n // tn),
        in_specs=[
            pl.BlockSpec((batch, k, LANES), lambda l, j: (0, 0, 0)),
            pl.BlockSpec((None, k, tn), lambda l, j: (l, 0, j)),
            pl.BlockSpec((None, 1, tn), lambda l, j: (l, 0, j)),
        ],
        out_specs=pl.BlockSpec((None, batch, tn), lambda l, j: (l, 0, j)),
        out_shape=jax.ShapeDtypeStruct((depth, batch, n), F32),
        scratch_shapes=[pltpu.VMEM((batch, k, LANES), F32)],
        compiler_params=_params(("arbitrary", "arbitrary")),
        name="ada_mod",
    )(cb, ada_w, ada_b.reshape(depth, 1, n))
    return out


def _norm_kernel(x_ref, w_ref, sc_ref, sh_ref, o_ref):
    x = x_ref[...]
    y = x * lax.rsqrt(jnp.mean(x * x, axis=-1, keepdims=True) + NORM_EPS)
    y = y * w_ref[...]
    y = y * (1.0 + sc_ref[...]) + sh_ref[...]
    o_ref[...] = y.astype(o_ref.dtype)


def _norm_mod(x2, w, sc, sh, seq, out_dtype):
    m, d = x2.shape
    batch = m // seq
    tm = 256
    per = seq // tm
    return pl.pallas_call(
        _norm_kernel,
        grid=(m // tm,),
        in_specs=[
            pl.BlockSpec((tm, d), lambda i: (i, 0)),
            pl.BlockSpec((1, d), lambda i: (0, 0)),
            pl.BlockSpec((None, 1, d), lambda i: (i // per, 0, 0)),
            pl.BlockSpec((None, 1, d), lambda i: (i // per, 0, 0)),
        ],
        out_specs=pl.BlockSpec((tm, d), lambda i: (i, 0)),
        out_shape=jax.ShapeDtypeStruct((m, d), out_dtype),
        compiler_params=_params(("parallel",)),
        name="norm_mod",
    )(x2, w.reshape(1, d), sc.reshape(batch, 1, d), sh.reshape(batch, 1, d))


MM_TM = 1024
MM_TN = 512
MM_TK = 4096


def _mm_kernel(a_ref, w_ref, o_ref, wb_ref, *, relu2):
    @pl.when(pl.program_id(1) == 0)
    def _():
        wb_ref[...] = w_ref[...].astype(BF16)

    r = jnp.dot(a_ref[...], wb_ref[...], preferred_element_type=F32)
    if relu2:
        r = jnp.square(jnp.maximum(r, 0.0))
    o_ref[...] = r.astype(o_ref.dtype)


def _matmul(a, w, layer, out_dtype, relu2=False, tn=MM_TN):
    m, kd = a.shape
    n = w.shape[2]
    tm = min(MM_TM, m)
    assert n % tn == 0 and m % tm == 0
    return pl.pallas_call(
        functools.partial(_mm_kernel, relu2=relu2),
        grid=(n // tn, m // tm),
        in_specs=[
            pl.BlockSpec((tm, kd), lambda j, i: (i, 0)),
            pl.BlockSpec((None, kd, tn), lambda j, i: (layer, 0, j)),
        ],
        out_specs=pl.BlockSpec((tm, tn), lambda j, i: (i, j)),
        out_shape=jax.ShapeDtypeStruct((m, n), out_dtype),
        scratch_shapes=[pltpu.VMEM((kd, tn), BF16)],
        compiler_params=_params(("parallel", "arbitrary")),
        name="matmul",
    )(a, w)


def _mm_t_kernel(*refs, shift):
    if shift:
        a_ref, w_ref, wn_ref, o_ref, wb_ref = refs
    else:
        a_ref, w_ref, o_ref, wb_ref = refs

    @pl.when(pl.program_id(1) == 0)
    def _():
        tn = w_ref.shape[0]
        wb_ref[0:tn - shift, :] = w_ref[shift:tn, :].astype(BF16)
        if shift:
            wb_ref[tn - shift:tn, :] = wn_ref[0:shift, :].astype(BF16)

    o_ref[...] = _nt_dot(a_ref[...], wb_ref[...]).astype(o_ref.dtype)


BF16_SUBLANES = 16


def _matmul_t(a, wt, layer, row0, nrows, out_dtype, tn=MM_TN):
    m, kd = a.shape
    tm = min(MM_TM, m)
    shift = row0 % tn
    base = row0 - shift
    nxt = 64
    assert nrows % tn == 0 and m % tm == 0 and shift % BF16_SUBLANES == 0 and shift <= nxt and tn % nxt == 0
    in_specs = [
        pl.BlockSpec((tm, kd), lambda j, i: (i, 0)),
        pl.BlockSpec((None, tn, kd), lambda j, i: (layer, base // tn + j, 0)),
    ]
    args = [a, wt]
    if shift:
        in_specs.append(pl.BlockSpec((None, nxt, kd), lambda j, i: (layer, (base + tn * (j + 1)) // nxt, 0)))
        args.append(wt)
    return pl.pallas_call(
        functools.partial(_mm_t_kernel, shift=shift),
        grid=(nrows // tn, m // tm),
        in_specs=in_specs,
        out_specs=pl.BlockSpec((tm, tn), lambda j, i: (i, j)),
        out_shape=jax.ShapeDtypeStruct((m, nrows), out_dtype),
        scratch_shapes=[pltpu.VMEM((tn, kd), BF16)],
        compiler_params=_params(("parallel", "arbitrary")),
        name="matmul_t",
    )(*args)


def _mm_resid_kernel(a_ref, w_ref, x_ref, g_ref, o_ref, wb_ref):
    @pl.when(pl.program_id(1) == 0)
    def _():
        wb_ref[...] = w_ref[...].astype(BF16)

    o_ref[...] = x_ref[...] + g_ref[...] * jnp.dot(a_ref[...], wb_ref[...], preferred_element_type=F32)


def _matmul_resid(a, w, layer, x2, gate, seq, tn=MM_TN, tk=None, ks=0):
    m = a.shape[0]
    kd = tk if tk is not None else a.shape[1]
    n = w.shape[2]
    tm = min(MM_TM, seq)
    per = seq // tm
    batch = m // seq
    return pl.pallas_call(
        _mm_resid_kernel,
        grid=(n // tn, m // tm),
        in_specs=[
            pl.BlockSpec((tm, kd), lambda j, i: (i, ks)),
            pl.BlockSpec((None, kd, tn), lambda j, i: (layer, ks, j)),
            pl.BlockSpec((tm, tn), lambda j, i: (i, j)),
            pl.BlockSpec((None, 1, tn), lambda j, i: (i // per, 0, j)),
        ],
        out_specs=pl.BlockSpec((tm, tn), lambda j, i: (i, j)),
        out_shape=jax.ShapeDtypeStruct((m, n), F32),
        scratch_shapes=[pltpu.VMEM((kd, tn), BF16)],
        compiler_params=_params(("parallel", "arbitrary")),
        name="matmul_resid",
    )(a, w, x2, gate.reshape(batch, 1, n))


def _matmul_resid_ksplit(a, w, layer, x2, gate, seq, tk=MM_TK):
    kd = a.shape[1]
    for ks in range(kd // tk):
        x2 = _matmul_resid(a, w, layer, x2, gate, seq, tk=tk, ks=ks)
    return x2


def _mm_merge_kernel(a_ref, b_ref, wa_ref, wb_ref, ga_ref, gb_ref, o_ref, wab_ref, wbb_ref):
    @pl.when(pl.program_id(1) == 0)
    def _():
        wab_ref[...] = wa_ref[...].astype(BF16)
        wbb_ref[...] = wb_ref[...].astype(BF16)

    ya = jnp.dot(a_ref[...], wab_ref[...], preferred_element_type=F32)
    yb = jnp.dot(b_ref[...], wbb_ref[...], preferred_element_type=F32)
    o_ref[...] = (jax.nn.sigmoid(ga_ref[...]) * ya + jax.nn.sigmoid(gb_ref[...]) * yb).astype(o_ref.dtype)


def _matmul_merge(a, b, wa, wb, layer, pb, ga_col, gb_col, tn=MM_TN):
    m, kd = a.shape
    n = wa.shape[2]
    tm = min(MM_TM, m)
    return pl.pallas_call(
        _mm_merge_kernel,
        grid=(n // tn, m // tm),
        in_specs=[
            pl.BlockSpec((tm, kd), lambda j, i: (i, 0)),
            pl.BlockSpec((tm, kd), lambda j, i: (i, 0)),
            pl.BlockSpec((None, kd, tn), lambda j, i: (layer, 0, j)),
            pl.BlockSpec((None, kd, tn), lambda j, i: (layer, 0, j)),
            pl.BlockSpec((tm, tn), lambda j, i: (i, ga_col // tn + j)),
            pl.BlockSpec((tm, tn), lambda j, i: (i, gb_col // tn + j)),
        ],
        out_specs=pl.BlockSpec((tm, tn), lambda j, i: (i, j)),
        out_shape=jax.ShapeDtypeStruct((m, n), BF16),
        scratch_shapes=[pltpu.VMEM((kd, tn), BF16), pltpu.VMEM((kd, tn), BF16)],
        compiler_params=_params(("parallel", "arbitrary")),
        name="matmul_merge",
    )(a, b, wa, wb, pb, pb)


def _gelu_tanh(x):
    c = math.sqrt(2.0 / math.pi)
    return 0.5 * x * (1.0 + jnp.tanh(c * (x + 0.044715 * (x * x * x))))


def _compress_one(x_ref, pos_ref, w1_ref, w2_ref, kv, nhalf):
    hp = lax.Precision.HIGHEST
    half = CMP_LEN // 2
    acc_a = jnp.zeros((nhalf, HEAD_DIM), F32)
    acc_b = jnp.zeros((nhalf, HEAD_DIM), F32)
    for l in range(half):
        xl = x_ref[pl.ds(l, nhalf, stride=CMP_STRIDE), :]
        acc_a = acc_a + jnp.dot(xl + pos_ref[kv, l:l + 1, :], w1_ref[kv, l], precision=hp, preferred_element_type=F32)
        acc_b = acc_b + jnp.dot(xl + pos_ref[kv, half + l:half + l + 1, :], w1_ref[kv, half + l], precision=hp,
                                preferred_element_type=F32)
    hid = acc_a + pltpu.roll(acc_b, nhalf - 1, 0)
    hid = _gelu_tanh(hid)
    return jnp.dot(hid, w2_ref[kv], precision=hp, preferred_element_type=F32)


def _compress_kernel(xk_ref, xv_ref, pos_ref, w1_ref, w2_ref, kc_ref, vct_ref, *, nhalf):
    kc_ref[...] = _compress_one(xk_ref, pos_ref, w1_ref, w2_ref, 0, nhalf)
    vct_ref[...] = _compress_one(xv_ref, pos_ref, w1_ref, w2_ref, 1, nhalf).T


def _compress(pa, pos, w1, w2, batch, seq):
    nhalf = seq // CMP_STRIDE
    g = NSA_KV_GROUPS
    col0 = NSA_WIDTH // HEAD_DIM
    full = lambda shape: pl.BlockSpec(shape, lambda b, gi: (0,) * len(shape))
    return pl.pallas_call(
        functools.partial(_compress_kernel, nhalf=nhalf),
        grid=(batch, g),
        in_specs=[
            pl.BlockSpec((seq, HEAD_DIM), lambda b, gi: (b, col0 + gi)),
            pl.BlockSpec((seq, HEAD_DIM), lambda b, gi: (b, col0 + g + gi)),
            full(pos.shape), full(w1.shape), full(w2.shape),
        ],
        out_specs=[
            pl.BlockSpec((None, None, nhalf, HEAD_DIM), lambda b, gi: (b, gi, 0, 0)),
            pl.BlockSpec((None, None, HEAD_DIM, nhalf), lambda b, gi: (b, gi, 0, 0)),
        ],
        out_shape=[
            jax.ShapeDtypeStruct((batch, g, nhalf, HEAD_DIM), F32),
            jax.ShapeDtypeStruct((batch, g, HEAD_DIM, nhalf), F32),
        ],
        compiler_params=_params(("parallel", "parallel")),
        name="nsa_compress",
    )(pa, pa, pos, w1, w2)


def _cmp_slc_overlap_t(seq, ncp):
    nc = (seq - CMP_LEN) // CMP_STRIDE + 1
    nsel = seq // SLC_BLOCK
    cs = np.arange(nc) * CMP_STRIDE
    ce = cs + CMP_LEN - 1
    ss = np.arange(nsel) * SLC_BLOCK
    se = ss + SLC_BLOCK - 1
    ov = np.minimum(ce[:, None], se[None, :]) - np.maximum(cs[:, None], ss[None, :]) + 1
    ov = np.maximum(ov, 0).astype(np.float32)
    out = np.zeros((nsel, ncp), np.float32)
    out[:, :nc] = ov.T
    return out


def _heads_t(x_ref, fn=None):
    parts = []
    for j in range(NSA_HPG):
        x = x_ref[:, j * HEAD_DIM:(j + 1) * HEAD_DIM]
        if fn is not None:
            x = fn(x)
        parts.append(x.T.astype(BF16))
    return jnp.concatenate(parts, axis=1)


def _cmp_attn_kernel(q_ref, kc_ref, vct_ref, ovt_ref, o_ref, sel_ref, *, tq, ncp, nsel):
    i = pl.program_id(2)
    t0 = i * tq
    tpos = t0 + lax.broadcasted_iota(jnp.int32, (ncp, tq), 1)
    nidx = lax.broadcasted_iota(jnp.int32, (ncp, tq), 0)
    vis = (nidx * CMP_STRIDE + (CMP_LEN - 1)) <= tpos
    qt = _heads_t(q_ref)
    st = jnp.dot(kc_ref[...].astype(BF16), qt, preferred_element_type=F32) * ATTN_SCALE
    vct = vct_ref[...].astype(BF16)
    imp = jnp.zeros((ncp, tq), F32)
    for j in range(NSA_HPG):
        s = jnp.where(vis, st[:, j * tq:(j + 1) * tq], NEG_INF)
        e = jnp.exp(s - jnp.max(s, axis=0, keepdims=True))
        p = e / jnp.sum(e, axis=0, keepdims=True)
        p = jnp.where(vis, p, 0.0)
        o_ref[j * HEAD_DIM:(j + 1) * HEAD_DIM, :] = jnp.dot(vct, p.astype(BF16), preferred_element_type=F32)
        imp = imp + p
    pslc = jnp.dot(ovt_ref[...], imp, precision=lax.Precision.HIGHEST, preferred_element_type=F32)
    tl = t0 + lax.broadcasted_iota(jnp.int32, (nsel, tq), 1)
    mi = lax.broadcasted_iota(jnp.int32, (nsel, tq), 0)
    cur = tl // SLC_BLOCK
    valid = mi * SLC_BLOCK <= tl
    forced = (mi == 0) | (mi == cur) | (mi == cur - 1)
    score = jnp.where(valid, pslc + jnp.where(forced, FORCED_BONUS, 0.0), -jnp.inf)
    rank = jnp.zeros((nsel, tq), F32)
    for mp in range(nsel):
        row = score[mp:mp + 1, :]
        before = (row > score) | ((row == score) & (mi > mp))
        rank = rank + jnp.where(before, 1.0, 0.0)
    sel_ref[...] = jnp.where(valid & (rank < float(min(SLC_TOPK, nsel))), 1.0, 0.0)


def _cmp_attn(pa, kc, vct, batch, seq, tq=256):
    ncp = seq // CMP_STRIDE
    nsel = seq // SLC_BLOCK
    g = NSA_KV_GROUPS
    nq = seq // tq
    gw = NSA_HPG * HEAD_DIM
    ovt = jnp.asarray(_cmp_slc_overlap_t(seq, ncp))
    return pl.pallas_call(
        functools.partial(_cmp_attn_kernel, tq=tq, ncp=ncp, nsel=nsel),
        grid=(batch, g, nq),
        in_specs=[
            pl.BlockSpec((tq, gw), lambda b, gi, i: (b * nq + i, gi)),
            pl.BlockSpec((None, None, ncp, HEAD_DIM), lambda b, gi, i: (b, gi, 0, 0)),
            pl.BlockSpec((None, None, HEAD_DIM, ncp), lambda b, gi, i: (b, gi, 0, 0)),
            pl.BlockSpec((nsel, ncp), lambda b, gi, i: (0, 0)),
        ],
        out_specs=[
            pl.BlockSpec((None, gw, tq), lambda b, gi, i: (b, gi, i)),
            pl.BlockSpec((None, None, nsel, tq), lambda b, gi, i: (b, gi, 0, i)),
        ],
        out_shape=[
            jax.ShapeDtypeStruct((batch, NSA_WIDTH, seq), F32),
            jax.ShapeDtypeStruct((batch, g, nsel, seq), F32),
        ],
        compiler_params=_params(("parallel", "parallel", "parallel")),
        name="nsa_cmp_attn",
    )(pa, kc, vct, ovt)


def _rope_tables(positions):
    half = ROT_DIM // 2
    inv = ROPE_THETA ** (-jnp.arange(0, ROT_DIM, 2, dtype=F32) / ROT_DIM)
    ang = positions.astype(F32)[..., None] * inv
    cos, sin = jnp.cos(ang), jnp.sin(ang)
    rest = HEAD_DIM - ROT_DIM
    b, s = positions.shape
    one = jnp.ones((b, s, rest), F32)
    zero = jnp.zeros((b, s, rest), F32)
    zh = jnp.zeros((b, s, half), F32)
    c = jnp.concatenate([cos, cos, one], axis=-1)
    s1 = jnp.concatenate([-sin, zh, zero], axis=-1)
    s2 = jnp.concatenate([zh, sin, zero], axis=-1)
    return [t.reshape(b * s, HEAD_DIM) for t in (c, s1, s2)]


def _rot(x, c, s1, s2):
    half = ROT_DIM // 2
    return x * c + pltpu.roll(x, HEAD_DIM - half, 1) * s1 + pltpu.roll(x, half, 1) * s2


def _kv_prep_kernel(xs_ref, xw_ref, c_ref, s1_ref, s2_ref, k_ref, vt_ref):
    c, s1, s2 = c_ref[...], s1_ref[...], s2_ref[...]
    for br, x_ref in enumerate((xs_ref, xw_ref)):
        for gi in range(NSA_KV_GROUPS):
            src = gi * HEAD_DIM
            dst = br * KV_WIDTH + gi * HEAD_DIM
            k_ref[:, dst:dst + HEAD_DIM] = _rot(x_ref[:, src:src + HEAD_DIM], c, s1, s2).astype(k_ref.dtype)
            vt_ref[dst:dst + HEAD_DIM, :] = x_ref[:, KV_WIDTH + src:KV_WIDTH + src + HEAD_DIM].T.astype(vt_ref.dtype)


def _kv_prep(pa, tables, batch, seq, tm=256):
    nt = seq // tm
    w = 2 * KV_WIDTH
    col = (NSA_WIDTH + 2 * KV_WIDTH) // w
    assert col * w == NSA_WIDTH + 2 * KV_WIDTH
    return pl.pallas_call(
        _kv_prep_kernel,
        grid=(batch, nt),
        in_specs=[pl.BlockSpec((tm, w), lambda b, i: (b * nt + i, col)),
                  pl.BlockSpec((tm, w), lambda b, i: (b * nt + i, col + 1))]
        + [pl.BlockSpec((tm, HEAD_DIM), lambda b, i: (b * nt + i, 0))] * 3,
        out_specs=[
            pl.BlockSpec((tm, 2 * KV_WIDTH), lambda b, i: (b * nt + i, 0)),
            pl.BlockSpec((None, 2 * KV_WIDTH, tm), lambda b, i: (b, 0, i)),
        ],
        out_shape=[
            jax.ShapeDtypeStruct((batch * seq, 2 * KV_WIDTH), BF16),
            jax.ShapeDtypeStruct((batch, 2 * KV_WIDTH, seq), BF16),
        ],
        compiler_params=_params(("parallel", "parallel")),
        name="nsa_kv_prep",
    )(pa, pa, *tables)


def _flash_tiles_t(qt, k_ref, vt_ref, lo, hi, mask_fn, m_ref, l_ref, acc_ref, *, tk, peel_last):
    m_ref[...] = jnp.full(m_ref.shape, NEG_INF, F32)
    l_ref[...] = jnp.zeros(l_ref.shape, F32)
    acc_ref[...] = jnp.zeros(acc_ref.shape, F32)

    def step(kt, diag):
        k0 = pl.multiple_of(kt * tk, tk)
        k = k_ref[pl.ds(k0, tk), :]
        vt = vt_ref[:, pl.ds(k0, tk)]
        s = jnp.dot(k, qt, preferred_element_type=F32)
        mask = mask_fn(kt, diag)
        mask = jnp.concatenate([mask] * NSA_HPG, axis=1)
        s = jnp.where(mask, s, NEG_INF)
        m_old = m_ref[...]
        m_new = jnp.maximum(m_old, jnp.max(s, axis=0, keepdims=True))
        m_use = jnp.where(m_new > 0.5 * NEG_INF, m_new, 0.0)
        p = jnp.exp2(s - m_use)
        alpha = jnp.exp2(m_old - m_new)
        l_ref[...] = alpha * l_ref[...] + jnp.sum(p, axis=0, keepdims=True)
        acc_ref[...] = alpha * acc_ref[...] + jnp.dot(vt, p.astype(BF16), preferred_element_type=F32)
        m_ref[...] = m_new

    def body(diag):
        def run(kt, carry):
            step(kt, diag)
            return carry
        return run

    if peel_last:
        lax.fori_loop(lo, hi - 1, body(False), 0)
        step(hi - 1, True)
    else:
        lax.fori_loop(lo, hi, body(True), 0)
    return acc_ref[...] / l_ref[...]


def _sel_attn_kernel(q_ref, c_ref, s1_ref, s2_ref, ks_ref, kw_ref, vst_ref, vwt_ref, sel_ref, oct_ref, ng_ref, o_ref,
                     m_ref, l_ref, acc_ref, gt_ref, *, tq, tk):
    gi = pl.program_id(1)
    i = pl.program_id(2)
    t0 = i * tq
    nj = NSA_HPG
    c, s1, s2 = c_ref[...], s1_ref[...], s2_ref[...]
    scale2 = ATTN_SCALE * math.log2(math.e)
    qt = _heads_t(q_ref, lambda x: _rot(x, c, s1, s2) * scale2)
    kpos = lax.broadcasted_iota(jnp.int32, (tk, tq), 0)
    tpos = t0 + lax.broadcasted_iota(jnp.int32, (tk, tq), 1)
    nb = tk // SLC_BLOCK

    def sel_mask(kt, diag):
        rows = sel_ref[pl.ds(pl.multiple_of(kt * nb, nb), nb), :]
        chosen = jnp.broadcast_to(rows[:, None, :], (nb, SLC_BLOCK, tq)).reshape(tk, tq) > 0.5
        return chosen & ((kt * tk + kpos) <= tpos) if diag else chosen

    def win_mask(kt, diag):
        kp = kt * tk + kpos
        return (kp <= tpos) & (kp > tpos - WINDOW)

    last = (t0 + tq - 1) // tk + 1
    o_slc = _flash_tiles_t(qt, ks_ref, vst_ref, 0, last, sel_mask, m_ref, l_ref, acc_ref, tk=tk, peel_last=True)
    first = jnp.maximum(t0 - (WINDOW - 1), 0) // tk
    o_win = _flash_tiles_t(qt, kw_ref, vwt_ref, first, last, win_mask, m_ref, l_ref, acc_ref, tk=tk, peel_last=False)
    gt_ref[...] = jax.nn.sigmoid(ng_ref[...].T)
    for j in range(nj):
        gate = [gt_ref[pl.ds(br * NSA_HEADS + gi * nj + j, 1), :] for br in range(3)]
        ot = (gate[0] * oct_ref[j * HEAD_DIM:(j + 1) * HEAD_DIM, :] + gate[1] * o_slc[:, j * tq:(j + 1) * tq]
              + gate[2] * o_win[:, j * tq:(j + 1) * tq])
        o_ref[:, j * HEAD_DIM:(j + 1) * HEAD_DIM] = ot.T.astype(o_ref.dtype)


def _sel_attn(pa, tables, kn, vt, sel, o_cmp_t, png, batch, seq, tq=256, tk=512):
    g = NSA_KV_GROUPS
    nq = seq // tq
    gw = NSA_HPG * HEAD_DIM
    tk = min(tk, seq)
    assert tk % tq == 0 and seq % tk == 0
    nsel = seq // SLC_BLOCK
    n = NSA_HPG * tq
    tab = pl.BlockSpec((tq, HEAD_DIM), lambda b, gi, i: (b * nq + i, 0))
    return pl.pallas_call(
        functools.partial(_sel_attn_kernel, tq=tq, tk=tk),
        grid=(batch, g, nq),
        in_specs=[
            pl.BlockSpec((tq, gw), lambda b, gi, i: (b * nq + i, gi)),
            tab, tab, tab,
            pl.BlockSpec((seq, HEAD_DIM), lambda b, gi, i: (b, gi)),
            pl.BlockSpec((seq, HEAD_DIM), lambda b, gi, i: (b, g + gi)),
            pl.BlockSpec((None, HEAD_DIM, seq), lambda b, gi, i: (b, gi, 0)),
            pl.BlockSpec((None, HEAD_DIM, seq), lambda b, gi, i: (b, g + gi, 0)),
            pl.BlockSpec((None, None, nsel, tq), lambda b, gi, i: (b, gi, 0, i)),
            pl.BlockSpec((None, gw, tq), lambda b, gi, i: (b, gi, i)),
            pl.BlockSpec((tq, LANES), lambda b, gi, i: (b * nq + i, 0)),
        ],
        out_specs=pl.BlockSpec((tq, gw), lambda b, gi, i: (b * nq + i, gi)),
        out_shape=jax.ShapeDtypeStruct((batch * seq, NSA_WIDTH), BF16),
        scratch_shapes=[
            pltpu.VMEM((1, n), F32),
            pltpu.VMEM((1, n), F32),
            pltpu.VMEM((HEAD_DIM, n), F32),
            pltpu.VMEM((LANES, tq), F32),
        ],
        compiler_params=_params(("parallel", "parallel", "parallel")),
        name="nsa_sel_win_attn",
    )(pa, *tables, kn, kn, vt, vt, sel, o_cmp_t, png)


HG_TILE = 128
HG_GROUP = 8
HG_LEVELS = (64, 32, 16, 8)
HG_HEADS_PER_STEP = 4


def _split3(x):
    hi = x.astype(BF16)
    r = x - hi.astype(F32)
    mid = r.astype(BF16)
    lo = (r - mid.astype(F32)).astype(BF16)
    return hi, mid, lo


def _group_row(x, size, row):
    t, d = x.shape
    x3 = x.reshape(t // size, size, d)
    return jnp.broadcast_to(x3[:, row:row + 1, :], x3.shape).reshape(t, d)


def _hgrn_head(q, z, v, g, lb, nw, low, st):
    t = HG_TILE
    log_sig = jnp.minimum(z, 0.0) - jnp.log1p(jnp.exp(-jnp.abs(z)))
    a = jnp.log(lb)
    c = jnp.log1p(-lb) + log_sig
    log_f = jnp.maximum(a, c) + jnp.log1p(jnp.exp(-jnp.abs(a - c)))
    kk = (1.0 - lb) * jax.nn.sigmoid(-z)
    hi, mid, lo = _split3(log_f)
    b = (jnp.dot(low, hi, preferred_element_type=F32) + jnp.dot(low, mid, preferred_element_type=F32)
         + jnp.dot(low, lo, preferred_element_type=F32))
    ti = lax.broadcasted_iota(jnp.int32, (t, t), 0)
    si = lax.broadcasted_iota(jnp.int32, (t, t), 1)
    trow = lax.broadcasted_iota(jnp.int32, (t, HGRN_DK), 0)
    att = jnp.zeros((t, t), F32)
    for h in HG_LEVELS:
        bref = _group_row(b, 2 * h, h - 1)
        is_q = (trow % (2 * h)) >= h
        e = jnp.exp(jnp.where(is_q, b - bref, bref - b))
        qt = jnp.where(is_q, q * e, 0.0).astype(BF16)
        kt = jnp.where(is_q, 0.0, kk * e).astype(BF16)
        att = att + jnp.where((ti // (2 * h)) == (si // (2 * h)), _nt_dot(qt, kt), 0.0)
    lane_d = lax.broadcasted_iota(jnp.int32, (HGRN_DK, t), 1)
    diag = jnp.zeros((t, t), F32)
    for sg in range(HG_GROUP):
        bs = _group_row(b, HG_GROUP, sg)
        ks = _group_row(kk, HG_GROUP, sg)
        p = (q * jnp.exp(jnp.minimum(b - bs, 0.0))) * ks
        spread = jnp.where((lane_d % HG_GROUP) == sg, 1.0, 0.0).astype(BF16)
        diag = diag + jnp.dot(p.astype(BF16), spread, preferred_element_type=F32)
    att = att + jnp.where(((ti // HG_GROUP) == (si // HG_GROUP)) & (si <= ti), diag, 0.0)
    vb = v.astype(BF16)
    o = jnp.dot(att.astype(BF16), vb, preferred_element_type=F32)
    o = o + _nt_dot((q * jnp.exp(b)).astype(BF16), st.astype(BF16))
    b_last = b[t - 1:t, :]
    kdec = (kk * jnp.exp(b_last - b)).astype(BF16)
    st_new = st * jnp.exp(b_last) + _tn_dot(vb, kdec)
    y = o * lax.rsqrt(jnp.mean(o * o, axis=-1, keepdims=True) + NORM_EPS) * nw
    return y * (g * jax.nn.sigmoid(g)), st_new


def _hgrn_kernel(q_ref, z_ref, v_ref, g_ref, lb_ref, nw_ref, low_ref, o_ref, st_ref, *, nh):
    @pl.when(pl.program_id(2) == 0)
    def _():
        st_ref[...] = jnp.zeros_like(st_ref)

    low = low_ref[...]
    nw = nw_ref[...]
    for hd in range(nh):
        cs = slice(hd * HGRN_DK, (hd + 1) * HGRN_DK)
        y, st_new = _hgrn_head(q_ref[:, cs], z_ref[:, cs], v_ref[:, cs], g_ref[:, cs], lb_ref[hd:hd + 1, :], nw, low,
                               st_ref[hd])
        st_ref[hd] = st_new
        o_ref[:, cs] = y.astype(o_ref.dtype)


def _hgrn(pb, lb, norm_w, batch, seq):
    hh = HGRN_HEADS
    nh = HG_HEADS_PER_STEP
    nt = seq // HG_TILE
    low = jnp.asarray(np.tril(np.ones((HG_TILE, HG_TILE), np.float32)), dtype=BF16)
    spec = lambda part: pl.BlockSpec((HG_TILE, nh * HGRN_DK), lambda b, h, i: (b * nt + i, part * (hh // nh) + h))
    return pl.pallas_call(
        functools.partial(_hgrn_kernel, nh=nh),
        grid=(batch, hh // nh, nt),
        in_specs=[
            spec(0), spec(1), spec(2), spec(3),
            pl.BlockSpec((None, nh, HGRN_DK), lambda b, h, i: (h, 0, 0)),
            pl.BlockSpec((1, HGRN_DV), lambda b, h, i: (0, 0)),
            pl.BlockSpec(low.shape, lambda b, h, i: (0, 0)),
        ],
        out_specs=pl.BlockSpec((HG_TILE, nh * HGRN_DV), lambda b, h, i: (b * nt + i, h)),
        out_shape=jax.ShapeDtypeStruct((batch * seq, HGRN_VW), BF16),
        scratch_shapes=[pltpu.VMEM((nh, HGRN_DV, HGRN_DK), F32)],
        compiler_params=_params(("parallel", "parallel", "arbitrary")),
        name="hgrn2",
    )(pb, pb, pb, pb, lb.reshape(hh // nh, nh, HGRN_DK), norm_w.reshape(1, HGRN_DV), low)


def _hybrid_mixer(h, tables, layer, w_in_t, cmp_pos, cmp_w1, cmp_w2, lb, g_norm_w, w_up_a, w_up_b, batch, seq):
    pa = _matmul_t(h, w_in_t, layer, 0, SEG_A, F32)
    png = _matmul_t(h, w_in_t, layer, SEG_A, LANES, F32, tn=LANES)
    pb = _matmul_t(h, w_in_t, layer, SEG_A + SEG_G, SEG_B, F32)
    kc, vct = _compress(pa, cmp_pos, cmp_w1, cmp_w2, batch, seq)
    o_cmp_t, sel = _cmp_attn(pa, kc, vct, batch, seq)
    kn, vt = _kv_prep(pa, tables, batch, seq)
    o_nsa = _sel_attn(pa, tables, kn, vt, sel, o_cmp_t, png, batch, seq)
    o_hgrn = _hgrn(pb, lb, g_norm_w, batch, seq)
    return _matmul_merge(o_nsa, o_hgrn, w_up_a, w_up_b, layer, pb, 2 * HGRN_KW + 2 * HGRN_VW,
                         2 * HGRN_KW + 2 * HGRN_VW + D_MODEL)


def kernel(x, c, positions, ada_w, ada_b, norm_mix_w, w_in, nsa_cmp_pos, nsa_cmp_w1, nsa_cmp_w2, hgrn_lb_logits,
           hgrn_norm_w, w_up_a, w_up_b, w_out, norm_mlp_w, w_mlp1, w_mlp2, final_norm_w):
    batch, seq, d = x.shape
    depth = ada_w.shape[0]
    lb_all = jnp.cumsum(jax.nn.softmax(hgrn_lb_logits.astype(F32), axis=0), axis=0)
    lb_all = lb_all - lb_all[0:1]
    mod = _ada_mod(c, ada_w, ada_b)
    tables = _rope_tables(positions)
    x2 = x.reshape(batch * seq, d)
    zeros = jnp.zeros((batch, d), F32)
    w_in_t = jnp.transpose(w_in, (0, 2, 1))
    for l in range(depth):
        sh1, sc1, g1, sh2, sc2, g2 = [mod[l][:, k * d:(k + 1) * d] for k in range(6)]
        h = _norm_mod(x2, norm_mix_w[l], sc1, sh1, seq, BF16)
        y = _hybrid_mixer(h, tables, l, w_in_t, nsa_cmp_pos[l], nsa_cmp_w1[l], nsa_cmp_w2[l], lb_all[l],
                          hgrn_norm_w[l], w_up_a, w_up_b, batch, seq)
        x2 = _matmul_resid(y, w_out, l, x2, g1, seq)
        h = _norm_mod(x2, norm_mlp_w[l], sc2, sh2, seq, BF16)
        u = _matmul(h, w_mlp1, l, BF16, relu2=True)
        x2 = _matmul_resid_ksplit(u, w_mlp2, l, x2, g2, seq)
    out = _norm_mod(x2, final_norm_w, zeros, zeros, seq, F32)
    return out.reshape(batch, seq, d)
```

```python
import functools
import math

import numpy as np
import jax
import jax.numpy as jnp
from jax import lax
from jax.experimental import pallas as pl
from jax.experimental.pallas import tpu as pltpu

D_MODEL = 4096
DEPTH = 2
NSA_HEADS = 16
NSA_KV_GROUPS = 4
NSA_HPG = NSA_HEADS // NSA_KV_GROUPS
HEAD_DIM = 128
CMP_LEN = 32
CMP_STRIDE = 16
SLC_BLOCK = 64
SLC_TOPK = 16
WINDOW = 512
ATTN_SCALE = HEAD_DIM ** -0.5
FORCED_BONUS = 1e6
NEG_INF = -1e30
HGRN_HEADS = 16
HGRN_DK = 128
HGRN_DV = 128
ROPE_THETA = 500000.0
ROT_DIM = HEAD_DIM // 4
D_FF = 4 * D_MODEL
NORM_EPS = 1e-6

NSA_WIDTH = NSA_HEADS * HEAD_DIM
KV_WIDTH = NSA_KV_GROUPS * HEAD_DIM
HGRN_KW = HGRN_HEADS * HGRN_DK
HGRN_VW = HGRN_HEADS * HGRN_DV
SEG_A = NSA_WIDTH + 6 * KV_WIDTH
SEG_G = 3 * NSA_HEADS
SEG_B = 2 * HGRN_KW + 2 * HGRN_VW + 2 * D_MODEL

LANES = 128
VMEM_LIMIT = 56 * 1024 * 1024

BF16 = jnp.bfloat16
F32 = jnp.float32


def _params(sem):
    return pltpu.CompilerParams(dimension_semantics=sem, vmem_limit_bytes=VMEM_LIMIT)


def _nt_dot(a, b, precision=None):
    return lax.dot_general(a, b, (((1,), (1,)), ((), ())), preferred_element_type=F32, precision=precision)


def _tn_dot(a, b, precision=None):
    return lax.dot_general(a, b, (((0,), (0,)), ((), ())), preferred_element_type=F32, precision=precision)


def _ada_kernel(cb_ref, w_ref, b_ref, o_ref, cact_ref, *, batch, tn):
    @pl.when((pl.program_id(0) == 0) & (pl.program_id(1) == 0))
    def _():
        cb = cb_ref[...]
        cact_ref[...] = cb * jax.nn.sigmoid(cb)

    for b in range(batch):
        cact = cact_ref[b]
        rows = []
        for s in range(tn // LANES):
            w = w_ref[:, s * LANES:(s + 1) * LANES]
            rows.append(jnp.sum(w * cact, axis=0, keepdims=True))
        o_ref[b:b + 1, :] = jnp.concatenate(rows, axis=1) + b_ref[...]


def _ada_mod(c, ada_w, ada_b):
    depth, k, n = ada_w.shape
    batch = c.shape[0]
    tn = 512
    cb = jnp.broadcast_to(c[:, :, None], (batch, k, LANES))
    out = pl.pallas_call(
        functools.partial(_ada_kernel, batch=batch, tn=tn),
        grid=(depth, n // tn),
        in_specs=[
            pl.BlockSpec((batch, k, LANES), lambda l, j: (0, 0, 0)),
            pl.BlockSpec((None, k, tn), lambda l, j: (l, 0, j)),
            pl.BlockSpec((None, 1, tn), lambda l, j: (l, 0, j)),
        ],
        out_specs=pl.BlockSpec((None, batch, tn), lambda l, j: (l, 0, j)),
        out_shape=jax.ShapeDtypeStruct((depth, batch, n), F32),
        scratch_shapes=[pltpu.VMEM((batch, k, LANES), F32)],
        compiler_params=_params(("arbitrary", "arbitrary")),
        name="ada_mod",
    )(cb, ada_w, ada_b.reshape(depth, 1, n))
    return out


def _norm_kernel(x_ref, w_ref, sc_ref, sh_ref, o_ref):
    x = x_ref[...]
    y = x * lax.rsqrt(jnp.mean(x * x, axis=-1, keepdims=True) + NORM_EPS)
    y = y * w_ref[...]
    y = y * (1.0 + sc_ref[...]) + sh_ref[...]
    o_ref[...] = y.astype(o_ref.dtype)


def _norm_mod(x2, w, sc, sh, seq, out_dtype):
    m, d = x2.shape
    batch = m // seq
    tm = 256
    per = seq // tm
    return pl.pallas_call(
        _norm_kernel,
        grid=(m // tm,),
        in_specs=[
            pl.BlockSpec((tm, d), lambda i: (i, 0)),
            pl.BlockSpec((1, d), lambda i: (0, 0)),
            pl.BlockSpec((None, 1, d), lambda i: (i // per, 0, 0)),
            pl.BlockSpec((None, 1, d), lambda i: (i // per, 0, 0)),
        ],
        out_specs=pl.BlockSpec((tm, d), lambda i: (i, 0)),
        out_shape=jax.ShapeDtypeStruct((m, d), out_dtype),
        compiler_params=_params(("parallel",)),
        name="norm_mod",
    )(x2, w.reshape(1, d), sc.reshape(batch, 1, d), sh.reshape(batch, 1, d))


MM_TM = 1024
MM_TN = 512
MM_TK = 4096


def _mm_kernel(a_ref, w_ref, o_ref, wb_ref, *, relu2):
    @pl.when(pl.program_id(1) == 0)
    def _():
        wb_ref[...] = w_ref[...].astype(BF16)

    r = jnp.dot(a_ref[...], wb_ref[...], preferred_element_type=F32)
    if relu2:
        r = jnp.square(jnp.maximum(r, 0.0))
    o_ref[...] = r.astype(o_ref.dtype)


def _matmul(a, w, layer, out_dtype, relu2=False, tn=MM_TN):
    m, kd = a.shape
    n = w.shape[2]
    tm = min(MM_TM, m)
    assert n % tn == 0 and m % tm == 0
    return pl.pallas_call(
        functools.partial(_mm_kernel, relu2=relu2),
        grid=(n // tn, m // tm),
        in_specs=[
            pl.BlockSpec((tm, kd), lambda j, i: (i, 0)),
            pl.BlockSpec((None, kd, tn), lambda j, i: (layer, 0, j)),
        ],
        out_specs=pl.BlockSpec((tm, tn), lambda j, i: (i, j)),
        out_shape=jax.ShapeDtypeStruct((m, n), out_dtype),
        scratch_shapes=[pltpu.VMEM((kd, tn), BF16)],
        compiler_params=_params(("parallel", "arbitrary")),
        name="matmul",
    )(a, w)


def _mm_t_kernel(*refs, shift):
    if shift:
        a_ref, w_ref, wn_ref, o_ref, wb_ref = refs
    else:
        a_ref, w_ref, o_ref, wb_ref = refs

    @pl.when(pl.program_id(1) == 0)
    def _():
        tn = w_ref.shape[0]
        wb_ref[0:tn - shift, :] = w_ref[shift:tn, :].astype(BF16)
        if shift:
            wb_ref[tn - shift:tn, :] = wn_ref[0:shift, :].astype(BF16)

    o_ref[...] = _nt_dot(a_ref[...], wb_ref[...]).astype(o_ref.dtype)


BF16_SUBLANES = 16


def _matmul_t(a, wt, layer, row0, nrows, out_dtype, tn=MM_TN):
    m, kd = a.shape
    tm = min(MM_TM, m)
    shift = row0 % tn
    base = row0 - shift
    nxt = 64
    assert nrows % tn == 0 and m % tm == 0 and shift % BF16_SUBLANES == 0 and shift <= nxt and tn % nxt == 0
    in_specs = [
        pl.BlockSpec((tm, kd), lambda j, i: (i, 0)),
        pl.BlockSpec((None, tn, kd), lambda j, i: (layer, base // tn + j, 0)),
    ]
    args = [a, wt]
    if shift:
        in_specs.append(pl.BlockSpec((None, nxt, kd), lambda j, i: (layer, (base + tn * (j + 1)) // nxt, 0)))
        args.append(wt)
    return pl.pallas_call(
        functools.partial(_mm_t_kernel, shift=shift),
        grid=(nrows // tn, m // tm),
        in_specs=in_specs,
        out_specs=pl.BlockSpec((tm, tn), lambda j, i: (i, j)),
        out_shape=jax.ShapeDtypeStruct((m, nrows), out_dtype),
        scratch_shapes=[pltpu.VMEM((tn, kd), BF16)],
        compiler_params=_params(("parallel", "arbitrary")),
        name="matmul_t",
    )(*args)


def _mm_resid_kernel(a_ref, w_ref, x_ref, g_ref, o_ref, wb_ref):
    @pl.when(pl.program_id(1) == 0)
    def _():
        wb_ref[...] = w_ref[...].astype(BF16)

    o_ref[...] = x_ref[...] + g_ref[...] * jnp.dot(a_ref[...], wb_ref[...], preferred_element_type=F32)


def _matmul_resid(a, w, layer, x2, gate, seq, tn=MM_TN, tk=None, ks=0):
    m = a.shape[0]
    kd = tk if tk is not None else a.shape[1]
    n = w.shape[2]
    tm = min(MM_TM, seq)
    per = seq // tm
    batch = m // seq
    return pl.pallas_call(
        _mm_resid_kernel,
        grid=(n // tn, m // tm),
        in_specs=[
            pl.BlockSpec((tm, kd), lambda j, i: (i, ks)),
            pl.BlockSpec((None, kd, tn), lambda j, i: (layer, ks, j)),
            pl.BlockSpec((tm, tn), lambda j, i: (i, j)),
            pl.BlockSpec((None, 1, tn), lambda j, i: (i // per, 0, j)),
        ],
        out_specs=pl.BlockSpec((tm, tn), lambda j, i: (i, j)),
        out_shape=jax.ShapeDtypeStruct((m, n), F32),
        scratch_shapes=[pltpu.VMEM((kd, tn), BF16)],
        compiler_params=_params(("parallel", "arbitrary")),
        name="matmul_resid",
    )(a, w, x2, gate.reshape(batch, 1, n))


def _matmul_resid_ksplit(a, w, layer, x2, gate, seq, tk=MM_TK):
    kd = a.shape[1]
    for ks in range(kd // tk):
        x2 = _matmul_resid(a, w, layer, x2, gate, seq, tk=tk, ks=ks)
    return x2


def _mm_merge_kernel(a_ref, b_ref, wa_ref, wb_ref, ga_ref, gb_ref, o_ref, wab_ref, wbb_ref):
    @pl.when(pl.program_id(1) == 0)
    def _():
        wab_ref[...] = wa_ref[...].astype(BF16)
        wbb_ref[...] = wb_ref[...].astype(BF16)

    ya = jnp.dot(a_ref[...], wab_ref[...], preferred_element_type=F32)
    yb = jnp.dot(b_ref[...], wbb_ref[...], preferred_element_type=F32)
    o_ref[...] = (jax.nn.sigmoid(ga_ref[...]) * ya + jax.nn.sigmoid(gb_ref[...]) * yb).astype(o_ref.dtype)


def _matmul_merge(a, b, wa, wb, layer, pb, ga_col, gb_col, tn=MM_TN):
    m, kd = a.shape
    n = wa.shape[2]
    tm = min(MM_TM, m)
    return pl.pallas_call(
        _mm_merge_kernel,
        grid=(n // tn, m // tm),
        in_specs=[
            pl.BlockSpec((tm, kd), lambda j, i: (i, 0)),
            pl.BlockSpec((tm, kd), lambda j, i: (i, 0)),
            pl.BlockSpec((None, kd, tn), lambda j, i: (layer, 0, j)),
            pl.BlockSpec((None, kd, tn), lambda j, i: (layer, 0, j)),
            pl.BlockSpec((tm, tn), lambda j, i: (i, ga_col // tn + j)),
            pl.BlockSpec((tm, tn), lambda j, i: (i, gb_col // tn + j)),
        ],
        out_specs=pl.BlockSpec((tm, tn), lambda j, i: (i, j)),
        out_shape=jax.ShapeDtypeStruct((m, n), BF16),
        scratch_shapes=[pltpu.VMEM((kd, tn), BF16), pltpu.VMEM((kd, tn), BF16)],
        compiler_params=_params(("parallel", "arbitrary")),
        name="matmul_merge",
    )(a, b, wa, wb, pb, pb)


def _gelu_tanh(x):
    c = math.sqrt(2.0 / math.pi)
    return 0.5 * x * (1.0 + jnp.tanh(c * (x + 0.044715 * (x * x * x))))


def _compress_one(x_ref, pos_ref, w1_ref, w2_ref, kv, nhalf):
    hp = lax.Precision.HIGHEST
    half = CMP_LEN // 2
    acc_a = jnp.zeros((nhalf, HEAD_DIM), F32)
    acc_b = jnp.zeros((nhalf, HEAD_DIM), F32)
    for l in range(half):
        xl = x_ref[pl.ds(l, nhalf, stride=CMP_STRIDE), :]
        acc_a = acc_a + jnp.dot(xl + pos_ref[kv, l:l + 1, :], w1_ref[kv, l], precision=hp, preferred_element_type=F32)
        acc_b = acc_b + jnp.dot(xl + pos_ref[kv, half + l:half + l + 1, :], w1_ref[kv, half + l], precision=hp,
                                preferred_element_type=F32)
    hid = acc_a + pltpu.roll(acc_b, nhalf - 1, 0)
    hid = _gelu_tanh(hid)
    return jnp.dot(hid, w2_ref[kv], precision=hp, preferred_element_type=F32)


def _compress_kernel(xk_ref, xv_ref, pos_ref, w1_ref, w2_ref, kc_ref, vct_ref, *, nhalf):
    kc_ref[...] = _compress_one(xk_ref, pos_ref, w1_ref, w2_ref, 0, nhalf)
    vct_ref[...] = _compress_one(xv_ref, pos_ref, w1_ref, w2_ref, 1, nhalf).T


def _compress(pa, pos, w1, w2, batch, seq):
    nhalf = seq // CMP_STRIDE
    g = NSA_KV_GROUPS
    col0 = NSA_WIDTH // HEAD_DIM
    full = lambda shape: pl.BlockSpec(shape, lambda b, gi: (0,) * len(shape))
    return pl.pallas_call(
        functools.partial(_compress_kernel, nhalf=nhalf),
        grid=(batch, g),
        in_specs=[
            pl.BlockSpec((seq, HEAD_DIM), lambda b, gi: (b, col0 + gi)),
            pl.BlockSpec((seq, HEAD_DIM), lambda b, gi: (b, col0 + g + gi)),
            full(pos.shape), full(w1.shape), full(w2.shape),
        ],
        out_specs=[
            pl.BlockSpec((None, None, nhalf, HEAD_DIM), lambda b, gi: (b, gi, 0, 0)),
            pl.BlockSpec((None, None, HEAD_DIM, nhalf), lambda b, gi: (b, gi, 0, 0)),
        ],
        out_shape=[
            jax.ShapeDtypeStruct((batch, g, nhalf, HEAD_DIM), F32),
            jax.ShapeDtypeStruct((batch, g, HEAD_DIM, nhalf), F32),
        ],
        compiler_params=_params(("parallel", "parallel")),
        name="nsa_compress",
    )(pa, pa, pos, w1, w2)


def _cmp_slc_overlap_t(seq, ncp):
    nc = (seq - CMP_LEN) // CMP_STRIDE + 1
    nsel = seq // SLC_BLOCK
    cs = np.arange(nc) * CMP_STRIDE
    ce = cs + CMP_LEN - 1
    ss = np.arange(nsel) * SLC_BLOCK
    se = ss + SLC_BLOCK - 1
    ov = np.minimum(ce[:, None], se[None, :]) - np.maximum(cs[:, None], ss[None, :]) + 1
    ov = np.maximum(ov, 0).astype(np.float32)
    out = np.zeros((nsel, ncp), np.float32)
    out[:, :nc] = ov.T
    return out


def _heads_t(x_ref, fn=None):
    parts = []
    for j in range(NSA_HPG):
        x = x_ref[:, j * HEAD_DIM:(j + 1) * HEAD_DIM]
        if fn is not None:
            x = fn(x)
        parts.append(x.T.astype(BF16))
    return jnp.concatenate(parts, axis=1)


def _cmp_attn_kernel(q_ref, kc_ref, vct_ref, ovt_ref, o_ref, sel_ref, *, tq, ncp, nsel):
    i = pl.program_id(2)
    t0 = i * tq
    tpos = t0 + lax.broadcasted_iota(jnp.int32, (ncp, tq), 1)
    nidx = lax.broadcasted_iota(jnp.int32, (ncp, tq), 0)
    vis = (nidx * CMP_STRIDE + (CMP_LEN - 1)) <= tpos
    qt = _heads_t(q_ref)
    st = jnp.dot(kc_ref[...].astype(BF16), qt, preferred_element_type=F32) * ATTN_SCALE
    vct = vct_ref[...].astype(BF16)
    imp = jnp.zeros((ncp, tq), F32)
    for j in range(NSA_HPG):
        s = jnp.where(vis, st[:, j * tq:(j + 1) * tq], NEG_INF)
        e = jnp.exp(s - jnp.max(s, axis=0, keepdims=True))
        p = e / jnp.sum(e, axis=0, keepdims=True)
        p = jnp.where(vis, p, 0.0)
        o_ref[j * HEAD_DIM:(j + 1) * HEAD_DIM, :] = jnp.dot(vct, p.astype(BF16), preferred_element_type=F32)
        imp = imp + p
    pslc = jnp.dot(ovt_ref[...], imp, precision=lax.Precision.HIGHEST, preferred_element_type=F32)
    tl = t0 + lax.broadcasted_iota(jnp.int32, (nsel, tq), 1)
    mi = lax.broadcasted_iota(jnp.int32, (nsel, tq), 0)
    cur = tl // SLC_BLOCK
    valid = mi * SLC_BLOCK <= tl
    forced = (mi == 0) | (mi == cur) | (mi == cur - 1)
    score = jnp.where(valid, pslc + jnp.where(forced, FORCED_BONUS, 0.0), -jnp.inf)
    rank = jnp.zeros((nsel, tq), F32)
    for mp in range(nsel):
        row = score[mp:mp + 1, :]
        before = (row > score) | ((row == score) & (mi > mp))
        rank = rank + jnp.where(before, 1.0, 0.0)
    sel_ref[...] = jnp.where(valid & (rank < float(min(SLC_TOPK, nsel))), 1.0, 0.0)


def _cmp_attn(pa, kc, vct, batch, seq, tq=256):
    ncp = seq // CMP_STRIDE
    nsel = seq // SLC_BLOCK
    g = NSA_KV_GROUPS
    nq = seq // tq
    gw = NSA_HPG * HEAD_DIM
    ovt = jnp.asarray(_cmp_slc_overlap_t(seq, ncp))
    return pl.pallas_call(
        functools.partial(_cmp_attn_kernel, tq=tq, ncp=ncp, nsel=nsel),
        grid=(batch, g, nq),
        in_specs=[
            pl.BlockSpec((tq, gw), lambda b, gi, i: (b * nq + i, gi)),
            pl.BlockSpec((None, None, ncp, HEAD_DIM), lambda b, gi, i: (b, gi, 0, 0)),
            pl.BlockSpec((None, None, HEAD_DIM, ncp), lambda b, gi, i: (b, gi, 0, 0)),
            pl.BlockSpec((nsel, ncp), lambda b, gi, i: (0, 0)),
        ],
        out_specs=[
            pl.BlockSpec((None, gw, tq), lambda b, gi, i: (b, gi, i)),
            pl.BlockSpec((None, None, nsel, tq), lambda b, gi, i: (b, gi, 0, i)),
        ],
        out_shape=[
            jax.ShapeDtypeStruct((batch, NSA_WIDTH, seq), F32),
            jax.ShapeDtypeStruct((batch, g, nsel, seq), F32),
        ],
        compiler_params=_params(("parallel", "parallel", "parallel")),
        name="nsa_cmp_attn",
    )(pa, kc, vct, ovt)


def _rope_tables(positions):
    half = ROT_DIM // 2
    inv = ROPE_THETA ** (-jnp.arange(0, ROT_DIM, 2, dtype=F32) / ROT_DIM)
    ang = positions.astype(F32)[..., None] * inv
    cos, sin = jnp.cos(ang), jnp.sin(ang)
    rest = HEAD_DIM - ROT_DIM
    b, s = positions.shape
    one = jnp.ones((b, s, rest), F32)
    zero = jnp.zeros((b, s, rest), F32)
    zh = jnp.zeros((b, s, half), F32)
    c = jnp.concatenate([cos, cos, one], axis=-1)
    s1 = jnp.concatenate([-sin, zh, zero], axis=-1)
    s2 = jnp.concatenate([zh, sin, zero], axis=-1)
    return [t.reshape(b * s, HEAD_DIM) for t in (c, s1, s2)]


def _rot(x, c, s1, s2):
    half = ROT_DIM // 2
    return x * c + pltpu.roll(x, HEAD_DIM - half, 1) * s1 + pltpu.roll(x, half, 1) * s2


def _kv_prep_kernel(xs_ref, xw_ref, c_ref, s1_ref, s2_ref, k_ref, vt_ref):
    c, s1, s2 = c_ref[...], s1_ref[...], s2_ref[...]
    for br, x_ref in enumerate((xs_ref, xw_ref)):
        for gi in range(NSA_KV_GROUPS):
            src = gi * HEAD_DIM
            dst = br * KV_WIDTH + gi * HEAD_DIM
            k_ref[:, dst:dst + HEAD_DIM] = _rot(x_ref[:, src:src + HEAD_DIM], c, s1, s2).astype(k_ref.dtype)
            vt_ref[dst:dst + HEAD_DIM, :] = x_ref[:, KV_WIDTH + src:KV_WIDTH + src + HEAD_DIM].T.astype(vt_ref.dtype)


def _kv_prep(pa, tables, batch, seq, tm=256):
    nt = seq // tm
    w = 2 * KV_WIDTH
    col = (NSA_WIDTH + 2 * KV_WIDTH) // w
    assert col * w == NSA_WIDTH + 2 * KV_WIDTH
    return pl.pallas_call(
        _kv_prep_kernel,
        grid=(batch, nt),
        in_specs=[pl.BlockSpec((tm, w), lambda b, i: (b * nt + i, col)),
                  pl.BlockSpec((tm, w), lambda b, i: (b * nt + i, col + 1))]
        + [pl.BlockSpec((tm, HEAD_DIM), lambda b, i: (b * nt + i, 0))] * 3,
        out_specs=[
            pl.BlockSpec((tm, 2 * KV_WIDTH), lambda b, i: (b * nt + i, 0)),
            pl.BlockSpec((None, 2 * KV_WIDTH, tm), lambda b, i: (b, 0, i)),
        ],
        out_shape=[
            jax.ShapeDtypeStruct((batch * seq, 2 * KV_WIDTH), BF16),
            jax.ShapeDtypeStruct((batch, 2 * KV_WIDTH, seq), BF16),
        ],
        compiler_params=_params(("parallel", "parallel")),
        name="nsa_kv_prep",
    )(pa, pa, *tables)


def _flash_tiles_t(qt, k_ref, vt_ref, lo, hi, mask_fn, m_ref, l_ref, acc_ref, *, tk, peel_last):
    m_ref[...] = jnp.full(m_ref.shape, NEG_INF, F32)
    l_ref[...] = jnp.zeros(l_ref.shape, F32)
    acc_ref[...] = jnp.zeros(acc_ref.shape, F32)

    def step(kt, diag):
        k0 = pl.multiple_of(kt * tk, tk)
        k = k_ref[pl.ds(k0, tk), :]
        vt = vt_ref[:, pl.ds(k0, tk)]
        s = jnp.dot(k, qt, preferred_element_type=F32)
        mask = mask_fn(kt, diag)
        mask = jnp.concatenate([mask] * NSA_HPG, axis=1)
        s = jnp.where(mask, s, NEG_INF)
        m_old = m_ref[...]
        m_new = jnp.maximum(m_old, jnp.max(s, axis=0, keepdims=True))
        m_use = jnp.where(m_new > 0.5 * NEG_INF, m_new, 0.0)
        p = jnp.exp2(s - m_use)
        alpha = jnp.exp2(m_old - m_new)
        l_ref[...] = alpha * l_ref[...] + jnp.sum(p, axis=0, keepdims=True)
        acc_ref[...] = alpha * acc_ref[...] + jnp.dot(vt, p.astype(BF16), preferred_element_type=F32)
        m_ref[...] = m_new

    def body(diag):
        def run(kt, carry):
            step(kt, diag)
            return carry
        return run

    if peel_last:
        lax.fori_loop(lo, hi - 1, body(False), 0)
        step(hi - 1, True)
    else:
        lax.fori_loop(lo, hi, body(True), 0)
    return acc_ref[...] / l_ref[...]


def _sel_attn_kernel(q_ref, c_ref, s1_ref, s2_ref, ks_ref, kw_ref, vst_ref, vwt_ref, sel_ref, oct_ref, ng_ref, o_ref,
                     m_ref, l_ref, acc_ref, gt_ref, *, tq, tk):
    gi = pl.program_id(1)
    i = pl.program_id(2)
    t0 = i * tq
    nj = NSA_HPG
    c, s1, s2 = c_ref[...], s1_ref[...], s2_ref[...]
    scale2 = ATTN_SCALE * math.log2(math.e)
    qt = _heads_t(q_ref, lambda x: _rot(x, c, s1, s2) * scale2)
    kpos = lax.broadcasted_iota(jnp.int32, (tk, tq), 0)
    tpos = t0 + lax.broadcasted_iota(jnp.int32, (tk, tq), 1)
    nb = tk // SLC_BLOCK

    def sel_mask(kt, diag):
        rows = sel_ref[pl.ds(pl.multiple_of(kt * nb, nb), nb), :]
        chosen = jnp.broadcast_to(rows[:, None, :], (nb, SLC_BLOCK, tq)).reshape(tk, tq) > 0.5
        return chosen & ((kt * tk + kpos) <= tpos) if diag else chosen

    def win_mask(kt, diag):
        kp = kt * tk + kpos
        return (kp <= tpos) & (kp > tpos - WINDOW)

    last = (t0 + tq - 1) // tk + 1
    o_slc = _flash_tiles_t(qt, ks_ref, vst_ref, 0, last, sel_mask, m_ref, l_ref, acc_ref, tk=tk, peel_last=True)
    first = jnp.maximum(t0 - (WINDOW - 1), 0) // tk
    o_win = _flash_tiles_t(qt, kw_ref, vwt_ref, first, last, win_mask, m_ref, l_ref, acc_ref, tk=tk, peel_last=False)
    gt_ref[...] = jax.nn.sigmoid(ng_ref[...].T)
    for j in range(nj):
        gate = [gt_ref[pl.ds(br * NSA_HEADS + gi * nj + j, 1), :] for br in range(3)]
        ot = (gate[0] * oct_ref[j * HEAD_DIM:(j + 1) * HEAD_DIM, :] + gate[1] * o_slc[:, j * tq:(j + 1) * tq]
              + gate[2] * o_win[:, j * tq:(j + 1) * tq])
        o_ref[:, j * HEAD_DIM:(j + 1) * HEAD_DIM] = ot.T.astype(o_ref.dtype)


def _sel_attn(pa, tables, kn, vt, sel, o_cmp_t, png, batch, seq, tq=256, tk=512):
    g = NSA_KV_GROUPS
    nq = seq // tq
    gw = NSA_HPG * HEAD_DIM
    tk = min(tk, seq)
    assert tk % tq == 0 and seq % tk == 0
    nsel = seq // SLC_BLOCK
    n = NSA_HPG * tq
    tab = pl.BlockSpec((tq, HEAD_DIM), lambda b, gi, i: (b * nq + i, 0))
    return pl.pallas_call(
        functools.partial(_sel_attn_kernel, tq=tq, tk=tk),
        grid=(batch, g, nq),
        in_specs=[
            pl.BlockSpec((tq, gw), lambda b, gi, i: (b * nq + i, gi)),
            tab, tab, tab,
            pl.BlockSpec((seq, HEAD_DIM), lambda b, gi, i: (b, gi)),
            pl.BlockSpec((seq, HEAD_DIM), lambda b, gi, i: (b, g + gi)),
            pl.BlockSpec((None, HEAD_DIM, seq), lambda b, gi, i: (b, gi, 0)),
            pl.BlockSpec((None, HEAD_DIM, seq), lambda b, gi, i: (b, g + gi, 0)),
            pl.BlockSpec((None, None, nsel, tq), lambda b, gi, i: (b, gi, 0, i)),
            pl.BlockSpec((None, gw, tq), lambda b, gi, i: (b, gi, i)),
            pl.BlockSpec((tq, LANES), lambda b, gi, i: (b * nq + i, 0)),
        ],
        out_specs=pl.BlockSpec((tq, gw), lambda b, gi, i: (b * nq + i, gi)),
        out_shape=jax.ShapeDtypeStruct((batch * seq, NSA_WIDTH), BF16),
        scratch_shapes=[
            pltpu.VMEM((1, n), F32),
            pltpu.VMEM((1, n), F32),
            pltpu.VMEM((HEAD_DIM, n), F32),
            pltpu.VMEM((LANES, tq), F32),
        ],
        compiler_params=_params(("parallel", "parallel", "parallel")),
        name="nsa_sel_win_attn",
    )(pa, *tables, kn, kn, vt, vt, sel, o_cmp_t, png)


HG_TILE = 128
HG_GROUP = 8
HG_LEVELS = (64, 32, 16, 8)
HG_HEADS_PER_STEP = 8


def _split3(x):
    hi = x.astype(BF16)
    r = x - hi.astype(F32)
    mid = r.astype(BF16)
    lo = (r - mid.astype(F32)).astype(BF16)
    return hi, mid, lo


def _group_row(x, size, row):
    t, d = x.shape
    x3 = x.reshape(t // size, size, d)
    return jnp.broadcast_to(x3[:, row:row + 1, :], x3.shape).reshape(t, d)


def _hgrn_heads(qs, zs, vs, gs, lbs, nw, low, sts):
    t = HG_TILE
    nh = len(qs)
    heads = range(nh)
    log2e = math.log2(math.e)
    kks, l2fs = [], []
    for i in heads:
        z, lb = zs[i], lbs[i]
        ez = jnp.exp(-jnp.abs(z))
        r = 1.0 / (1.0 + ez)
        pos = z >= 0.0
        f = lb + (1.0 - lb) * jnp.where(pos, r, ez * r)
        kks.append((1.0 - lb) * jnp.where(pos, ez * r, r))
        l2fs.append(jnp.where(f > 0.0, jnp.log2(f), jnp.minimum(z, 0.0) * log2e))
    bs_ = []
    for i in heads:
        hi, mid, lo = _split3(l2fs[i])
        bs_.append(jnp.dot(low, hi, preferred_element_type=F32) + jnp.dot(low, mid, preferred_element_type=F32)
                   + jnp.dot(low, lo, preferred_element_type=F32))
    ti = lax.broadcasted_iota(jnp.int32, (t, t), 0)
    si = lax.broadcasted_iota(jnp.int32, (t, t), 1)
    atts = [jnp.zeros((t, t), F32) for _ in heads]
    for h in HG_LEVELS:
        pair = ((ti // (2 * h)) == (si // (2 * h))) & ((ti % (2 * h)) >= h) & ((si % (2 * h)) < h)
        for i in heads:
            b = bs_[i]
            bref = _group_row(b, 2 * h, h - 1)
            e = jnp.exp2(-jnp.abs(b - bref))
            atts[i] = atts[i] + jnp.where(pair, _nt_dot((qs[i] * e).astype(BF16), (kks[i] * e).astype(BF16)), 0.0)
    lane_d = lax.broadcasted_iota(jnp.int32, (HGRN_DK, t), 1)
    diags = [jnp.zeros((t, t), F32) for _ in heads]
    for sg in range(HG_GROUP):
        spread = jnp.where((lane_d % HG_GROUP) == sg, 1.0, 0.0).astype(BF16)
        for i in heads:
            bsg = _group_row(bs_[i], HG_GROUP, sg)
            ksg = _group_row(kks[i], HG_GROUP, sg)
            p = (qs[i] * jnp.exp2(jnp.minimum(bs_[i] - bsg, 0.0))) * ksg
            diags[i] = diags[i] + jnp.dot(p.astype(BF16), spread, preferred_element_type=F32)
    same = ((ti // HG_GROUP) == (si // HG_GROUP)) & (si <= ti)
    ys, st_new = [], []
    for i in heads:
        b = bs_[i]
        att = atts[i] + jnp.where(same, diags[i], 0.0)
        vb = vs[i].astype(BF16)
        o = jnp.dot(att.astype(BF16), vb, preferred_element_type=F32)
        o = o + _nt_dot((qs[i] * jnp.exp2(b)).astype(BF16), sts[i].astype(BF16))
        b_last = b[t - 1:t, :]
        kdec = (kks[i] * jnp.exp2(b_last - b)).astype(BF16)
        st_new.append(sts[i] * jnp.exp2(b_last) + _tn_dot(vb, kdec))
        y = o * lax.rsqrt(jnp.mean(o * o, axis=-1, keepdims=True) + NORM_EPS) * nw
        g = gs[i]
        ys.append(y * (g * jax.nn.sigmoid(g)))
    return ys, st_new


def _hgrn_kernel(q_ref, z_ref, v_ref, g_ref, lb_ref, nw_ref, low_ref, o_ref, st_ref, *, nh):
    @pl.when(pl.program_id(2) == 0)
    def _():
        st_ref[...] = jnp.zeros_like(st_ref)

    cols = [slice(hd * HGRN_DK, (hd + 1) * HGRN_DK) for hd in range(nh)]
    ys, st_new = _hgrn_heads([q_ref[:, c] for c in cols], [z_ref[:, c] for c in cols], [v_ref[:, c] for c in cols],
                             [g_ref[:, c] for c in cols], [lb_ref[hd:hd + 1, :] for hd in range(nh)],
                             nw_ref[...], low_ref[...], [st_ref[hd] for hd in range(nh)])
    for hd in range(nh):
        st_ref[hd] = st_new[hd]
        o_ref[:, cols[hd]] = ys[hd].astype(o_ref.dtype)


def _hgrn(pb, lb, norm_w, batch, seq):
    hh = HGRN_HEADS
    nh = HG_HEADS_PER_STEP
    nt = seq // HG_TILE
    low = jnp.asarray(np.tril(np.ones((HG_TILE, HG_TILE), np.float32)), dtype=BF16)
    spec = lambda part: pl.BlockSpec((HG_TILE, nh * HGRN_DK), lambda b, h, i: (b * nt + i, part * (hh // nh) + h))
    return pl.pallas_call(
        functools.partial(_hgrn_kernel, nh=nh),
        grid=(batch, hh // nh, nt),
        in_specs=[
            spec(0), spec(1), spec(2), spec(3),
            pl.BlockSpec((None, nh, HGRN_DK), lambda b, h, i: (h, 0, 0)),
            pl.BlockSpec((1, HGRN_DV), lambda b, h, i: (0, 0)),
            pl.BlockSpec(low.shape, lambda b, h, i: (0, 0)),
        ],
        out_specs=pl.BlockSpec((HG_TILE, nh * HGRN_DV), lambda b, h, i: (b * nt + i, h)),
        out_shape=jax.ShapeDtypeStruct((batch * seq, HGRN_VW), BF16),
        scratch_shapes=[pltpu.VMEM((nh, HGRN_DV, HGRN_DK), F32)],
        compiler_params=_params(("parallel", "parallel", "arbitrary")),
        name="hgrn2",
    )(pb, pb, pb, pb, lb.reshape(hh // nh, nh, HGRN_DK), norm_w.reshape(1, HGRN_DV), low)


def _hybrid_mixer(h, tables, layer, w_in_t, cmp_pos, cmp_w1, cmp_w2, lb, g_norm_w, w_up_a, w_up_b, batch, seq):
    pa = _matmul_t(h, w_in_t, layer, 0, SEG_A, F32)
    png = _matmul_t(h, w_in_t, layer, SEG_A, LANES, F32, tn=LANES)
    pb = _matmul_t(h, w_in_t, layer, SEG_A + SEG_G, SEG_B, F32)
    kc, vct = _compress(pa, cmp_pos, cmp_w1, cmp_w2, batch, seq)
    o_cmp_t, sel = _cmp_attn(pa, kc, vct, batch, seq)
    kn, vt = _kv_prep(pa, tables, batch, seq)
    o_nsa = _sel_attn(pa, tables, kn, vt, sel, o_cmp_t, png, batch, seq)
    o_hgrn = _hgrn(pb, lb, g_norm_w, batch, seq)
    return _matmul_merge(o_nsa, o_hgrn, w_up_a, w_up_b, layer, pb, 2 * HGRN_KW + 2 * HGRN_VW,
                         2 * HGRN_KW + 2 * HGRN_VW + D_MODEL)


def kernel(x, c, positions, ada_w, ada_b, norm_mix_w, w_in, nsa_cmp_pos, nsa_cmp_w1, nsa_cmp_w2, hgrn_lb_logits,
           hgrn_norm_w, w_up_a, w_up_b, w_out, norm_mlp_w, w_mlp1, w_mlp2, final_norm_w):
    batch, seq, d = x.shape
    depth = ada_w.shape[0]
    lb_all = jnp.cumsum(jax.nn.softmax(hgrn_lb_logits.astype(F32), axis=0), axis=0)
    lb_all = lb_all - lb_all[0:1]
    mod = _ada_mod(c, ada_w, ada_b)
    tables = _rope_tables(positions)
    x2 = x.reshape(batch * seq, d)
    zeros = jnp.zeros((batch, d), F32)
    w_in_t = jnp.transpose(w_in, (0, 2, 1))
    for l in range(depth):
        sh1, sc1, g1, sh2, sc2, g2 = [mod[l][:, k * d:(k + 1) * d] for k in range(6)]
        h = _norm_mod(x2, norm_mix_w[l], sc1, sh1, seq, BF16)
        y = _hybrid_mixer(h, tables, l, w_in_t, nsa_cmp_pos[l], nsa_cmp_w1[l], nsa_cmp_w2[l], lb_all[l],
                          hgrn_norm_w[l], w_up_a, w_up_b, batch, seq)
        x2 = _matmul_resid(y, w_out, l, x2, g1, seq)
        h = _norm_mod(x2, norm_mlp_w[l], sc2, sh2, seq, BF16)
        u = _matmul(h, w_mlp1, l, BF16, relu2=True)
        x2 = _matmul_resid_ksplit(u, w_mlp2, l, x2, g2, seq)
    out = _norm_mod(x2, final_norm_w, zeros, zeros, seq, F32)
    return out.reshape(batch, seq, d)
```

```python
import functools
import math

import numpy as np
import jax
import jax.numpy as jnp
from jax import lax
from jax.experimental import pallas as pl
from jax.experimental.pallas import tpu as pltpu

D_MODEL = 4096
DEPTH = 2
NSA_HEADS = 16
NSA_KV_GROUPS = 4
NSA_HPG = NSA_HEADS // NSA_KV_GROUPS
HEAD_DIM = 128
CMP_LEN = 32
CMP_STRIDE = 16
SLC_BLOCK = 64
SLC_TOPK = 16
WINDOW = 512
ATTN_SCALE = HEAD_DIM ** -0.5
FORCED_BONUS = 1e6
NEG_INF = -1e30
HGRN_HEADS = 16
HGRN_DK = 128
HGRN_DV = 128
ROPE_THETA = 500000.0
ROT_DIM = HEAD_DIM // 4
D_FF = 4 * D_MODEL
NORM_EPS = 1e-6

NSA_WIDTH = NSA_HEADS * HEAD_DIM
KV_WIDTH = NSA_KV_GROUPS * HEAD_DIM
HGRN_KW = HGRN_HEADS * HGRN_DK
HGRN_VW = HGRN_HEADS * HGRN_DV
SEG_A = NSA_WIDTH + 6 * KV_WIDTH
SEG_G = 3 * NSA_HEADS
SEG_B = 2 * HGRN_KW + 2 * HGRN_VW + 2 * D_MODEL

LANES = 128
VMEM_LIMIT = 56 * 1024 * 1024

BF16 = jnp.bfloat16
F32 = jnp.float32


def _params(sem):
    return pltpu.CompilerParams(dimension_semantics=sem, vmem_limit_bytes=VMEM_LIMIT)


def _nt_dot(a, b, precision=None):
    return lax.dot_general(a, b, (((1,), (1,)), ((), ())), preferred_element_type=F32, precision=precision)


def _tn_dot(a, b, precision=None):
    return lax.dot_general(a, b, (((0,), (0,)), ((), ())), preferred_element_type=F32, precision=precision)


def _ada_kernel(cb_ref, w_ref, b_ref, o_ref, cact_ref, *, batch, tn):
    @pl.when((pl.program_id(0) == 0) & (pl.program_id(1) == 0))
    def _():
        cb = cb_ref[...]
        cact_ref[...] = cb * jax.nn.sigmoid(cb)

    for b in range(batch):
        cact = cact_ref[b]
        rows = []
        for s in range(tn // LANES):
            w = w_ref[:, s * LANES:(s + 1) * LANES]
            rows.append(jnp.sum(w * cact, axis=0, keepdims=True))
        o_ref[b:b + 1, :] = jnp.concatenate(rows, axis=1) + b_ref[...]


def _ada_mod(c, ada_w, ada_b):
    depth, k, n = ada_w.shape
    batch = c.shape[0]
    tn = 512
    cb = jnp.broadcast_to(c[:, :, None], (batch, k, LANES))
    out = pl.pallas_call(
        functools.partial(_ada_kernel, batch=batch, tn=tn),
        grid=(depth, n // tn),
        in_specs=[
            pl.BlockSpec((batch, k, LANES), lambda l, j: (0, 0, 0)),
            pl.BlockSpec((None, k, tn), lambda l, j: (l, 0, j)),
            pl.BlockSpec((None, 1, tn), lambda l, j: (l, 0, j)),
        ],
        out_specs=pl.BlockSpec((None, batch, tn), lambda l, j: (l, 0, j)),
        out_shape=jax.ShapeDtypeStruct((depth, batch, n), F32),
        scratch_shapes=[pltpu.VMEM((batch, k, LANES), F32)],
        compiler_params=_params(("arbitrary", "arbitrary")),
        name="ada_mod",
    )(cb, ada_w, ada_b.reshape(depth, 1, n))
    return out


def _norm_kernel(x_ref, w_ref, sc_ref, sh_ref, o_ref):
    x = x_ref[...]
    y = x * lax.rsqrt(jnp.mean(x * x, axis=-1, keepdims=True) + NORM_EPS)
    y = y * w_ref[...]
    y = y * (1.0 + sc_ref[...]) + sh_ref[...]
    o_ref[...] = y.astype(o_ref.dtype)


def _norm_mod(x2, w, sc, sh, seq, out_dtype):
    m, d = x2.shape
    batch = m // seq
    tm = 256
    per = seq // tm
    return pl.pallas_call(
        _norm_kernel,
        grid=(m // tm,),
        in_specs=[
            pl.BlockSpec((tm, d), lambda i: (i, 0)),
            pl.BlockSpec((1, d), lambda i: (0, 0)),
            pl.BlockSpec((None, 1, d), lambda i: (i // per, 0, 0)),
            pl.BlockSpec((None, 1, d), lambda i: (i // per, 0, 0)),
        ],
        out_specs=pl.BlockSpec((tm, d), lambda i: (i, 0)),
        out_shape=jax.ShapeDtypeStruct((m, d), out_dtype),
        compiler_params=_params(("parallel",)),
        name="norm_mod",
    )(x2, w.reshape(1, d), sc.reshape(batch, 1, d), sh.reshape(batch, 1, d))


MM_TM = 1024
MM_TN = 512
MM_TK = 4096


def _mm_kernel(a_ref, w_ref, o_ref, wb_ref, *, relu2):
    @pl.when(pl.program_id(1) == 0)
    def _():
        wb_ref[...] = w_ref[...].astype(BF16)

    r = jnp.dot(a_ref[...], wb_ref[...], preferred_element_type=F32)
    if relu2:
        r = jnp.square(jnp.maximum(r, 0.0))
    o_ref[...] = r.astype(o_ref.dtype)


def _matmul(a, w, layer, out_dtype, relu2=False, tn=MM_TN):
    m, kd = a.shape
    n = w.shape[2]
    tm = min(MM_TM, m)
    assert n % tn == 0 and m % tm == 0
    return pl.pallas_call(
        functools.partial(_mm_kernel, relu2=relu2),
        grid=(n // tn, m // tm),
        in_specs=[
            pl.BlockSpec((tm, kd), lambda j, i: (i, 0)),
            pl.BlockSpec((None, kd, tn), lambda j, i: (layer, 0, j)),
        ],
        out_specs=pl.BlockSpec((tm, tn), lambda j, i: (i, j)),
        out_shape=jax.ShapeDtypeStruct((m, n), out_dtype),
        scratch_shapes=[pltpu.VMEM((kd, tn), BF16)],
        compiler_params=_params(("parallel", "arbitrary")),
        name="matmul",
    )(a, w)


def _mm_t_kernel(*refs, shift):
    if shift:
        a_ref, w_ref, wn_ref, o_ref, wb_ref = refs
    else:
        a_ref, w_ref, o_ref, wb_ref = refs

    @pl.when(pl.program_id(1) == 0)
    def _():
        tn = w_ref.shape[0]
        wb_ref[0:tn - shift, :] = w_ref[shift:tn, :].astype(BF16)
        if shift:
            wb_ref[tn - shift:tn, :] = wn_ref[0:shift, :].astype(BF16)

    o_ref[...] = _nt_dot(a_ref[...], wb_ref[...]).astype(o_ref.dtype)


BF16_SUBLANES = 16


def _matmul_t(a, wt, layer, row0, nrows, out_dtype, tn=MM_TN):
    m, kd = a.shape
    tm = min(MM_TM, m)
    shift = row0 % tn
    base = row0 - shift
    nxt = 64
    assert nrows % tn == 0 and m % tm == 0 and shift % BF16_SUBLANES == 0 and shift <= nxt and tn % nxt == 0
    in_specs = [
        pl.BlockSpec((tm, kd), lambda j, i: (i, 0)),
        pl.BlockSpec((None, tn, kd), lambda j, i: (layer, base // tn + j, 0)),
    ]
    args = [a, wt]
    if shift:
        in_specs.append(pl.BlockSpec((None, nxt, kd), lambda j, i: (layer, (base + tn * (j + 1)) // nxt, 0)))
        args.append(wt)
    return pl.pallas_call(
        functools.partial(_mm_t_kernel, shift=shift),
        grid=(nrows // tn, m // tm),
        in_specs=in_specs,
        out_specs=pl.BlockSpec((tm, tn), lambda j, i: (i, j)),
        out_shape=jax.ShapeDtypeStruct((m, nrows), out_dtype),
        scratch_shapes=[pltpu.VMEM((tn, kd), BF16)],
        compiler_params=_params(("parallel", "arbitrary")),
        name="matmul_t",
    )(*args)


def _mm_resid_kernel(a_ref, w_ref, x_ref, g_ref, o_ref, wb_ref):
    @pl.when(pl.program_id(1) == 0)
    def _():
        wb_ref[...] = w_ref[...].astype(BF16)

    o_ref[...] = x_ref[...] + g_ref[...] * jnp.dot(a_ref[...], wb_ref[...], preferred_element_type=F32)


def _matmul_resid(a, w, layer, x2, gate, seq, tn=MM_TN, tk=None, ks=0):
    m = a.shape[0]
    kd = tk if tk is not None else a.shape[1]
    n = w.shape[2]
    tm = min(MM_TM, seq)
    per = seq // tm
    batch = m // seq
    return pl.pallas_call(
        _mm_resid_kernel,
        grid=(n // tn, m // tm),
        in_specs=[
            pl.BlockSpec((tm, kd), lambda j, i: (i, ks)),
            pl.BlockSpec((None, kd, tn), lambda j, i: (layer, ks, j)),
            pl.BlockSpec((tm, tn), lambda j, i: (i, j)),
            pl.BlockSpec((None, 1, tn), lambda j, i: (i // per, 0, j)),
        ],
        out_specs=pl.BlockSpec((tm, tn), lambda j, i: (i, j)),
        out_shape=jax.ShapeDtypeStruct((m, n), F32),
        scratch_shapes=[pltpu.VMEM((kd, tn), BF16)],
        compiler_params=_params(("parallel", "arbitrary")),
        name="matmul_resid",
    )(a, w, x2, gate.reshape(batch, 1, n))


def _matmul_resid_ksplit(a, w, layer, x2, gate, seq, tk=MM_TK):
    kd = a.shape[1]
    for ks in range(kd // tk):
        x2 = _matmul_resid(a, w, layer, x2, gate, seq, tk=tk, ks=ks)
    return x2


def _mm_merge_kernel(a_ref, b_ref, wa_ref, wb_ref, ga_ref, gb_ref, o_ref, wab_ref, wbb_ref):
    @pl.when(pl.program_id(1) == 0)
    def _():
        wab_ref[...] = wa_ref[...].astype(BF16)
        wbb_ref[...] = wb_ref[...].astype(BF16)

    ya = jnp.dot(a_ref[...], wab_ref[...], preferred_element_type=F32)
    yb = jnp.dot(b_ref[...], wbb_ref[...], preferred_element_type=F32)
    o_ref[...] = (jax.nn.sigmoid(ga_ref[...]) * ya + jax.nn.sigmoid(gb_ref[...]) * yb).astype(o_ref.dtype)


def _matmul_merge(a, b, wa, wb, layer, pb, ga_col, gb_col, tn=MM_TN):
    m, kd = a.shape
    n = wa.shape[2]
    tm = min(MM_TM, m)
    return pl.pallas_call(
        _mm_merge_kernel,
        grid=(n // tn, m // tm),
        in_specs=[
            pl.BlockSpec((tm, kd), lambda j, i: (i, 0)),
            pl.BlockSpec((tm, kd), lambda j, i: (i, 0)),
            pl.BlockSpec((None, kd, tn), lambda j, i: (layer, 0, j)),
            pl.BlockSpec((None, kd, tn), lambda j, i: (layer, 0, j)),
            pl.BlockSpec((tm, tn), lambda j, i: (i, ga_col // tn + j)),
            pl.BlockSpec((tm, tn), lambda j, i: (i, gb_col // tn + j)),
        ],
        out_specs=pl.BlockSpec((tm, tn), lambda j, i: (i, j)),
        out_shape=jax.ShapeDtypeStruct((m, n), BF16),
        scratch_shapes=[pltpu.VMEM((kd, tn), BF16), pltpu.VMEM((kd, tn), BF16)],
        compiler_params=_params(("parallel", "arbitrary")),
        name="matmul_merge",
    )(a, b, wa, wb, pb, pb)


def _gelu_tanh(x):
    c = math.sqrt(2.0 / math.pi)
    return 0.5 * x * (1.0 + jnp.tanh(c * (x + 0.044715 * (x * x * x))))


def _compress_one(x_ref, pos_ref, w1_ref, w2_ref, kv, nhalf):
    hp = lax.Precision.HIGHEST
    half = CMP_LEN // 2
    acc_a = jnp.zeros((nhalf, HEAD_DIM), F32)
    acc_b = jnp.zeros((nhalf, HEAD_DIM), F32)
    for l in range(half):
        xl = x_ref[pl.ds(l, nhalf, stride=CMP_STRIDE), :]
        acc_a = acc_a + jnp.dot(xl + pos_ref[kv, l:l + 1, :], w1_ref[kv, l], precision=hp, preferred_element_type=F32)
        acc_b = acc_b + jnp.dot(xl + pos_ref[kv, half + l:half + l + 1, :], w1_ref[kv, half + l], precision=hp,
                                preferred_element_type=F32)
    hid = acc_a + pltpu.roll(acc_b, nhalf - 1, 0)
    hid = _gelu_tanh(hid)
    return jnp.dot(hid, w2_ref[kv], precision=hp, preferred_element_type=F32)


def _compress_kernel(xk_ref, xv_ref, pos_ref, w1_ref, w2_ref, kc_ref, vct_ref, *, nhalf):
    kc_ref[...] = _compress_one(xk_ref, pos_ref, w1_ref, w2_ref, 0, nhalf)
    vct_ref[...] = _compress_one(xv_ref, pos_ref, w1_ref, w2_ref, 1, nhalf).T


def _compress(pa, pos, w1, w2, batch, seq):
    nhalf = seq // CMP_STRIDE
    g = NSA_KV_GROUPS
    col0 = NSA_WIDTH // HEAD_DIM
    full = lambda shape: pl.BlockSpec(shape, lambda b, gi: (0,) * len(shape))
    return pl.pallas_call(
        functools.partial(_compress_kernel, nhalf=nhalf),
        grid=(batch, g),
        in_specs=[
            pl.BlockSpec((seq, HEAD_DIM), lambda b, gi: (b, col0 + gi)),
            pl.BlockSpec((seq, HEAD_DIM), lambda b, gi: (b, col0 + g + gi)),
            full(pos.shape), full(w1.shape), full(w2.shape),
        ],
        out_specs=[
            pl.BlockSpec((None, None, nhalf, HEAD_DIM), lambda b, gi: (b, gi, 0, 0)),
            pl.BlockSpec((None, None, HEAD_DIM, nhalf), lambda b, gi: (b, gi, 0, 0)),
        ],
        out_shape=[
            jax.ShapeDtypeStruct((batch, g, nhalf, HEAD_DIM), F32),
            jax.ShapeDtypeStruct((batch, g, HEAD_DIM, nhalf), F32),
        ],
        compiler_params=_params(("parallel", "parallel")),
        name="nsa_compress",
    )(pa, pa, pos, w1, w2)


def _cmp_slc_overlap_t(seq, ncp):
    nc = (seq - CMP_LEN) // CMP_STRIDE + 1
    nsel = seq // SLC_BLOCK
    cs = np.arange(nc) * CMP_STRIDE
    ce = cs + CMP_LEN - 1
    ss = np.arange(nsel) * SLC_BLOCK
    se = ss + SLC_BLOCK - 1
    ov = np.minimum(ce[:, None], se[None, :]) - np.maximum(cs[:, None], ss[None, :]) + 1
    ov = np.maximum(ov, 0).astype(np.float32)
    out = np.zeros((nsel, ncp), np.float32)
    out[:, :nc] = ov.T
    return out


def _heads_t(x_ref, fn=None):
    parts = []
    for j in range(NSA_HPG):
        x = x_ref[:, j * HEAD_DIM:(j + 1) * HEAD_DIM]
        if fn is not None:
            x = fn(x)
        parts.append(x.T.astype(BF16))
    return jnp.concatenate(parts, axis=1)


def _cmp_attn_kernel(q_ref, kc_ref, vct_ref, ovt_ref, o_ref, sel_ref, *, tq, ncp, nsel):
    i = pl.program_id(2)
    t0 = i * tq
    tpos = t0 + lax.broadcasted_iota(jnp.int32, (ncp, tq), 1)
    nidx = lax.broadcasted_iota(jnp.int32, (ncp, tq), 0)
    vis = (nidx * CMP_STRIDE + (CMP_LEN - 1)) <= tpos
    qt = _heads_t(q_ref)
    st = jnp.dot(kc_ref[...].astype(BF16), qt, preferred_element_type=F32) * ATTN_SCALE
    vct = vct_ref[...].astype(BF16)
    imp = jnp.zeros((ncp, tq), F32)
    for j in range(NSA_HPG):
        s = jnp.where(vis, st[:, j * tq:(j + 1) * tq], NEG_INF)
        e = jnp.exp(s - jnp.max(s, axis=0, keepdims=True))
        p = e / jnp.sum(e, axis=0, keepdims=True)
        p = jnp.where(vis, p, 0.0)
        o_ref[j * HEAD_DIM:(j + 1) * HEAD_DIM, :] = jnp.dot(vct, p.astype(BF16), preferred_element_type=F32)
        imp = imp + p
    pslc = jnp.dot(ovt_ref[...], imp, precision=lax.Precision.HIGHEST, preferred_element_type=F32)
    tl = t0 + lax.broadcasted_iota(jnp.int32, (nsel, tq), 1)
    mi = lax.broadcasted_iota(jnp.int32, (nsel, tq), 0)
    cur = tl // SLC_BLOCK
    valid = mi * SLC_BLOCK <= tl
    forced = (mi == 0) | (mi == cur) | (mi == cur - 1)
    score = jnp.where(valid, pslc + jnp.where(forced, FORCED_BONUS, 0.0), -jnp.inf)
    topk = min(SLC_TOPK, nsel)
    few = (t0 + tq) <= topk * SLC_BLOCK

    @pl.when(few)
    def _():
        sel_ref[...] = jnp.where(valid, 1.0, 0.0)

    @pl.when(jnp.logical_not(few))
    def _():
        rank = jnp.zeros((nsel, tq), F32)
        for mp in range(nsel):
            row = score[mp:mp + 1, :]
            before = (row > score) | ((row == score) & (mi > mp))
            rank = rank + jnp.where(before, 1.0, 0.0)
        sel_ref[...] = jnp.where(valid & (rank < float(topk)), 1.0, 0.0)


def _cmp_attn(pa, kc, vct, batch, seq, tq=256):
    ncp = seq // CMP_STRIDE
    nsel = seq // SLC_BLOCK
    g = NSA_KV_GROUPS
    nq = seq // tq
    gw = NSA_HPG * HEAD_DIM
    ovt = jnp.asarray(_cmp_slc_overlap_t(seq, ncp))
    return pl.pallas_call(
        functools.partial(_cmp_attn_kernel, tq=tq, ncp=ncp, nsel=nsel),
        grid=(batch, g, nq),
        in_specs=[
            pl.BlockSpec((tq, gw), lambda b, gi, i: (b * nq + i, gi)),
            pl.BlockSpec((None, None, ncp, HEAD_DIM), lambda b, gi, i: (b, gi, 0, 0)),
            pl.BlockSpec((None, None, HEAD_DIM, ncp), lambda b, gi, i: (b, gi, 0, 0)),
            pl.BlockSpec((nsel, ncp), lambda b, gi, i: (0, 0)),
        ],
        out_specs=[
            pl.BlockSpec((None, gw, tq), lambda b, gi, i: (b, gi, i)),
            pl.BlockSpec((None, None, nsel, tq), lambda b, gi, i: (b, gi, 0, i)),
        ],
        out_shape=[
            jax.ShapeDtypeStruct((batch, NSA_WIDTH, seq), F32),
            jax.ShapeDtypeStruct((batch, g, nsel, seq), F32),
        ],
        compiler_params=_params(("parallel", "parallel", "parallel")),
        name="nsa_cmp_attn",
    )(pa, kc, vct, ovt)


def _rope_tables(positions):
    half = ROT_DIM // 2
    inv = ROPE_THETA ** (-jnp.arange(0, ROT_DIM, 2, dtype=F32) / ROT_DIM)
    ang = positions.astype(F32)[..., None] * inv
    cos, sin = jnp.cos(ang), jnp.sin(ang)
    rest = HEAD_DIM - ROT_DIM
    b, s = positions.shape
    one = jnp.ones((b, s, rest), F32)
    zero = jnp.zeros((b, s, rest), F32)
    zh = jnp.zeros((b, s, half), F32)
    c = jnp.concatenate([cos, cos, one], axis=-1)
    s1 = jnp.concatenate([-sin, zh, zero], axis=-1)
    s2 = jnp.concatenate([zh, sin, zero], axis=-1)
    return [t.reshape(b * s, HEAD_DIM) for t in (c, s1, s2)]


def _rot(x, c, s1, s2):
    half = ROT_DIM // 2
    return x * c + pltpu.roll(x, HEAD_DIM - half, 1) * s1 + pltpu.roll(x, half, 1) * s2


def _kv_prep_kernel(xs_ref, xw_ref, c_ref, s1_ref, s2_ref, k_ref, vt_ref):
    c, s1, s2 = c_ref[...], s1_ref[...], s2_ref[...]
    for br, x_ref in enumerate((xs_ref, xw_ref)):
        for gi in range(NSA_KV_GROUPS):
            src = gi * HEAD_DIM
            dst = br * KV_WIDTH + gi * HEAD_DIM
            k_ref[:, dst:dst + HEAD_DIM] = _rot(x_ref[:, src:src + HEAD_DIM], c, s1, s2).astype(k_ref.dtype)
            vt_ref[dst:dst + HEAD_DIM, :] = x_ref[:, KV_WIDTH + src:KV_WIDTH + src + HEAD_DIM].T.astype(vt_ref.dtype)


def _kv_prep(pa, tables, batch, seq, tm=256):
    nt = seq // tm
    w = 2 * KV_WIDTH
    col = (NSA_WIDTH + 2 * KV_WIDTH) // w
    assert col * w == NSA_WIDTH + 2 * KV_WIDTH
    return pl.pallas_call(
        _kv_prep_kernel,
        grid=(batch, nt),
        in_specs=[pl.BlockSpec((tm, w), lambda b, i: (b * nt + i, col)),
                  pl.BlockSpec((tm, w), lambda b, i: (b * nt + i, col + 1))]
        + [pl.BlockSpec((tm, HEAD_DIM), lambda b, i: (b * nt + i, 0))] * 3,
        out_specs=[
            pl.BlockSpec((tm, 2 * KV_WIDTH), lambda b, i: (b * nt + i, 0)),
            pl.BlockSpec((None, 2 * KV_WIDTH, tm), lambda b, i: (b, 0, i)),
        ],
        out_shape=[
            jax.ShapeDtypeStruct((batch * seq, 2 * KV_WIDTH), BF16),
            jax.ShapeDtypeStruct((batch, 2 * KV_WIDTH, seq), BF16),
        ],
        compiler_params=_params(("parallel", "parallel")),
        name="nsa_kv_prep",
    )(pa, pa, *tables)


def _scores_t(k_ref, qt, kt, tk):
    k0 = pl.multiple_of(kt * tk, tk)
    return jnp.dot(k_ref[pl.ds(k0, tk), :], qt, preferred_element_type=F32)


def _flash_init(m_ref, l_ref, acc_ref):
    m_ref[...] = jnp.full(m_ref.shape, NEG_INF, F32)
    l_ref[...] = jnp.zeros(l_ref.shape, F32)
    acc_ref[...] = jnp.zeros(acc_ref.shape, F32)


def _flash_consume_t(s_ref, vt_ref, kt, mask, m_ref, l_ref, acc_ref, tk):
    k0 = pl.multiple_of(kt * tk, tk)
    vt = vt_ref[:, pl.ds(k0, tk)]
    mask = jnp.concatenate([mask] * NSA_HPG, axis=1)
    s = jnp.where(mask, s_ref[...], NEG_INF)
    m_old = m_ref[...]
    m_new = jnp.maximum(m_old, jnp.max(s, axis=0, keepdims=True))
    m_use = jnp.where(m_new > 0.5 * NEG_INF, m_new, 0.0)
    p = jnp.exp2(s - m_use)
    alpha = jnp.exp2(m_old - m_new)
    l_ref[...] = alpha * l_ref[...] + jnp.sum(p, axis=0, keepdims=True)
    acc_ref[...] = alpha * acc_ref[...] + jnp.dot(vt, p.astype(BF16), preferred_element_type=F32)
    m_ref[...] = m_new


def _sel_attn_kernel(q_ref, c_ref, s1_ref, s2_ref, ks_ref, kw_ref, vst_ref, vwt_ref, sel_ref, oct_ref, ng_ref, o_ref,
                     m_ref, l_ref, acc_ref, gt_ref, s_ref, *, tq, tk):
    gi = pl.program_id(1)
    i = pl.program_id(2)
    t0 = i * tq
    nj = NSA_HPG
    c, s1, s2 = c_ref[...], s1_ref[...], s2_ref[...]
    scale2 = ATTN_SCALE * math.log2(math.e)
    qt = _heads_t(q_ref, lambda x: _rot(x, c, s1, s2) * scale2)
    kpos = lax.broadcasted_iota(jnp.int32, (tk, tq), 0)
    tpos = t0 + lax.broadcasted_iota(jnp.int32, (tk, tq), 1)
    nb = tk // SLC_BLOCK

    def sel_mask(kt, diag):
        rows = sel_ref[pl.ds(pl.multiple_of(kt * nb, nb), nb), :]
        chosen = jnp.broadcast_to(rows[:, None, :], (nb, SLC_BLOCK, tq)).reshape(tk, tq) > 0.5
        return chosen & ((kt * tk + kpos) <= tpos) if diag else chosen

    def win_mask(kt):
        kp = kt * tk + kpos
        return (kp <= tpos) & (kp > tpos - WINDOW)

    sel_st = (m_ref.at[0], l_ref.at[0], acc_ref.at[0])
    win_st = (m_ref.at[1], l_ref.at[1], acc_ref.at[1])
    _flash_init(*sel_st)
    _flash_init(*win_st)
    last = (t0 + tq - 1) // tk + 1
    first = jnp.maximum(t0 - (WINDOW - 1), 0) // tk

    s_ref[...] = _scores_t(ks_ref, qt, 0, tk)

    def sel_trip(kt, carry):
        s_next = _scores_t(ks_ref, qt, kt + 1, tk)
        _flash_consume_t(s_ref, vst_ref, kt, sel_mask(kt, False), *sel_st, tk)
        s_ref[...] = s_next
        return carry

    lax.fori_loop(0, last - 1, sel_trip, 0)
    s_next = _scores_t(kw_ref, qt, first, tk)
    _flash_consume_t(s_ref, vst_ref, last - 1, sel_mask(last - 1, True), *sel_st, tk)
    s_ref[...] = s_next

    def win_trip(kt, carry):
        s_next = _scores_t(kw_ref, qt, kt + 1, tk)
        _flash_consume_t(s_ref, vwt_ref, kt, win_mask(kt), *win_st, tk)
        s_ref[...] = s_next
        return carry

    lax.fori_loop(first, last - 1, win_trip, 0)
    _flash_consume_t(s_ref, vwt_ref, last - 1, win_mask(last - 1), *win_st, tk)
    o_slc = acc_ref[0] / l_ref[0]
    o_win = acc_ref[1] / l_ref[1]
    gt_ref[...] = jax.nn.sigmoid(ng_ref[...].T)
    for j in range(nj):
        gate = [gt_ref[pl.ds(br * NSA_HEADS + gi * nj + j, 1), :] for br in range(3)]
        ot = (gate[0] * oct_ref[j * HEAD_DIM:(j + 1) * HEAD_DIM, :] + gate[1] * o_slc[:, j * tq:(j + 1) * tq]
              + gate[2] * o_win[:, j * tq:(j + 1) * tq])
        o_ref[:, j * HEAD_DIM:(j + 1) * HEAD_DIM] = ot.T.astype(o_ref.dtype)


def _sel_attn(pa, tables, kn, vt, sel, o_cmp_t, png, batch, seq, tq=256, tk=512):
    g = NSA_KV_GROUPS
    nq = seq // tq
    gw = NSA_HPG * HEAD_DIM
    tk = min(tk, seq)
    assert tk % tq == 0 and seq % tk == 0
    nsel = seq // SLC_BLOCK
    n = NSA_HPG * tq
    tab = pl.BlockSpec((tq, HEAD_DIM), lambda b, gi, i: (b * nq + i, 0))
    return pl.pallas_call(
        functools.partial(_sel_attn_kernel, tq=tq, tk=tk),
        grid=(batch, g, nq),
        in_specs=[
            pl.BlockSpec((tq, gw), lambda b, gi, i: (b * nq + i, gi)),
            tab, tab, tab,
            pl.BlockSpec((seq, HEAD_DIM), lambda b, gi, i: (b, gi)),
            pl.BlockSpec((seq, HEAD_DIM), lambda b, gi, i: (b, g + gi)),
            pl.BlockSpec((None, HEAD_DIM, seq), lambda b, gi, i: (b, gi, 0)),
            pl.BlockSpec((None, HEAD_DIM, seq), lambda b, gi, i: (b, g + gi, 0)),
            pl.BlockSpec((None, None, nsel, tq), lambda b, gi, i: (b, gi, 0, i)),
            pl.BlockSpec((None, gw, tq), lambda b, gi, i: (b, gi, i)),
            pl.BlockSpec((tq, LANES), lambda b, gi, i: (b * nq + i, 0)),
        ],
        out_specs=pl.BlockSpec((tq, gw), lambda b, gi, i: (b * nq + i, gi)),
        out_shape=jax.ShapeDtypeStruct((batch * seq, NSA_WIDTH), BF16),
        scratch_shapes=[
            pltpu.VMEM((2, 1, n), F32),
            pltpu.VMEM((2, 1, n), F32),
            pltpu.VMEM((2, HEAD_DIM, n), F32),
            pltpu.VMEM((LANES, tq), F32),
            pltpu.VMEM((tk, n), F32),
        ],
        compiler_params=_params(("parallel", "parallel", "parallel")),
        name="nsa_sel_win_attn",
    )(pa, *tables, kn, kn, vt, vt, sel, o_cmp_t, png)


HG_TILE = 128
HG_GROUP = 8
HG_LEVELS = (64, 32, 16, 8)
HG_HEADS_PER_STEP = 8


def _split3(x):
    hi = x.astype(BF16)
    r = x - hi.astype(F32)
    mid = r.astype(BF16)
    lo = (r - mid.astype(F32)).astype(BF16)
    return hi, mid, lo


def _group_row(x, size, row):
    t, d = x.shape
    x3 = x.reshape(t // size, size, d)
    return jnp.broadcast_to(x3[:, row:row + 1, :], x3.shape).reshape(t, d)


def _hgrn_heads(qs, zs, vs, gs, lbs, nw, low, sts):
    t = HG_TILE
    nh = len(qs)
    heads = range(nh)
    log2e = math.log2(math.e)
    kks, l2fs = [], []
    for i in heads:
        z, lb = zs[i], lbs[i]
        ez = jnp.exp(-jnp.abs(z))
        r = 1.0 / (1.0 + ez)
        pos = z >= 0.0
        f = lb + (1.0 - lb) * jnp.where(pos, r, ez * r)
        kks.append((1.0 - lb) * jnp.where(pos, ez * r, r))
        l2fs.append(jnp.where(f > 0.0, jnp.log2(f), jnp.minimum(z, 0.0) * log2e))
    bs_ = []
    for i in heads:
        hi, mid, lo = _split3(l2fs[i])
        bs_.append(jnp.dot(low, hi, preferred_element_type=F32) + jnp.dot(low, mid, preferred_element_type=F32)
                   + jnp.dot(low, lo, preferred_element_type=F32))
    ti = lax.broadcasted_iota(jnp.int32, (t, t), 0)
    si = lax.broadcasted_iota(jnp.int32, (t, t), 1)
    atts = [jnp.zeros((t, t), F32) for _ in heads]
    for h in HG_LEVELS:
        pair = ((ti // (2 * h)) == (si // (2 * h))) & ((ti % (2 * h)) >= h) & ((si % (2 * h)) < h)
        for i in heads:
            b = bs_[i]
            bref = _group_row(b, 2 * h, h - 1)
            e = jnp.exp2(-jnp.abs(b - bref))
            atts[i] = atts[i] + jnp.where(pair, _nt_dot((qs[i] * e).astype(BF16), (kks[i] * e).astype(BF16)), 0.0)
    lane_d = lax.broadcasted_iota(jnp.int32, (HGRN_DK, t), 1)
    diags = [jnp.zeros((t, t), F32) for _ in heads]
    for sg in range(HG_GROUP):
        spread = jnp.where((lane_d % HG_GROUP) == sg, 1.0, 0.0).astype(BF16)
        for i in heads:
            bsg = _group_row(bs_[i], HG_GROUP, sg)
            ksg = _group_row(kks[i], HG_GROUP, sg)
            p = (qs[i] * jnp.exp2(jnp.minimum(bs_[i] - bsg, 0.0))) * ksg
            diags[i] = diags[i] + jnp.dot(p.astype(BF16), spread, preferred_element_type=F32)
    same = ((ti // HG_GROUP) == (si // HG_GROUP)) & (si <= ti)
    ys, st_new = [], []
    for i in heads:
        b = bs_[i]
        att = atts[i] + jnp.where(same, diags[i], 0.0)
        vb = vs[i].astype(BF16)
        o = jnp.dot(att.astype(BF16), vb, preferred_element_type=F32)
        o = o + _nt_dot((qs[i] * jnp.exp2(b)).astype(BF16), sts[i].astype(BF16))
        b_last = b[t - 1:t, :]
        kdec = (kks[i] * jnp.exp2(b_last - b)).astype(BF16)
        st_new.append(sts[i] * jnp.exp2(b_last) + _tn_dot(vb, kdec))
        y = o * lax.rsqrt(jnp.mean(o * o, axis=-1, keepdims=True) + NORM_EPS) * nw
        g = gs[i]
        ys.append(y * (g * jax.nn.sigmoid(g)))
    return ys, st_new


def _hgrn_kernel(q_ref, z_ref, v_ref, g_ref, lb_ref, nw_ref, low_ref, o_ref, st_ref, *, nh):
    @pl.when(pl.program_id(2) == 0)
    def _():
        st_ref[...] = jnp.zeros_like(st_ref)

    cols = [slice(hd * HGRN_DK, (hd + 1) * HGRN_DK) for hd in range(nh)]
    ys, st_new = _hgrn_heads([q_ref[:, c] for c in cols], [z_ref[:, c] for c in cols], [v_ref[:, c] for c in cols],
                             [g_ref[:, c] for c in cols], [lb_ref[hd:hd + 1, :] for hd in range(nh)],
                             nw_ref[...], low_ref[...], [st_ref[hd] for hd in range(nh)])
    for hd in range(nh):
        st_ref[hd] = st_new[hd]
        o_ref[:, cols[hd]] = ys[hd].astype(o_ref.dtype)


def _hgrn(pb, lb, norm_w, batch, seq):
    hh = HGRN_HEADS
    nh = HG_HEADS_PER_STEP
    nt = seq // HG_TILE
    low = jnp.asarray(np.tril(np.ones((HG_TILE, HG_TILE), np.float32)), dtype=BF16)
    spec = lambda part: pl.BlockSpec((HG_TILE, nh * HGRN_DK), lambda b, h, i: (b * nt + i, part * (hh // nh) + h))
    return pl.pallas_call(
        functools.partial(_hgrn_kernel, nh=nh),
        grid=(batch, hh // nh, nt),
        in_specs=[
            spec(0), spec(1), spec(2), spec(3),
            pl.BlockSpec((None, nh, HGRN_DK), lambda b, h, i: (h, 0, 0)),
            pl.BlockSpec((1, HGRN_DV), lambda b, h, i: (0, 0)),
            pl.BlockSpec(low.shape, lambda b, h, i: (0, 0)),
        ],
        out_specs=pl.BlockSpec((HG_TILE, nh * HGRN_DV), lambda b, h, i: (b * nt + i, h)),
        out_shape=jax.ShapeDtypeStruct((batch * seq, HGRN_VW), BF16),
        scratch_shapes=[pltpu.VMEM((nh, HGRN_DV, HGRN_DK), F32)],
        compiler_params=_params(("parallel", "parallel", "arbitrary")),
        name="hgrn2",
    )(pb, pb, pb, pb, lb.reshape(hh // nh, nh, HGRN_DK), norm_w.reshape(1, HGRN_DV), low)


def _hybrid_mixer(h, tables, layer, w_in_t, cmp_pos, cmp_w1, cmp_w2, lb, g_norm_w, w_up_a, w_up_b, batch, seq):
    pa = _matmul_t(h, w_in_t, layer, 0, SEG_A, F32)
    png = _matmul_t(h, w_in_t, layer, SEG_A, LANES, F32, tn=LANES)
    pb = _matmul_t(h, w_in_t, layer, SEG_A + SEG_G, SEG_B, F32)
    kc, vct = _compress(pa, cmp_pos, cmp_w1, cmp_w2, batch, seq)
    o_cmp_t, sel = _cmp_attn(pa, kc, vct, batch, seq)
    kn, vt = _kv_prep(pa, tables, batch, seq)
    o_nsa = _sel_attn(pa, tables, kn, vt, sel, o_cmp_t, png, batch, seq)
    o_hgrn = _hgrn(pb, lb, g_norm_w, batch, seq)
    return _matmul_merge(o_nsa, o_hgrn, w_up_a, w_up_b, layer, pb, 2 * HGRN_KW + 2 * HGRN_VW,
                         2 * HGRN_KW + 2 * HGRN_VW + D_MODEL)


def kernel(x, c, positions, ada_w, ada_b, norm_mix_w, w_in, nsa_cmp_pos, nsa_cmp_w1, nsa_cmp_w2, hgrn_lb_logits,
           hgrn_norm_w, w_up_a, w_up_b, w_out, norm_mlp_w, w_mlp1, w_mlp2, final_norm_w):
    batch, seq, d = x.shape
    depth = ada_w.shape[0]
    lb_all = jnp.cumsum(jax.nn.softmax(hgrn_lb_logits.astype(F32), axis=0), axis=0)
    lb_all = lb_all - lb_all[0:1]
    mod = _ada_mod(c, ada_w, ada_b)
    tables = _rope_tables(positions)
    x2 = x.reshape(batch * seq, d)
    zeros = jnp.zeros((batch, d), F32)
    w_in_t = jnp.transpose(w_in, (0, 2, 1))
    for l in range(depth):
        sh1, sc1, g1, sh2, sc2, g2 = [mod[l][:, k * d:(k + 1) * d] for k in range(6)]
        h = _norm_mod(x2, norm_mix_w[l], sc1, sh1, seq, BF16)
        y = _hybrid_mixer(h, tables, l, w_in_t, nsa_cmp_pos[l], nsa_cmp_w1[l], nsa_cmp_w2[l], lb_all[l],
                          hgrn_norm_w[l], w_up_a, w_up_b, batch, seq)
        x2 = _matmul_resid(y, w_out, l, x2, g1, seq)
        h = _norm_mod(x2, norm_mlp_w[l], sc2, sh2, seq, BF16)
        u = _matmul(h, w_mlp1, l, BF16, relu2=True)
        x2 = _matmul_resid_ksplit(u, w_mlp2, l, x2, g2, seq)
    out = _norm_mod(x2, final_norm_w, zeros, zeros, seq, F32)
    return out.reshape(batch, seq, d)
```

```python
import functools
import math

import numpy as np
import jax
import jax.numpy as jnp
from jax import lax
from jax.experimental import pallas as pl
from jax.experimental.pallas import tpu as pltpu

D_MODEL = 4096
DEPTH = 2
NSA_HEADS = 16
NSA_KV_GROUPS = 4
NSA_HPG = NSA_HEADS // NSA_KV_GROUPS
HEAD_DIM = 128
CMP_LEN = 32
CMP_STRIDE = 16
SLC_BLOCK = 64
SLC_TOPK = 16
WINDOW = 512
ATTN_SCALE = HEAD_DIM ** -0.5
FORCED_BONUS = 1e6
NEG_INF = -1e30
HGRN_HEADS = 16
HGRN_DK = 128
HGRN_DV = 128
ROPE_THETA = 500000.0
ROT_DIM = HEAD_DIM // 4
D_FF = 4 * D_MODEL
NORM_EPS = 1e-6

NSA_WIDTH = NSA_HEADS * HEAD_DIM
KV_WIDTH = NSA_KV_GROUPS * HEAD_DIM
HGRN_KW = HGRN_HEADS * HGRN_DK
HGRN_VW = HGRN_HEADS * HGRN_DV
SEG_A = NSA_WIDTH + 6 * KV_WIDTH
SEG_G = 3 * NSA_HEADS
SEG_B = 2 * HGRN_KW + 2 * HGRN_VW + 2 * D_MODEL

LANES = 128
VMEM_LIMIT = 58 * 1024 * 1024

BF16 = jnp.bfloat16
F32 = jnp.float32


def _params(sem):
    return pltpu.CompilerParams(dimension_semantics=sem, vmem_limit_bytes=VMEM_LIMIT)


def _nt_dot(a, b, precision=None):
    return lax.dot_general(a, b, (((1,), (1,)), ((), ())), preferred_element_type=F32, precision=precision)


def _tn_dot(a, b, precision=None):
    return lax.dot_general(a, b, (((0,), (0,)), ((), ())), preferred_element_type=F32, precision=precision)


def _ada_kernel(cb_ref, w_ref, b_ref, o_ref, cact_ref, *, batch, tn):
    @pl.when((pl.program_id(0) == 0) & (pl.program_id(1) == 0))
    def _():
        cb = cb_ref[...]
        cact_ref[...] = cb * jax.nn.sigmoid(cb)

    for b in range(batch):
        cact = cact_ref[b]
        rows = []
        for s in range(tn // LANES):
            w = w_ref[:, s * LANES:(s + 1) * LANES]
            rows.append(jnp.sum(w * cact, axis=0, keepdims=True))
        o_ref[b:b + 1, :] = jnp.concatenate(rows, axis=1) + b_ref[...]


def _ada_mod(c, ada_w, ada_b):
    depth, k, n = ada_w.shape
    batch = c.shape[0]
    tn = 512
    cb = jnp.broadcast_to(c[:, :, None], (batch, k, LANES))
    out = pl.pallas_call(
        functools.partial(_ada_kernel, batch=batch, tn=tn),
        grid=(depth, n // tn),
        in_specs=[
            pl.BlockSpec((batch, k, LANES), lambda l, j: (0, 0, 0)),
            pl.BlockSpec((None, k, tn), lambda l, j: (l, 0, j)),
            pl.BlockSpec((None, 1, tn), lambda l, j: (l, 0, j)),
        ],
        out_specs=pl.BlockSpec((None, batch, tn), lambda l, j: (l, 0, j)),
        out_shape=jax.ShapeDtypeStruct((depth, batch, n), F32),
        scratch_shapes=[pltpu.VMEM((batch, k, LANES), F32)],
        compiler_params=_params(("arbitrary", "arbitrary")),
        name="ada_mod",
    )(cb, ada_w, ada_b.reshape(depth, 1, n))
    return out


def _norm_kernel(x_ref, w_ref, sc_ref, sh_ref, o_ref):
    x = x_ref[...]
    y = x * lax.rsqrt(jnp.mean(x * x, axis=-1, keepdims=True) + NORM_EPS)
    y = y * w_ref[...]
    y = y * (1.0 + sc_ref[...]) + sh_ref[...]
    o_ref[...] = y.astype(o_ref.dtype)


def _norm_mod(x2, w, sc, sh, seq, out_dtype):
    m, d = x2.shape
    batch = m // seq
    tm = 256
    per = seq // tm
    return pl.pallas_call(
        _norm_kernel,
        grid=(m // tm,),
        in_specs=[
            pl.BlockSpec((tm, d), lambda i: (i, 0)),
            pl.BlockSpec((1, d), lambda i: (0, 0)),
            pl.BlockSpec((None, 1, d), lambda i: (i // per, 0, 0)),
            pl.BlockSpec((None, 1, d), lambda i: (i // per, 0, 0)),
        ],
        out_specs=pl.BlockSpec((tm, d), lambda i: (i, 0)),
        out_shape=jax.ShapeDtypeStruct((m, d), out_dtype),
        compiler_params=_params(("parallel",)),
        name="norm_mod",
    )(x2, w.reshape(1, d), sc.reshape(batch, 1, d), sh.reshape(batch, 1, d))


MM_TM = 1024
MM_TN = 512
MM_TK = 4096
MM_TM_WIDE = 512
MM_TN_WIDE = 1024


def _mm_kernel(a_ref, w_ref, o_ref, wb_ref, *, relu2):
    @pl.when(pl.program_id(1) == 0)
    def _():
        wb_ref[...] = w_ref[...].astype(BF16)

    r = jnp.dot(a_ref[...], wb_ref[...], preferred_element_type=F32)
    if relu2:
        r = jnp.square(jnp.maximum(r, 0.0))
    o_ref[...] = r.astype(o_ref.dtype)


def _matmul(a, w, layer, out_dtype, relu2=False, tm=MM_TM_WIDE, tn=MM_TN_WIDE):
    m, kd = a.shape
    n = w.shape[2]
    tm = min(tm, m)
    assert n % tn == 0 and m % tm == 0
    return pl.pallas_call(
        functools.partial(_mm_kernel, relu2=relu2),
        grid=(n // tn, m // tm),
        in_specs=[
            pl.BlockSpec((tm, kd), lambda j, i: (i, 0)),
            pl.BlockSpec((None, kd, tn), lambda j, i: (layer, 0, j)),
        ],
        out_specs=pl.BlockSpec((tm, tn), lambda j, i: (i, j)),
        out_shape=jax.ShapeDtypeStruct((m, n), out_dtype),
        scratch_shapes=[pltpu.VMEM((kd, tn), BF16)],
        compiler_params=_params(("parallel", "arbitrary")),
        name="matmul",
    )(a, w)


def _mm_t_kernel(*refs, shift):
    if shift:
        a_ref, w_ref, wn_ref, o_ref, wb_ref = refs
    else:
        a_ref, w_ref, o_ref, wb_ref = refs

    @pl.when(pl.program_id(1) == 0)
    def _():
        tn = w_ref.shape[0]
        wb_ref[0:tn - shift, :] = w_ref[shift:tn, :].astype(BF16)
        if shift:
            wb_ref[tn - shift:tn, :] = wn_ref[0:shift, :].astype(BF16)

    o_ref[...] = _nt_dot(a_ref[...], wb_ref[...]).astype(o_ref.dtype)


BF16_SUBLANES = 16


def _matmul_t(a, wt, layer, row0, nrows, out_dtype, tm=MM_TM_WIDE, tn=MM_TN_WIDE):
    m, kd = a.shape
    tm = min(tm, m)
    shift = row0 % tn
    base = row0 - shift
    nxt = 64
    assert nrows % tn == 0 and m % tm == 0 and shift % BF16_SUBLANES == 0 and shift <= nxt and tn % nxt == 0
    in_specs = [
        pl.BlockSpec((tm, kd), lambda j, i: (i, 0)),
        pl.BlockSpec((None, tn, kd), lambda j, i: (layer, base // tn + j, 0)),
    ]
    args = [a, wt]
    if shift:
        in_specs.append(pl.BlockSpec((None, nxt, kd), lambda j, i: (layer, (base + tn * (j + 1)) // nxt, 0)))
        args.append(wt)
    return pl.pallas_call(
        functools.partial(_mm_t_kernel, shift=shift),
        grid=(nrows // tn, m // tm),
        in_specs=in_specs,
        out_specs=pl.BlockSpec((tm, tn), lambda j, i: (i, j)),
        out_shape=jax.ShapeDtypeStruct((m, nrows), out_dtype),
        scratch_shapes=[pltpu.VMEM((tn, kd), BF16)],
        compiler_params=_params(("parallel", "arbitrary")),
        name="matmul_t",
    )(*args)


def _mm_resid_kernel(a_ref, w_ref, x_ref, g_ref, o_ref, wb_ref):
    @pl.when(pl.program_id(1) == 0)
    def _():
        wb_ref[...] = w_ref[...].astype(BF16)

    o_ref[...] = x_ref[...] + g_ref[...] * jnp.dot(a_ref[...], wb_ref[...], preferred_element_type=F32)


def _matmul_resid(a, w, layer, x2, gate, seq, tn=MM_TN, tk=None, ks=0):
    m = a.shape[0]
    kd = tk if tk is not None else a.shape[1]
    n = w.shape[2]
    tm = min(MM_TM, seq)
    per = seq // tm
    batch = m // seq
    return pl.pallas_call(
        _mm_resid_kernel,
        grid=(n // tn, m // tm),
        in_specs=[
            pl.BlockSpec((tm, kd), lambda j, i: (i, ks)),
            pl.BlockSpec((None, kd, tn), lambda j, i: (layer, ks, j)),
            pl.BlockSpec((tm, tn), lambda j, i: (i, j)),
            pl.BlockSpec((None, 1, tn), lambda j, i: (i // per, 0, j)),
        ],
        out_specs=pl.BlockSpec((tm, tn), lambda j, i: (i, j)),
        out_shape=jax.ShapeDtypeStruct((m, n), F32),
        scratch_shapes=[pltpu.VMEM((kd, tn), BF16)],
        compiler_params=_params(("parallel", "arbitrary")),
        name="matmul_resid",
    )(a, w, x2, gate.reshape(batch, 1, n))


def _matmul_resid_ksplit(a, w, layer, x2, gate, seq, tk=MM_TK):
    kd = a.shape[1]
    for ks in range(kd // tk):
        x2 = _matmul_resid(a, w, layer, x2, gate, seq, tk=tk, ks=ks)
    return x2


def _mm_merge_kernel(a_ref, b_ref, wa_ref, wb_ref, ga_ref, gb_ref, o_ref, wab_ref, wbb_ref):
    @pl.when(pl.program_id(1) == 0)
    def _():
        wab_ref[...] = wa_ref[...].astype(BF16)
        wbb_ref[...] = wb_ref[...].astype(BF16)

    ya = jnp.dot(a_ref[...], wab_ref[...], preferred_element_type=F32)
    yb = jnp.dot(b_ref[...], wbb_ref[...], preferred_element_type=F32)
    o_ref[...] = (jax.nn.sigmoid(ga_ref[...]) * ya + jax.nn.sigmoid(gb_ref[...]) * yb).astype(o_ref.dtype)


def _matmul_merge(a, b, wa, wb, layer, pb, ga_col, gb_col, tn=MM_TN):
    m, kd = a.shape
    n = wa.shape[2]
    tm = min(MM_TM, m)
    return pl.pallas_call(
        _mm_merge_kernel,
        grid=(n // tn, m // tm),
        in_specs=[
            pl.BlockSpec((tm, kd), lambda j, i: (i, 0)),
            pl.BlockSpec((tm, kd), lambda j, i: (i, 0)),
            pl.BlockSpec((None, kd, tn), lambda j, i: (layer, 0, j)),
            pl.BlockSpec((None, kd, tn), lambda j, i: (layer, 0, j)),
            pl.BlockSpec((tm, tn), lambda j, i: (i, ga_col // tn + j)),
            pl.BlockSpec((tm, tn), lambda j, i: (i, gb_col // tn + j)),
        ],
        out_specs=pl.BlockSpec((tm, tn), lambda j, i: (i, j)),
        out_shape=jax.ShapeDtypeStruct((m, n), BF16),
        scratch_shapes=[pltpu.VMEM((kd, tn), BF16), pltpu.VMEM((kd, tn), BF16)],
        compiler_params=_params(("parallel", "arbitrary")),
        name="matmul_merge",
    )(a, b, wa, wb, pb, pb)


def _gelu_tanh(x):
    c = math.sqrt(2.0 / math.pi)
    return 0.5 * x * (1.0 + jnp.tanh(c * (x + 0.044715 * (x * x * x))))


def _compress_one(x_ref, pos_ref, w1_ref, w2_ref, kv, nhalf):
    hp = lax.Precision.HIGHEST
    half = CMP_LEN // 2
    acc_a = jnp.zeros((nhalf, HEAD_DIM), F32)
    acc_b = jnp.zeros((nhalf, HEAD_DIM), F32)
    for l in range(half):
        xl = x_ref[pl.ds(l, nhalf, stride=CMP_STRIDE), :]
        acc_a = acc_a + jnp.dot(xl + pos_ref[kv, l:l + 1, :], w1_ref[kv, l], precision=hp, preferred_element_type=F32)
        acc_b = acc_b + jnp.dot(xl + pos_ref[kv, half + l:half + l + 1, :], w1_ref[kv, half + l], precision=hp,
                                preferred_element_type=F32)
    hid = acc_a + pltpu.roll(acc_b, nhalf - 1, 0)
    hid = _gelu_tanh(hid)
    return jnp.dot(hid, w2_ref[kv], precision=hp, preferred_element_type=F32)


def _compress_kernel(xk_ref, xv_ref, pos_ref, w1_ref, w2_ref, kc_ref, vct_ref, *, nhalf):
    kc_ref[...] = _compress_one(xk_ref, pos_ref, w1_ref, w2_ref, 0, nhalf)
    vct_ref[...] = _compress_one(xv_ref, pos_ref, w1_ref, w2_ref, 1, nhalf).T


def _compress(pa, pos, w1, w2, batch, seq):
    nhalf = seq // CMP_STRIDE
    g = NSA_KV_GROUPS
    col0 = NSA_WIDTH // HEAD_DIM
    full = lambda shape: pl.BlockSpec(shape, lambda b, gi: (0,) * len(shape))
    return pl.pallas_call(
        functools.partial(_compress_kernel, nhalf=nhalf),
        grid=(batch, g),
        in_specs=[
            pl.BlockSpec((seq, HEAD_DIM), lambda b, gi: (b, col0 + gi)),
            pl.BlockSpec((seq, HEAD_DIM), lambda b, gi: (b, col0 + g + gi)),
            full(pos.shape), full(w1.shape), full(w2.shape),
        ],
        out_specs=[
            pl.BlockSpec((None, None, nhalf, HEAD_DIM), lambda b, gi: (b, gi, 0, 0)),
            pl.BlockSpec((None, None, HEAD_DIM, nhalf), lambda b, gi: (b, gi, 0, 0)),
        ],
        out_shape=[
            jax.ShapeDtypeStruct((batch, g, nhalf, HEAD_DIM), F32),
            jax.ShapeDtypeStruct((batch, g, HEAD_DIM, nhalf), F32),
        ],
        compiler_params=_params(("parallel", "parallel")),
        name="nsa_compress",
    )(pa, pa, pos, w1, w2)


def _cmp_slc_overlap_t(seq, ncp):
    nc = (seq - CMP_LEN) // CMP_STRIDE + 1
    nsel = seq // SLC_BLOCK
    cs = np.arange(nc) * CMP_STRIDE
    ce = cs + CMP_LEN - 1
    ss = np.arange(nsel) * SLC_BLOCK
    se = ss + SLC_BLOCK - 1
    ov = np.minimum(ce[:, None], se[None, :]) - np.maximum(cs[:, None], ss[None, :]) + 1
    ov = np.maximum(ov, 0).astype(np.float32)
    out = np.zeros((nsel, ncp), np.float32)
    out[:, :nc] = ov.T
    return out


def _heads_t(x_ref, fn=None):
    parts = []
    for j in range(NSA_HPG):
        x = x_ref[:, j * HEAD_DIM:(j + 1) * HEAD_DIM]
        if fn is not None:
            x = fn(x)
        parts.append(x.T.astype(BF16))
    return jnp.concatenate(parts, axis=1)


def _cmp_attn_kernel(q_ref, kc_ref, vct_ref, ovt_ref, o_ref, sel_ref, *, tq, ncp, nsel):
    i = pl.program_id(2)
    t0 = i * tq
    tpos = t0 + lax.broadcasted_iota(jnp.int32, (ncp, tq), 1)
    nidx = lax.broadcasted_iota(jnp.int32, (ncp, tq), 0)
    vis = (nidx * CMP_STRIDE + (CMP_LEN - 1)) <= tpos
    qt = _heads_t(q_ref)
    st = jnp.dot(kc_ref[...].astype(BF16), qt, preferred_element_type=F32) * ATTN_SCALE
    vct = vct_ref[...].astype(BF16)
    imp = jnp.zeros((ncp, tq), F32)
    for j in range(NSA_HPG):
        s = jnp.where(vis, st[:, j * tq:(j + 1) * tq], NEG_INF)
        e = jnp.exp(s - jnp.max(s, axis=0, keepdims=True))
        p = e / jnp.sum(e, axis=0, keepdims=True)
        p = jnp.where(vis, p, 0.0)
        o_ref[j * HEAD_DIM:(j + 1) * HEAD_DIM, :] = jnp.dot(vct, p.astype(BF16), preferred_element_type=F32)
        imp = imp + p
    pslc = jnp.dot(ovt_ref[...], imp, precision=lax.Precision.HIGHEST, preferred_element_type=F32)
    tl = t0 + lax.broadcasted_iota(jnp.int32, (nsel, tq), 1)
    mi = lax.broadcasted_iota(jnp.int32, (nsel, tq), 0)
    cur = tl // SLC_BLOCK
    valid = mi * SLC_BLOCK <= tl
    forced = (mi == 0) | (mi == cur) | (mi == cur - 1)
    score = jnp.where(valid, pslc + jnp.where(forced, FORCED_BONUS, 0.0), -jnp.inf)
    topk = min(SLC_TOPK, nsel)
    few = (t0 + tq) <= topk * SLC_BLOCK

    @pl.when(few)
    def _():
        sel_ref[...] = jnp.where(valid, 1.0, 0.0)

    @pl.when(jnp.logical_not(few))
    def _():
        rank = jnp.zeros((nsel, tq), F32)
        for mp in range(nsel):
            row = score[mp:mp + 1, :]
            before = (row > score) | ((row == score) & (mi > mp))
            rank = rank + jnp.where(before, 1.0, 0.0)
        sel_ref[...] = jnp.where(valid & (rank < float(topk)), 1.0, 0.0)


def _cmp_attn(pa, kc, vct, batch, seq, tq=256):
    ncp = seq // CMP_STRIDE
    nsel = seq // SLC_BLOCK
    g = NSA_KV_GROUPS
    nq = seq // tq
    gw = NSA_HPG * HEAD_DIM
    ovt = jnp.asarray(_cmp_slc_overlap_t(seq, ncp))
    return pl.pallas_call(
        functools.partial(_cmp_attn_kernel, tq=tq, ncp=ncp, nsel=nsel),
        grid=(batch, g, nq),
        in_specs=[
            pl.BlockSpec((tq, gw), lambda b, gi, i: (b * nq + i, gi)),
            pl.BlockSpec((None, None, ncp, HEAD_DIM), lambda b, gi, i: (b, gi, 0, 0)),
            pl.BlockSpec((None, None, HEAD_DIM, ncp), lambda b, gi, i: (b, gi, 0, 0)),
            pl.BlockSpec((nsel, ncp), lambda b, gi, i: (0, 0)),
        ],
        out_specs=[
            pl.BlockSpec((None, gw, tq), lambda b, gi, i: (b, gi, i)),
            pl.BlockSpec((None, None, nsel, tq), lambda b, gi, i: (b, gi, 0, i)),
        ],
        out_shape=[
            jax.ShapeDtypeStruct((batch, NSA_WIDTH, seq), F32),
            jax.ShapeDtypeStruct((batch, g, nsel, seq), F32),
        ],
        compiler_params=_params(("parallel", "parallel", "parallel")),
        name="nsa_cmp_attn",
    )(pa, kc, vct, ovt)


def _rope_tables(positions):
    half = ROT_DIM // 2
    inv = ROPE_THETA ** (-jnp.arange(0, ROT_DIM, 2, dtype=F32) / ROT_DIM)
    ang = positions.astype(F32)[..., None] * inv
    cos, sin = jnp.cos(ang), jnp.sin(ang)
    rest = HEAD_DIM - ROT_DIM
    b, s = positions.shape
    one = jnp.ones((b, s, rest), F32)
    zero = jnp.zeros((b, s, rest), F32)
    zh = jnp.zeros((b, s, half), F32)
    c = jnp.concatenate([cos, cos, one], axis=-1)
    s1 = jnp.concatenate([-sin, zh, zero], axis=-1)
    s2 = jnp.concatenate([zh, sin, zero], axis=-1)
    return [t.reshape(b * s, HEAD_DIM) for t in (c, s1, s2)]


def _rot(x, c, s1, s2):
    half = ROT_DIM // 2
    return x * c + pltpu.roll(x, HEAD_DIM - half, 1) * s1 + pltpu.roll(x, half, 1) * s2


def _kv_prep_kernel(xs_ref, xw_ref, c_ref, s1_ref, s2_ref, k_ref, vt_ref):
    c, s1, s2 = c_ref[...], s1_ref[...], s2_ref[...]
    for br, x_ref in enumerate((xs_ref, xw_ref)):
        for gi in range(NSA_KV_GROUPS):
            src = gi * HEAD_DIM
            dst = br * KV_WIDTH + gi * HEAD_DIM
            k_ref[:, dst:dst + HEAD_DIM] = _rot(x_ref[:, src:src + HEAD_DIM], c, s1, s2).astype(k_ref.dtype)
            vt_ref[dst:dst + HEAD_DIM, :] = x_ref[:, KV_WIDTH + src:KV_WIDTH + src + HEAD_DIM].T.astype(vt_ref.dtype)


def _kv_prep(pa, tables, batch, seq, tm=256):
    nt = seq // tm
    w = 2 * KV_WIDTH
    col = (NSA_WIDTH + 2 * KV_WIDTH) // w
    assert col * w == NSA_WIDTH + 2 * KV_WIDTH
    return pl.pallas_call(
        _kv_prep_kernel,
        grid=(batch, nt),
        in_specs=[pl.BlockSpec((tm, w), lambda b, i: (b * nt + i, col)),
                  pl.BlockSpec((tm, w), lambda b, i: (b * nt + i, col + 1))]
        + [pl.BlockSpec((tm, HEAD_DIM), lambda b, i: (b * nt + i, 0))] * 3,
        out_specs=[
            pl.BlockSpec((tm, 2 * KV_WIDTH), lambda b, i: (b * nt + i, 0)),
            pl.BlockSpec((None, 2 * KV_WIDTH, tm), lambda b, i: (b, 0, i)),
        ],
        out_shape=[
            jax.ShapeDtypeStruct((batch * seq, 2 * KV_WIDTH), BF16),
            jax.ShapeDtypeStruct((batch, 2 * KV_WIDTH, seq), BF16),
        ],
        compiler_params=_params(("parallel", "parallel")),
        name="nsa_kv_prep",
    )(pa, pa, *tables)


def _scores_t(k_ref, qt, kt, tk):
    k0 = pl.multiple_of(kt * tk, tk)
    return jnp.dot(k_ref[pl.ds(k0, tk), :], qt, preferred_element_type=F32)


def _flash_init(m_ref, l_ref, acc_ref):
    m_ref[...] = jnp.full(m_ref.shape, NEG_INF, F32)
    l_ref[...] = jnp.zeros(l_ref.shape, F32)
    acc_ref[...] = jnp.zeros(acc_ref.shape, F32)


def _flash_consume_t(s_ref, vt_ref, kt, mask, m_ref, l_ref, acc_ref, tk):
    k0 = pl.multiple_of(kt * tk, tk)
    vt = vt_ref[:, pl.ds(k0, tk)]
    mask = jnp.concatenate([mask] * NSA_HPG, axis=1)
    s = jnp.where(mask, s_ref[...], NEG_INF)
    m_old = m_ref[...]
    m_new = jnp.maximum(m_old, jnp.max(s, axis=0, keepdims=True))
    m_use = jnp.where(m_new > 0.5 * NEG_INF, m_new, 0.0)
    p = jnp.exp2(s - m_use)
    alpha = jnp.exp2(m_old - m_new)
    l_ref[...] = alpha * l_ref[...] + jnp.sum(p, axis=0, keepdims=True)
    acc_ref[...] = alpha * acc_ref[...] + jnp.dot(vt, p.astype(BF16), preferred_element_type=F32)
    m_ref[...] = m_new


def _sel_attn_kernel(q_ref, c_ref, s1_ref, s2_ref, ks_ref, kw_ref, vst_ref, vwt_ref, sel_ref, oct_ref, ng_ref, o_ref,
                     m_ref, l_ref, acc_ref, gt_ref, s_ref, *, tq, tk):
    gi = pl.program_id(1)
    i = pl.program_id(2)
    t0 = i * tq
    nj = NSA_HPG
    c, s1, s2 = c_ref[...], s1_ref[...], s2_ref[...]
    scale2 = ATTN_SCALE * math.log2(math.e)
    qt = _heads_t(q_ref, lambda x: _rot(x, c, s1, s2) * scale2)
    kpos = lax.broadcasted_iota(jnp.int32, (tk, tq), 0)
    tpos = t0 + lax.broadcasted_iota(jnp.int32, (tk, tq), 1)
    nb = tk // SLC_BLOCK

    def sel_mask(kt, diag):
        rows = sel_ref[pl.ds(pl.multiple_of(kt * nb, nb), nb), :]
        chosen = jnp.broadcast_to(rows[:, None, :], (nb, SLC_BLOCK, tq)).reshape(tk, tq) > 0.5
        return chosen & ((kt * tk + kpos) <= tpos) if diag else chosen

    def win_mask(kt):
        kp = kt * tk + kpos
        return (kp <= tpos) & (kp > tpos - WINDOW)

    sel_st = (m_ref.at[0], l_ref.at[0], acc_ref.at[0])
    win_st = (m_ref.at[1], l_ref.at[1], acc_ref.at[1])
    _flash_init(*sel_st)
    _flash_init(*win_st)
    last = (t0 + tq - 1) // tk + 1
    first = jnp.maximum(t0 - (WINDOW - 1), 0) // tk

    s_ref[...] = _scores_t(ks_ref, qt, 0, tk)

    def sel_trip(kt, carry):
        s_next = _scores_t(ks_ref, qt, kt + 1, tk)
        _flash_consume_t(s_ref, vst_ref, kt, sel_mask(kt, False), *sel_st, tk)
        s_ref[...] = s_next
        return carry

    lax.fori_loop(0, last - 1, sel_trip, 0)
    s_next = _scores_t(kw_ref, qt, first, tk)
    _flash_consume_t(s_ref, vst_ref, last - 1, sel_mask(last - 1, True), *sel_st, tk)
    s_ref[...] = s_next

    def win_trip(kt, carry):
        s_next = _scores_t(kw_ref, qt, kt + 1, tk)
        _flash_consume_t(s_ref, vwt_ref, kt, win_mask(kt), *win_st, tk)
        s_ref[...] = s_next
        return carry

    lax.fori_loop(first, last - 1, win_trip, 0)
    _flash_consume_t(s_ref, vwt_ref, last - 1, win_mask(last - 1), *win_st, tk)
    o_slc = acc_ref[0] / l_ref[0]
    o_win = acc_ref[1] / l_ref[1]
    gt_ref[...] = jax.nn.sigmoid(ng_ref[...].T)
    for j in range(nj):
        gate = [gt_ref[pl.ds(br * NSA_HEADS + gi * nj + j, 1), :] for br in range(3)]
        ot = (gate[0] * oct_ref[j * HEAD_DIM:(j + 1) * HEAD_DIM, :] + gate[1] * o_slc[:, j * tq:(j + 1) * tq]
              + gate[2] * o_win[:, j * tq:(j + 1) * tq])
        o_ref[:, j * HEAD_DIM:(j + 1) * HEAD_DIM] = ot.T.astype(o_ref.dtype)


def _sel_attn(pa, tables, kn, vt, sel, o_cmp_t, png, batch, seq, tq=256, tk=512):
    g = NSA_KV_GROUPS
    nq = seq // tq
    gw = NSA_HPG * HEAD_DIM
    tk = min(tk, seq)
    assert tk % tq == 0 and seq % tk == 0
    nsel = seq // SLC_BLOCK
    n = NSA_HPG * tq
    tab = pl.BlockSpec((tq, HEAD_DIM), lambda b, gi, i: (b * nq + i, 0))
    return pl.pallas_call(
        functools.partial(_sel_attn_kernel, tq=tq, tk=tk),
        grid=(batch, g, nq),
        in_specs=[
            pl.BlockSpec((tq, gw), lambda b, gi, i: (b * nq + i, gi)),
            tab, tab, tab,
            pl.BlockSpec((seq, HEAD_DIM), lambda b, gi, i: (b, gi)),
            pl.BlockSpec((seq, HEAD_DIM), lambda b, gi, i: (b, g + gi)),
            pl.BlockSpec((None, HEAD_DIM, seq), lambda b, gi, i: (b, gi, 0)),
            pl.BlockSpec((None, HEAD_DIM, seq), lambda b, gi, i: (b, g + gi, 0)),
            pl.BlockSpec((None, None, nsel, tq), lambda b, gi, i: (b, gi, 0, i)),
            pl.BlockSpec((None, gw, tq), lambda b, gi, i: (b, gi, i)),
            pl.BlockSpec((tq, LANES), lambda b, gi, i: (b * nq + i, 0)),
        ],
        out_specs=pl.BlockSpec((tq, gw), lambda b, gi, i: (b * nq + i, gi)),
        out_shape=jax.ShapeDtypeStruct((batch * seq, NSA_WIDTH), BF16),
        scratch_shapes=[
            pltpu.VMEM((2, 1, n), F32),
            pltpu.VMEM((2, 1, n), F32),
            pltpu.VMEM((2, HEAD_DIM, n), F32),
            pltpu.VMEM((LANES, tq), F32),
            pltpu.VMEM((tk, n), F32),
        ],
        compiler_params=_params(("parallel", "parallel", "parallel")),
        name="nsa_sel_win_attn",
    )(pa, *tables, kn, kn, vt, vt, sel, o_cmp_t, png)


HG_TILE = 128
HG_GROUP = 8
HG_LEVELS = (64, 32, 16, 8)
HG_HEADS_PER_STEP = 8


def _split3(x):
    hi = x.astype(BF16)
    r = x - hi.astype(F32)
    mid = r.astype(BF16)
    lo = (r - mid.astype(F32)).astype(BF16)
    return hi, mid, lo


def _group_row(x, size, row):
    t, d = x.shape
    x3 = x.reshape(t // size, size, d)
    return jnp.broadcast_to(x3[:, row:row + 1, :], x3.shape).reshape(t, d)


def _hgrn_heads(qs, zs, vs, gs, lbs, nw, low, sts):
    t = HG_TILE
    nh = len(qs)
    heads = range(nh)
    log2e = math.log2(math.e)
    kks, l2fs = [], []
    for i in heads:
        z, lb = zs[i], lbs[i]
        ez = jnp.exp(-jnp.abs(z))
        r = 1.0 / (1.0 + ez)
        pos = z >= 0.0
        f = lb + (1.0 - lb) * jnp.where(pos, r, ez * r)
        kks.append((1.0 - lb) * jnp.where(pos, ez * r, r))
        l2fs.append(jnp.where(f > 0.0, jnp.log2(f), jnp.minimum(z, 0.0) * log2e))
    bs_ = []
    for i in heads:
        hi, mid, lo = _split3(l2fs[i])
        bs_.append(jnp.dot(low, hi, preferred_element_type=F32) + jnp.dot(low, mid, preferred_element_type=F32)
                   + jnp.dot(low, lo, preferred_element_type=F32))
    ti = lax.broadcasted_iota(jnp.int32, (t, t), 0)
    si = lax.broadcasted_iota(jnp.int32, (t, t), 1)
    atts = [jnp.zeros((t, t), F32) for _ in heads]
    for h in HG_LEVELS:
        pair = ((ti // (2 * h)) == (si // (2 * h))) & ((ti % (2 * h)) >= h) & ((si % (2 * h)) < h)
        for i in heads:
            b = bs_[i]
            bref = _group_row(b, 2 * h, h - 1)
            e = jnp.exp2(-jnp.abs(b - bref))
            atts[i] = atts[i] + jnp.where(pair, _nt_dot((qs[i] * e).astype(BF16), (kks[i] * e).astype(BF16)), 0.0)
    lane_d = lax.broadcasted_iota(jnp.int32, (HGRN_DK, t), 1)
    diags = [jnp.zeros((t, t), F32) for _ in heads]
    for sg in range(HG_GROUP):
        spread = jnp.where((lane_d % HG_GROUP) == sg, 1.0, 0.0).astype(BF16)
        for i in heads:
            bsg = _group_row(bs_[i], HG_GROUP, sg)
            ksg = _group_row(kks[i], HG_GROUP, sg)
            p = (qs[i] * jnp.exp2(jnp.minimum(bs_[i] - bsg, 0.0))) * ksg
            diags[i] = diags[i] + jnp.dot(p.astype(BF16), spread, preferred_element_type=F32)
    same = ((ti // HG_GROUP) == (si // HG_GROUP)) & (si <= ti)
    ys, st_new = [], []
    for i in heads:
        b = bs_[i]
        att = atts[i] + jnp.where(same, diags[i], 0.0)
        vb = vs[i].astype(BF16)
        o = jnp.dot(att.astype(BF16), vb, preferred_element_type=F32)
        o = o + _nt_dot((qs[i] * jnp.exp2(b)).astype(BF16), sts[i].astype(BF16))
        b_last = b[t - 1:t, :]
        kdec = (kks[i] * jnp.exp2(b_last - b)).astype(BF16)
        st_new.append(sts[i] * jnp.exp2(b_last) + _tn_dot(vb, kdec))
        y = o * lax.rsqrt(jnp.mean(o * o, axis=-1, keepdims=True) + NORM_EPS) * nw
        g = gs[i]
        ys.append(y * (g * jax.nn.sigmoid(g)))
    return ys, st_new


def _hgrn_kernel(q_ref, z_ref, v_ref, g_ref, lb_ref, nw_ref, low_ref, o_ref, st_ref, *, nh):
    @pl.when(pl.program_id(2) == 0)
    def _():
        st_ref[...] = jnp.zeros_like(st_ref)

    cols = [slice(hd * HGRN_DK, (hd + 1) * HGRN_DK) for hd in range(nh)]
    ys, st_new = _hgrn_heads([q_ref[:, c] for c in cols], [z_ref[:, c] for c in cols], [v_ref[:, c] for c in cols],
                             [g_ref[:, c] for c in cols], [lb_ref[hd:hd + 1, :] for hd in range(nh)],
                             nw_ref[...], low_ref[...], [st_ref[hd] for hd in range(nh)])
    for hd in range(nh):
        st_ref[hd] = st_new[hd]
        o_ref[:, cols[hd]] = ys[hd].astype(o_ref.dtype)


def _hgrn(pb, lb, norm_w, batch, seq):
    hh = HGRN_HEADS
    nh = HG_HEADS_PER_STEP
    nt = seq // HG_TILE
    low = jnp.asarray(np.tril(np.ones((HG_TILE, HG_TILE), np.float32)), dtype=BF16)
    spec = lambda part: pl.BlockSpec((HG_TILE, nh * HGRN_DK), lambda b, h, i: (b * nt + i, part * (hh // nh) + h))
    return pl.pallas_call(
        functools.partial(_hgrn_kernel, nh=nh),
        grid=(batch, hh // nh, nt),
        in_specs=[
            spec(0), spec(1), spec(2), spec(3),
            pl.BlockSpec((None, nh, HGRN_DK), lambda b, h, i: (h, 0, 0)),
            pl.BlockSpec((1, HGRN_DV), lambda b, h, i: (0, 0)),
            pl.BlockSpec(low.shape, lambda b, h, i: (0, 0)),
        ],
        out_specs=pl.BlockSpec((HG_TILE, nh * HGRN_DV), lambda b, h, i: (b * nt + i, h)),
        out_shape=jax.ShapeDtypeStruct((batch * seq, HGRN_VW), BF16),
        scratch_shapes=[pltpu.VMEM((nh, HGRN_DV, HGRN_DK), F32)],
        compiler_params=_params(("parallel", "parallel", "arbitrary")),
        name="hgrn2",
    )(pb, pb, pb, pb, lb.reshape(hh // nh, nh, HGRN_DK), norm_w.reshape(1, HGRN_DV), low)


def _hybrid_mixer(h, tables, layer, w_in_t, cmp_pos, cmp_w1, cmp_w2, lb, g_norm_w, w_up_a, w_up_b, batch, seq):
    pa = _matmul_t(h, w_in_t, layer, 0, SEG_A, F32)
    png = _matmul_t(h, w_in_t, layer, SEG_A, LANES, F32, tn=LANES)
    pb = _matmul_t(h, w_in_t, layer, SEG_A + SEG_G, SEG_B, F32)
    kc, vct = _compress(pa, cmp_pos, cmp_w1, cmp_w2, batch, seq)
    o_cmp_t, sel = _cmp_attn(pa, kc, vct, batch, seq)
    kn, vt = _kv_prep(pa, tables, batch, seq)
    o_nsa = _sel_attn(pa, tables, kn, vt, sel, o_cmp_t, png, batch, seq)
    o_hgrn = _hgrn(pb, lb, g_norm_w, batch, seq)
    return _matmul_merge(o_nsa, o_hgrn, w_up_a, w_up_b, layer, pb, 2 * HGRN_KW + 2 * HGRN_VW,
                         2 * HGRN_KW + 2 * HGRN_VW + D_MODEL)


def kernel(x, c, positions, ada_w, ada_b, norm_mix_w, w_in, nsa_cmp_pos, nsa_cmp_w1, nsa_cmp_w2, hgrn_lb_logits,
           hgrn_norm_w, w_up_a, w_up_b, w_out, norm_mlp_w, w_mlp1, w_mlp2, final_norm_w):
    batch, seq, d = x.shape
    depth = ada_w.shape[0]
    lb_all = jnp.cumsum(jax.nn.softmax(hgrn_lb_logits.astype(F32), axis=0), axis=0)
    lb_all = lb_all - lb_all[0:1]
    mod = _ada_mod(c, ada_w, ada_b)
    tables = _rope_tables(positions)
    x2 = x.reshape(batch * seq, d)
    zeros = jnp.zeros((batch, d), F32)
    w_in_t = jnp.transpose(w_in, (0, 2, 1))
    for l in range(depth):
        sh1, sc1, g1, sh2, sc2, g2 = [mod[l][:, k * d:(k + 1) * d] for k in range(6)]
        h = _norm_mod(x2, norm_mix_w[l], sc1, sh1, seq, BF16)
        y = _hybrid_mixer(h, tables, l, w_in_t, nsa_cmp_pos[l], nsa_cmp_w1[l], nsa_cmp_w2[l], lb_all[l],
                          hgrn_norm_w[l], w_up_a, w_up_b, batch, seq)
        x2 = _matmul_resid(y, w_out, l, x2, g1, seq)
        h = _norm_mod(x2, norm_mlp_w[l], sc2, sh2, seq, BF16)
        u = _matmul(h, w_mlp1, l, BF16, relu2=True)
        x2 = _matmul_resid_ksplit(u, w_mlp2, l, x2, g2, seq)
    out = _norm_mod(x2, final_norm_w, zeros, zeros, seq, F32)
    return out.reshape(batch, seq, d)
```

```python
import functools
import math

import numpy as np
import jax
import jax.numpy as jnp
from jax import lax
from jax.experimental import pallas as pl
from jax.experimental.pallas import tpu as pltpu

D_MODEL = 4096
DEPTH = 2
NSA_HEADS = 16
NSA_KV_GROUPS = 4
NSA_HPG = NSA_HEADS // NSA_KV_GROUPS
HEAD_DIM = 128
CMP_LEN = 32
CMP_STRIDE = 16
SLC_BLOCK = 64
SLC_TOPK = 16
WINDOW = 512
ATTN_SCALE = HEAD_DIM ** -0.5
FORCED_BONUS = 1e6
NEG_INF = -1e30
HGRN_HEADS = 16
HGRN_DK = 128
HGRN_DV = 128
ROPE_THETA = 500000.0
ROT_DIM = HEAD_DIM // 4
D_FF = 4 * D_MODEL
NORM_EPS = 1e-6

NSA_WIDTH = NSA_HEADS * HEAD_DIM
KV_WIDTH = NSA_KV_GROUPS * HEAD_DIM
HGRN_KW = HGRN_HEADS * HGRN_DK
HGRN_VW = HGRN_HEADS * HGRN_DV
SEG_A = NSA_WIDTH + 6 * KV_WIDTH
SEG_G = 3 * NSA_HEADS
SEG_B = 2 * HGRN_KW + 2 * HGRN_VW + 2 * D_MODEL

LANES = 128
VMEM_LIMIT = 58 * 1024 * 1024

BF16 = jnp.bfloat16
F32 = jnp.float32


def _params(sem):
    return pltpu.CompilerParams(dimension_semantics=sem, vmem_limit_bytes=VMEM_LIMIT)


def _nt_dot(a, b, precision=None):
    return lax.dot_general(a, b, (((1,), (1,)), ((), ())), preferred_element_type=F32, precision=precision)


def _tn_dot(a, b, precision=None):
    return lax.dot_general(a, b, (((0,), (0,)), ((), ())), preferred_element_type=F32, precision=precision)


def _ada_kernel(cb_ref, w_ref, b_ref, o_ref, cact_ref, *, batch, tn):
    @pl.when((pl.program_id(0) == 0) & (pl.program_id(1) == 0))
    def _():
        cb = cb_ref[...]
        cact_ref[...] = cb * jax.nn.sigmoid(cb)

    for b in range(batch):
        cact = cact_ref[b]
        rows = []
        for s in range(tn // LANES):
            w = w_ref[:, s * LANES:(s + 1) * LANES]
            rows.append(jnp.sum(w * cact, axis=0, keepdims=True))
        o_ref[b:b + 1, :] = jnp.concatenate(rows, axis=1) + b_ref[...]


def _ada_mod(c, ada_w, ada_b):
    depth, k, n = ada_w.shape
    batch = c.shape[0]
    tn = 512
    cb = jnp.broadcast_to(c[:, :, None], (batch, k, LANES))
    out = pl.pallas_call(
        functools.partial(_ada_kernel, batch=batch, tn=tn),
        grid=(depth, n // tn),
        in_specs=[
            pl.BlockSpec((batch, k, LANES), lambda l, j: (0, 0, 0)),
            pl.BlockSpec((None, k, tn), lambda l, j: (l, 0, j)),
            pl.BlockSpec((None, 1, tn), lambda l, j: (l, 0, j)),
        ],
        out_specs=pl.BlockSpec((None, batch, tn), lambda l, j: (l, 0, j)),
        out_shape=jax.ShapeDtypeStruct((depth, batch, n), F32),
        scratch_shapes=[pltpu.VMEM((batch, k, LANES), F32)],
        compiler_params=_params(("arbitrary", "arbitrary")),
        name="ada_mod",
    )(cb, ada_w, ada_b.reshape(depth, 1, n))
    return out


def _norm_kernel(x_ref, w_ref, sc_ref, sh_ref, o_ref):
    x = x_ref[...]
    y = x * lax.rsqrt(jnp.mean(x * x, axis=-1, keepdims=True) + NORM_EPS)
    y = y * w_ref[...]
    y = y * (1.0 + sc_ref[...]) + sh_ref[...]
    o_ref[...] = y.astype(o_ref.dtype)


def _norm_mod(x2, w, sc, sh, seq, out_dtype):
    m, d = x2.shape
    batch = m // seq
    tm = 512
    per = seq // tm
    return pl.pallas_call(
        _norm_kernel,
        grid=(m // tm,),
        in_specs=[
            pl.BlockSpec((tm, d), lambda i: (i, 0)),
            pl.BlockSpec((1, d), lambda i: (0, 0)),
            pl.BlockSpec((None, 1, d), lambda i: (i // per, 0, 0)),
            pl.BlockSpec((None, 1, d), lambda i: (i // per, 0, 0)),
        ],
        out_specs=pl.BlockSpec((tm, d), lambda i: (i, 0)),
        out_shape=jax.ShapeDtypeStruct((m, d), out_dtype),
        compiler_params=_params(("parallel",)),
        name="norm_mod",
    )(x2, w.reshape(1, d), sc.reshape(batch, 1, d), sh.reshape(batch, 1, d))


MM_TM = 1024
MM_TN = 512
MM_TK = 4096
MM_TM_WIDE = 512
MM_TN_WIDE = 1024


def _mm_kernel(a_ref, w_ref, o_ref, wb_ref, *, relu2):
    @pl.when(pl.program_id(1) == 0)
    def _():
        wb_ref[...] = w_ref[...].astype(BF16)

    r = jnp.dot(a_ref[...], wb_ref[...], preferred_element_type=F32)
    if relu2:
        r = jnp.square(jnp.maximum(r, 0.0))
    o_ref[...] = r.astype(o_ref.dtype)


def _matmul(a, w, layer, out_dtype, relu2=False, tm=MM_TM_WIDE, tn=MM_TN_WIDE):
    m, kd = a.shape
    n = w.shape[2]
    tm = min(tm, m)
    assert n % tn == 0 and m % tm == 0
    return pl.pallas_call(
        functools.partial(_mm_kernel, relu2=relu2),
        grid=(n // tn, m // tm),
        in_specs=[
            pl.BlockSpec((tm, kd), lambda j, i: (i, 0)),
            pl.BlockSpec((None, kd, tn), lambda j, i: (layer, 0, j)),
        ],
        out_specs=pl.BlockSpec((tm, tn), lambda j, i: (i, j)),
        out_shape=jax.ShapeDtypeStruct((m, n), out_dtype),
        scratch_shapes=[pltpu.VMEM((kd, tn), BF16)],
        compiler_params=_params(("parallel", "arbitrary")),
        name="matmul",
    )(a, w)


def _mm_t_kernel(*refs, shift):
    if shift:
        a_ref, w_ref, wn_ref, o_ref, wb_ref = refs
    else:
        a_ref, w_ref, o_ref, wb_ref = refs

    @pl.when(pl.program_id(1) == 0)
    def _():
        tn = w_ref.shape[0]
        wb_ref[0:tn - shift, :] = w_ref[shift:tn, :].astype(BF16)
        if shift:
            wb_ref[tn - shift:tn, :] = wn_ref[0:shift, :].astype(BF16)

    o_ref[...] = _nt_dot(a_ref[...], wb_ref[...]).astype(o_ref.dtype)


BF16_SUBLANES = 16


def _matmul_t(a, wt, layer, row0, nrows, out_dtype, tm=MM_TM_WIDE, tn=MM_TN_WIDE):
    m, kd = a.shape
    tm = min(tm, m)
    shift = row0 % tn
    base = row0 - shift
    nxt = 64
    assert nrows % tn == 0 and m % tm == 0 and shift % BF16_SUBLANES == 0 and shift <= nxt and tn % nxt == 0
    in_specs = [
        pl.BlockSpec((tm, kd), lambda j, i: (i, 0)),
        pl.BlockSpec((None, tn, kd), lambda j, i: (layer, base // tn + j, 0)),
    ]
    args = [a, wt]
    if shift:
        in_specs.append(pl.BlockSpec((None, nxt, kd), lambda j, i: (layer, (base + tn * (j + 1)) // nxt, 0)))
        args.append(wt)
    return pl.pallas_call(
        functools.partial(_mm_t_kernel, shift=shift),
        grid=(nrows // tn, m // tm),
        in_specs=in_specs,
        out_specs=pl.BlockSpec((tm, tn), lambda j, i: (i, j)),
        out_shape=jax.ShapeDtypeStruct((m, nrows), out_dtype),
        scratch_shapes=[pltpu.VMEM((tn, kd), BF16)],
        compiler_params=_params(("parallel", "arbitrary")),
        name="matmul_t",
    )(*args)


def _mm_resid_kernel(a_ref, w_ref, x_ref, g_ref, o_ref, wb_ref):
    @pl.when(pl.program_id(1) == 0)
    def _():
        wb_ref[...] = w_ref[...].astype(BF16)

    o_ref[...] = x_ref[...] + g_ref[...] * jnp.dot(a_ref[...], wb_ref[...], preferred_element_type=F32)


def _matmul_resid(a, w, layer, x2, gate, seq, tn=MM_TN, tk=None, ks=0):
    m = a.shape[0]
    kd = tk if tk is not None else a.shape[1]
    n = w.shape[2]
    tm = min(MM_TM, seq)
    per = seq // tm
    batch = m // seq
    return pl.pallas_call(
        _mm_resid_kernel,
        grid=(n // tn, m // tm),
        in_specs=[
            pl.BlockSpec((tm, kd), lambda j, i: (i, ks)),
            pl.BlockSpec((None, kd, tn), lambda j, i: (layer, ks, j)),
            pl.BlockSpec((tm, tn), lambda j, i: (i, j)),
            pl.BlockSpec((None, 1, tn), lambda j, i: (i // per, 0, j)),
        ],
        out_specs=pl.BlockSpec((tm, tn), lambda j, i: (i, j)),
        out_shape=jax.ShapeDtypeStruct((m, n), F32),
        scratch_shapes=[pltpu.VMEM((kd, tn), BF16)],
        compiler_params=_params(("parallel", "arbitrary")),
        name="matmul_resid",
    )(a, w, x2, gate.reshape(batch, 1, n))


def _matmul_resid_ksplit(a, w, layer, x2, gate, seq, tk=MM_TK):
    kd = a.shape[1]
    for ks in range(kd // tk):
        x2 = _matmul_resid(a, w, layer, x2, gate, seq, tk=tk, ks=ks)
    return x2


def _mm_merge_kernel(a_ref, b_ref, wa_ref, wb_ref, ga_ref, gb_ref, o_ref, wab_ref, wbb_ref):
    @pl.when(pl.program_id(1) == 0)
    def _():
        wab_ref[...] = wa_ref[...].astype(BF16)
        wbb_ref[...] = wb_ref[...].astype(BF16)

    ya = jnp.dot(a_ref[...], wab_ref[...], preferred_element_type=F32)
    yb = jnp.dot(b_ref[...], wbb_ref[...], preferred_element_type=F32)
    ga, gb = ga_ref[...].astype(F32), gb_ref[...].astype(F32)
    o_ref[...] = (jax.nn.sigmoid(ga) * ya + jax.nn.sigmoid(gb) * yb).astype(o_ref.dtype)


def _matmul_merge(a, b, wa, wb, layer, pb, ga_col, gb_col, tn=MM_TN):
    m, kd = a.shape
    n = wa.shape[2]
    tm = min(MM_TM, m)
    return pl.pallas_call(
        _mm_merge_kernel,
        grid=(n // tn, m // tm),
        in_specs=[
            pl.BlockSpec((tm, kd), lambda j, i: (i, 0)),
            pl.BlockSpec((tm, kd), lambda j, i: (i, 0)),
            pl.BlockSpec((None, kd, tn), lambda j, i: (layer, 0, j)),
            pl.BlockSpec((None, kd, tn), lambda j, i: (layer, 0, j)),
            pl.BlockSpec((tm, tn), lambda j, i: (i, ga_col // tn + j)),
            pl.BlockSpec((tm, tn), lambda j, i: (i, gb_col // tn + j)),
        ],
        out_specs=pl.BlockSpec((tm, tn), lambda j, i: (i, j)),
        out_shape=jax.ShapeDtypeStruct((m, n), BF16),
        scratch_shapes=[pltpu.VMEM((kd, tn), BF16), pltpu.VMEM((kd, tn), BF16)],
        compiler_params=_params(("parallel", "arbitrary")),
        name="matmul_merge",
    )(a, b, wa, wb, pb, pb)


def _gelu_tanh(x):
    c = math.sqrt(2.0 / math.pi)
    return 0.5 * x * (1.0 + jnp.tanh(c * (x + 0.044715 * (x * x * x))))


def _compress_one(x_ref, pos_ref, w1_ref, w2_ref, kv, nhalf):
    hp = lax.Precision.HIGHEST
    half = CMP_LEN // 2
    acc_a = jnp.zeros((nhalf, HEAD_DIM), F32)
    acc_b = jnp.zeros((nhalf, HEAD_DIM), F32)
    for l in range(half):
        xl = x_ref[pl.ds(l, nhalf, stride=CMP_STRIDE), :]
        acc_a = acc_a + jnp.dot(xl + pos_ref[kv, l:l + 1, :], w1_ref[kv, l], precision=hp, preferred_element_type=F32)
        acc_b = acc_b + jnp.dot(xl + pos_ref[kv, half + l:half + l + 1, :], w1_ref[kv, half + l], precision=hp,
                                preferred_element_type=F32)
    hid = acc_a + pltpu.roll(acc_b, nhalf - 1, 0)
    hid = _gelu_tanh(hid)
    return jnp.dot(hid, w2_ref[kv], precision=hp, preferred_element_type=F32)


def _compress_kernel(xk_ref, xv_ref, pos_ref, w1_ref, w2_ref, kc_ref, vct_ref, *, nhalf):
    kc_ref[...] = _compress_one(xk_ref, pos_ref, w1_ref, w2_ref, 0, nhalf)
    vct_ref[...] = _compress_one(xv_ref, pos_ref, w1_ref, w2_ref, 1, nhalf).T


def _compress(pa, pos, w1, w2, batch, seq):
    nhalf = seq // CMP_STRIDE
    g = NSA_KV_GROUPS
    col0 = NSA_WIDTH // HEAD_DIM
    full = lambda shape: pl.BlockSpec(shape, lambda b, gi: (0,) * len(shape))
    return pl.pallas_call(
        functools.partial(_compress_kernel, nhalf=nhalf),
        grid=(batch, g),
        in_specs=[
            pl.BlockSpec((seq, HEAD_DIM), lambda b, gi: (b, col0 + gi)),
            pl.BlockSpec((seq, HEAD_DIM), lambda b, gi: (b, col0 + g + gi)),
            full(pos.shape), full(w1.shape), full(w2.shape),
        ],
        out_specs=[
            pl.BlockSpec((None, None, nhalf, HEAD_DIM), lambda b, gi: (b, gi, 0, 0)),
            pl.BlockSpec((None, None, HEAD_DIM, nhalf), lambda b, gi: (b, gi, 0, 0)),
        ],
        out_shape=[
            jax.ShapeDtypeStruct((batch, g, nhalf, HEAD_DIM), F32),
            jax.ShapeDtypeStruct((batch, g, HEAD_DIM, nhalf), F32),
        ],
        compiler_params=_params(("parallel", "parallel")),
        name="nsa_compress",
    )(pa, pa, pos, w1, w2)


def _cmp_slc_overlap_t(seq, ncp):
    nc = (seq - CMP_LEN) // CMP_STRIDE + 1
    nsel = seq // SLC_BLOCK
    cs = np.arange(nc) * CMP_STRIDE
    ce = cs + CMP_LEN - 1
    ss = np.arange(nsel) * SLC_BLOCK
    se = ss + SLC_BLOCK - 1
    ov = np.minimum(ce[:, None], se[None, :]) - np.maximum(cs[:, None], ss[None, :]) + 1
    ov = np.maximum(ov, 0).astype(np.float32)
    out = np.zeros((nsel, ncp), np.float32)
    out[:, :nc] = ov.T
    return out


def _heads_t(x_ref, fn=None):
    parts = []
    for j in range(NSA_HPG):
        x = x_ref[:, j * HEAD_DIM:(j + 1) * HEAD_DIM]
        if fn is not None:
            x = fn(x)
        parts.append(x.T.astype(BF16))
    return jnp.concatenate(parts, axis=1)


def _cmp_attn_kernel(q_ref, kc_ref, vct_ref, ovt_ref, o_ref, sel_ref, *, tq, ncp, nsel):
    i = pl.program_id(2)
    t0 = i * tq
    tpos = t0 + lax.broadcasted_iota(jnp.int32, (ncp, tq), 1)
    nidx = lax.broadcasted_iota(jnp.int32, (ncp, tq), 0)
    vis = (nidx * CMP_STRIDE + (CMP_LEN - 1)) <= tpos
    qt = _heads_t(q_ref)
    st = jnp.dot(kc_ref[...].astype(BF16), qt, preferred_element_type=F32) * ATTN_SCALE
    vct = vct_ref[...].astype(BF16)
    imp = jnp.zeros((ncp, tq), F32)
    for j in range(NSA_HPG):
        s = jnp.where(vis, st[:, j * tq:(j + 1) * tq], NEG_INF)
        e = jnp.exp(s - jnp.max(s, axis=0, keepdims=True))
        p = e / jnp.sum(e, axis=0, keepdims=True)
        p = jnp.where(vis, p, 0.0)
        o_ref[j * HEAD_DIM:(j + 1) * HEAD_DIM, :] = jnp.dot(vct, p.astype(BF16), preferred_element_type=F32)
        imp = imp + p
    pslc = jnp.dot(ovt_ref[...], imp, precision=lax.Precision.HIGHEST, preferred_element_type=F32)
    tl = t0 + lax.broadcasted_iota(jnp.int32, (nsel, tq), 1)
    mi = lax.broadcasted_iota(jnp.int32, (nsel, tq), 0)
    cur = tl // SLC_BLOCK
    valid = mi * SLC_BLOCK <= tl
    forced = (mi == 0) | (mi == cur) | (mi == cur - 1)
    score = jnp.where(valid, pslc + jnp.where(forced, FORCED_BONUS, 0.0), -jnp.inf)
    topk = min(SLC_TOPK, nsel)
    few = (t0 + tq) <= topk * SLC_BLOCK

    @pl.when(few)
    def _():
        sel_ref[...] = jnp.where(valid, 1.0, 0.0)

    @pl.when(jnp.logical_not(few))
    def _():
        rank = jnp.zeros((nsel, tq), F32)
        for mp in range(nsel):
            row = score[mp:mp + 1, :]
            before = (row > score) | ((row == score) & (mi > mp))
            rank = rank + jnp.where(before, 1.0, 0.0)
        sel_ref[...] = jnp.where(valid & (rank < float(topk)), 1.0, 0.0)


def _cmp_attn(pa, kc, vct, batch, seq, tq=256):
    ncp = seq // CMP_STRIDE
    nsel = seq // SLC_BLOCK
    g = NSA_KV_GROUPS
    nq = seq // tq
    gw = NSA_HPG * HEAD_DIM
    ovt = jnp.asarray(_cmp_slc_overlap_t(seq, ncp))
    return pl.pallas_call(
        functools.partial(_cmp_attn_kernel, tq=tq, ncp=ncp, nsel=nsel),
        grid=(batch, g, nq),
        in_specs=[
            pl.BlockSpec((tq, gw), lambda b, gi, i: (b * nq + i, gi)),
            pl.BlockSpec((None, None, ncp, HEAD_DIM), lambda b, gi, i: (b, gi, 0, 0)),
            pl.BlockSpec((None, None, HEAD_DIM, ncp), lambda b, gi, i: (b, gi, 0, 0)),
            pl.BlockSpec((nsel, ncp), lambda b, gi, i: (0, 0)),
        ],
        out_specs=[
            pl.BlockSpec((None, gw, tq), lambda b, gi, i: (b, gi, i)),
            pl.BlockSpec((None, None, nsel, tq), lambda b, gi, i: (b, gi, 0, i)),
        ],
        out_shape=[
            jax.ShapeDtypeStruct((batch, NSA_WIDTH, seq), F32),
            jax.ShapeDtypeStruct((batch, g, nsel, seq), F32),
        ],
        compiler_params=_params(("parallel", "parallel", "parallel")),
        name="nsa_cmp_attn",
    )(pa, kc, vct, ovt)


def _rope_tables(positions):
    half = ROT_DIM // 2
    inv = ROPE_THETA ** (-jnp.arange(0, ROT_DIM, 2, dtype=F32) / ROT_DIM)
    ang = positions.astype(F32)[..., None] * inv
    cos, sin = jnp.cos(ang), jnp.sin(ang)
    rest = HEAD_DIM - ROT_DIM
    b, s = positions.shape
    one = jnp.ones((b, s, rest), F32)
    zero = jnp.zeros((b, s, rest), F32)
    zh = jnp.zeros((b, s, half), F32)
    c = jnp.concatenate([cos, cos, one], axis=-1)
    s1 = jnp.concatenate([-sin, zh, zero], axis=-1)
    s2 = jnp.concatenate([zh, sin, zero], axis=-1)
    return [t.reshape(b * s, HEAD_DIM) for t in (c, s1, s2)]


def _rot(x, c, s1, s2):
    half = ROT_DIM // 2
    return x * c + pltpu.roll(x, HEAD_DIM - half, 1) * s1 + pltpu.roll(x, half, 1) * s2


def _kv_prep_kernel(xs_ref, xw_ref, c_ref, s1_ref, s2_ref, k_ref, vt_ref):
    c, s1, s2 = c_ref[...], s1_ref[...], s2_ref[...]
    for br, x_ref in enumerate((xs_ref, xw_ref)):
        for gi in range(NSA_KV_GROUPS):
            src = gi * HEAD_DIM
            dst = br * KV_WIDTH + gi * HEAD_DIM
            k_ref[:, dst:dst + HEAD_DIM] = _rot(x_ref[:, src:src + HEAD_DIM], c, s1, s2).astype(k_ref.dtype)
            vt_ref[dst:dst + HEAD_DIM, :] = x_ref[:, KV_WIDTH + src:KV_WIDTH + src + HEAD_DIM].T.astype(vt_ref.dtype)


def _kv_prep(pa, tables, batch, seq, tm=256):
    nt = seq // tm
    w = 2 * KV_WIDTH
    col = (NSA_WIDTH + 2 * KV_WIDTH) // w
    assert col * w == NSA_WIDTH + 2 * KV_WIDTH
    return pl.pallas_call(
        _kv_prep_kernel,
        grid=(batch, nt),
        in_specs=[pl.BlockSpec((tm, w), lambda b, i: (b * nt + i, col)),
                  pl.BlockSpec((tm, w), lambda b, i: (b * nt + i, col + 1))]
        + [pl.BlockSpec((tm, HEAD_DIM), lambda b, i: (b * nt + i, 0))] * 3,
        out_specs=[
            pl.BlockSpec((tm, 2 * KV_WIDTH), lambda b, i: (b * nt + i, 0)),
            pl.BlockSpec((None, 2 * KV_WIDTH, tm), lambda b, i: (b, 0, i)),
        ],
        out_shape=[
            jax.ShapeDtypeStruct((batch * seq, 2 * KV_WIDTH), BF16),
            jax.ShapeDtypeStruct((batch, 2 * KV_WIDTH, seq), BF16),
        ],
        compiler_params=_params(("parallel", "parallel")),
        name="nsa_kv_prep",
    )(pa, pa, *tables)


def _scores_t(k_ref, qt, kt, tk):
    k0 = pl.multiple_of(kt * tk, tk)
    return jnp.dot(k_ref[pl.ds(k0, tk), :], qt, preferred_element_type=F32)


def _flash_init(m_ref, l_ref, acc_ref):
    m_ref[...] = jnp.full(m_ref.shape, NEG_INF, F32)
    l_ref[...] = jnp.zeros(l_ref.shape, F32)
    acc_ref[...] = jnp.zeros(acc_ref.shape, F32)


def _flash_consume_t(s_ref, vt_ref, kt, mask, m_ref, l_ref, acc_ref, tk):
    k0 = pl.multiple_of(kt * tk, tk)
    vt = vt_ref[:, pl.ds(k0, tk)]
    mask = jnp.concatenate([mask] * NSA_HPG, axis=1)
    s = jnp.where(mask, s_ref[...], NEG_INF)
    m_old = m_ref[...]
    m_new = jnp.maximum(m_old, jnp.max(s, axis=0, keepdims=True))
    m_use = jnp.where(m_new > 0.5 * NEG_INF, m_new, 0.0)
    p = jnp.exp2(s - m_use)
    alpha = jnp.exp2(m_old - m_new)
    l_ref[...] = alpha * l_ref[...] + jnp.sum(p, axis=0, keepdims=True)
    acc_ref[...] = alpha * acc_ref[...] + jnp.dot(vt, p.astype(BF16), preferred_element_type=F32)
    m_ref[...] = m_new


def _sel_attn_kernel(q_ref, c_ref, s1_ref, s2_ref, ks_ref, kw_ref, vst_ref, vwt_ref, sel_ref, oct_ref, ng_ref, o_ref,
                     m_ref, l_ref, acc_ref, gt_ref, s_ref, *, tq, tk):
    gi = pl.program_id(1)
    i = pl.program_id(2)
    t0 = i * tq
    nj = NSA_HPG
    c, s1, s2 = c_ref[...], s1_ref[...], s2_ref[...]
    scale2 = ATTN_SCALE * math.log2(math.e)
    qt = _heads_t(q_ref, lambda x: _rot(x, c, s1, s2) * scale2)
    kpos = lax.broadcasted_iota(jnp.int32, (tk, tq), 0)
    tpos = t0 + lax.broadcasted_iota(jnp.int32, (tk, tq), 1)
    nb = tk // SLC_BLOCK

    def sel_mask(kt, diag):
        rows = sel_ref[pl.ds(pl.multiple_of(kt * nb, nb), nb), :]
        chosen = jnp.broadcast_to(rows[:, None, :], (nb, SLC_BLOCK, tq)).reshape(tk, tq) > 0.5
        return chosen & ((kt * tk + kpos) <= tpos) if diag else chosen

    def win_mask(kt):
        kp = kt * tk + kpos
        return (kp <= tpos) & (kp > tpos - WINDOW)

    sel_st = (m_ref.at[0], l_ref.at[0], acc_ref.at[0])
    win_st = (m_ref.at[1], l_ref.at[1], acc_ref.at[1])
    _flash_init(*sel_st)
    _flash_init(*win_st)
    last = (t0 + tq - 1) // tk + 1
    first = jnp.maximum(t0 - (WINDOW - 1), 0) // tk

    s_ref[...] = _scores_t(ks_ref, qt, 0, tk)

    def sel_trip(kt, carry):
        s_next = _scores_t(ks_ref, qt, kt + 1, tk)
        _flash_consume_t(s_ref, vst_ref, kt, sel_mask(kt, False), *sel_st, tk)
        s_ref[...] = s_next
        return carry

    lax.fori_loop(0, last - 1, sel_trip, 0)
    s_next = _scores_t(kw_ref, qt, first, tk)
    _flash_consume_t(s_ref, vst_ref, last - 1, sel_mask(last - 1, True), *sel_st, tk)
    s_ref[...] = s_next

    def win_trip(kt, carry):
        s_next = _scores_t(kw_ref, qt, kt + 1, tk)
        _flash_consume_t(s_ref, vwt_ref, kt, win_mask(kt), *win_st, tk)
        s_ref[...] = s_next
        return carry

    lax.fori_loop(first, last - 1, win_trip, 0)
    _flash_consume_t(s_ref, vwt_ref, last - 1, win_mask(last - 1), *win_st, tk)
    o_slc = acc_ref[0] / l_ref[0]
    o_win = acc_ref[1] / l_ref[1]
    gt_ref[...] = jax.nn.sigmoid(ng_ref[...].T)
    for j in range(nj):
        gate = [gt_ref[pl.ds(br * NSA_HEADS + gi * nj + j, 1), :] for br in range(3)]
        ot = (gate[0] * oct_ref[j * HEAD_DIM:(j + 1) * HEAD_DIM, :] + gate[1] * o_slc[:, j * tq:(j + 1) * tq]
              + gate[2] * o_win[:, j * tq:(j + 1) * tq])
        o_ref[:, j * HEAD_DIM:(j + 1) * HEAD_DIM] = ot.T.astype(o_ref.dtype)


def _sel_attn(pa, tables, kn, vt, sel, o_cmp_t, png, batch, seq, tq=256, tk=512):
    g = NSA_KV_GROUPS
    nq = seq // tq
    gw = NSA_HPG * HEAD_DIM
    tk = min(tk, seq)
    assert tk % tq == 0 and seq % tk == 0
    nsel = seq // SLC_BLOCK
    n = NSA_HPG * tq
    tab = pl.BlockSpec((tq, HEAD_DIM), lambda b, gi, i: (b * nq + i, 0))
    return pl.pallas_call(
        functools.partial(_sel_attn_kernel, tq=tq, tk=tk),
        grid=(batch, g, nq),
        in_specs=[
            pl.BlockSpec((tq, gw), lambda b, gi, i: (b * nq + i, gi)),
            tab, tab, tab,
            pl.BlockSpec((seq, HEAD_DIM), lambda b, gi, i: (b, gi)),
            pl.BlockSpec((seq, HEAD_DIM), lambda b, gi, i: (b, g + gi)),
            pl.BlockSpec((None, HEAD_DIM, seq), lambda b, gi, i: (b, gi, 0)),
            pl.BlockSpec((None, HEAD_DIM, seq), lambda b, gi, i: (b, g + gi, 0)),
            pl.BlockSpec((None, None, nsel, tq), lambda b, gi, i: (b, gi, 0, i)),
            pl.BlockSpec((None, gw, tq), lambda b, gi, i: (b, gi, i)),
            pl.BlockSpec((tq, LANES), lambda b, gi, i: (b * nq + i, 0)),
        ],
        out_specs=pl.BlockSpec((tq, gw), lambda b, gi, i: (b * nq + i, gi)),
        out_shape=jax.ShapeDtypeStruct((batch * seq, NSA_WIDTH), BF16),
        scratch_shapes=[
            pltpu.VMEM((2, 1, n), F32),
            pltpu.VMEM((2, 1, n), F32),
            pltpu.VMEM((2, HEAD_DIM, n), F32),
            pltpu.VMEM((LANES, tq), F32),
            pltpu.VMEM((tk, n), F32),
        ],
        compiler_params=_params(("parallel", "parallel", "parallel")),
        name="nsa_sel_win_attn",
    )(pa, *tables, kn, kn, vt, vt, sel, o_cmp_t, png)


HG_TILE = 128
HG_GROUP = 8
HG_LEVELS = (64, 32, 16, 8)
HG_HEADS_PER_STEP = 8


def _split3(x):
    hi = x.astype(BF16)
    r = x - hi.astype(F32)
    mid = r.astype(BF16)
    lo = (r - mid.astype(F32)).astype(BF16)
    return hi, mid, lo


def _group_row(x, size, row):
    t, d = x.shape
    x3 = x.reshape(t // size, size, d)
    return jnp.broadcast_to(x3[:, row:row + 1, :], x3.shape).reshape(t, d)


def _hgrn_heads(qs, zs, vs, gs, lbs, nw, low, sts):
    t = HG_TILE
    nh = len(qs)
    heads = range(nh)
    log2e = math.log2(math.e)
    kks, l2fs = [], []
    for i in heads:
        z, lb = zs[i], lbs[i]
        ez = jnp.exp(-jnp.abs(z))
        r = 1.0 / (1.0 + ez)
        pos = z >= 0.0
        f = lb + (1.0 - lb) * jnp.where(pos, r, ez * r)
        kks.append((1.0 - lb) * jnp.where(pos, ez * r, r))
        l2fs.append(jnp.where(f > 0.0, jnp.log2(f), jnp.minimum(z, 0.0) * log2e))
    bs_ = []
    for i in heads:
        hi, mid, lo = _split3(l2fs[i])
        bs_.append(jnp.dot(low, hi, preferred_element_type=F32) + jnp.dot(low, mid, preferred_element_type=F32)
                   + jnp.dot(low, lo, preferred_element_type=F32))
    ti = lax.broadcasted_iota(jnp.int32, (t, t), 0)
    si = lax.broadcasted_iota(jnp.int32, (t, t), 1)
    atts = [jnp.zeros((t, t), F32) for _ in heads]
    for h in HG_LEVELS:
        pair = ((ti // (2 * h)) == (si // (2 * h))) & ((ti % (2 * h)) >= h) & ((si % (2 * h)) < h)
        for i in heads:
            b = bs_[i]
            bref = _group_row(b, 2 * h, h - 1)
            e = jnp.exp2(-jnp.abs(b - bref))
            atts[i] = atts[i] + jnp.where(pair, _nt_dot((qs[i] * e).astype(BF16), (kks[i] * e).astype(BF16)), 0.0)
    lane_d = lax.broadcasted_iota(jnp.int32, (HGRN_DK, t), 1)
    diags = [jnp.zeros((t, t), F32) for _ in heads]
    for sg in range(HG_GROUP):
        spread = jnp.where((lane_d % HG_GROUP) == sg, 1.0, 0.0).astype(BF16)
        for i in heads:
            bsg = _group_row(bs_[i], HG_GROUP, sg)
            ksg = _group_row(kks[i], HG_GROUP, sg)
            p = (qs[i] * jnp.exp2(jnp.minimum(bs_[i] - bsg, 0.0))) * ksg
            diags[i] = diags[i] + jnp.dot(p.astype(BF16), spread, preferred_element_type=F32)
    same = ((ti // HG_GROUP) == (si // HG_GROUP)) & (si <= ti)
    ys, st_new = [], []
    for i in heads:
        b = bs_[i]
        att = atts[i] + jnp.where(same, diags[i], 0.0)
        vb = vs[i].astype(BF16)
        o = jnp.dot(att.astype(BF16), vb, preferred_element_type=F32)
        o = o + _nt_dot((qs[i] * jnp.exp2(b)).astype(BF16), sts[i].astype(BF16))
        b_last = b[t - 1:t, :]
        kdec = (kks[i] * jnp.exp2(b_last - b)).astype(BF16)
        st_new.append(sts[i] * jnp.exp2(b_last) + _tn_dot(vb, kdec))
        y = o * lax.rsqrt(jnp.mean(o * o, axis=-1, keepdims=True) + NORM_EPS) * nw
        g = gs[i]
        ys.append(y * (g * jax.nn.sigmoid(g)))
    return ys, st_new


def _hgrn_kernel(q_ref, z_ref, v_ref, g_ref, lb_ref, nw_ref, low_ref, o_ref, st_ref, *, nh):
    @pl.when(pl.program_id(2) == 0)
    def _():
        st_ref[...] = jnp.zeros_like(st_ref)

    cols = [slice(hd * HGRN_DK, (hd + 1) * HGRN_DK) for hd in range(nh)]
    load = lambda ref: [ref[:, c].astype(F32) for c in cols]
    ys, st_new = _hgrn_heads(load(q_ref), load(z_ref), load(v_ref), load(g_ref), [lb_ref[hd:hd + 1, :] for hd in range(nh)],
                             nw_ref[...], low_ref[...], [st_ref[hd] for hd in range(nh)])
    for hd in range(nh):
        st_ref[hd] = st_new[hd]
        o_ref[:, cols[hd]] = ys[hd].astype(o_ref.dtype)


def _hgrn(pb, lb, norm_w, batch, seq):
    hh = HGRN_HEADS
    nh = HG_HEADS_PER_STEP
    nt = seq // HG_TILE
    low = jnp.asarray(np.tril(np.ones((HG_TILE, HG_TILE), np.float32)), dtype=BF16)
    spec = lambda part: pl.BlockSpec((HG_TILE, nh * HGRN_DK), lambda b, h, i: (b * nt + i, part * (hh // nh) + h))
    return pl.pallas_call(
        functools.partial(_hgrn_kernel, nh=nh),
        grid=(batch, hh // nh, nt),
        in_specs=[
            spec(0), spec(1), spec(2), spec(3),
            pl.BlockSpec((None, nh, HGRN_DK), lambda b, h, i: (h, 0, 0)),
            pl.BlockSpec((1, HGRN_DV), lambda b, h, i: (0, 0)),
            pl.BlockSpec(low.shape, lambda b, h, i: (0, 0)),
        ],
        out_specs=pl.BlockSpec((HG_TILE, nh * HGRN_DV), lambda b, h, i: (b * nt + i, h)),
        out_shape=jax.ShapeDtypeStruct((batch * seq, HGRN_VW), BF16),
        scratch_shapes=[pltpu.VMEM((nh, HGRN_DV, HGRN_DK), F32)],
        compiler_params=_params(("parallel", "parallel", "arbitrary")),
        name="hgrn2",
    )(pb, pb, pb, pb, lb.reshape(hh // nh, nh, HGRN_DK), norm_w.reshape(1, HGRN_DV), low)


def _hybrid_mixer(h, tables, layer, w_in_t, cmp_pos, cmp_w1, cmp_w2, lb, g_norm_w, w_up_a, w_up_b, batch, seq):
    pa = _matmul_t(h, w_in_t, layer, 0, SEG_A, F32)
    png = _matmul_t(h, w_in_t, layer, SEG_A, LANES, F32, tn=LANES)
    pb = _matmul_t(h, w_in_t, layer, SEG_A + SEG_G, SEG_B, BF16)
    kc, vct = _compress(pa, cmp_pos, cmp_w1, cmp_w2, batch, seq)
    o_cmp_t, sel = _cmp_attn(pa, kc, vct, batch, seq)
    kn, vt = _kv_prep(pa, tables, batch, seq)
    o_nsa = _sel_attn(pa, tables, kn, vt, sel, o_cmp_t, png, batch, seq)
    o_hgrn = _hgrn(pb, lb, g_norm_w, batch, seq)
    return _matmul_merge(o_nsa, o_hgrn, w_up_a, w_up_b, layer, pb, 2 * HGRN_KW + 2 * HGRN_VW,
                         2 * HGRN_KW + 2 * HGRN_VW + D_MODEL)


def kernel(x, c, positions, ada_w, ada_b, norm_mix_w, w_in, nsa_cmp_pos, nsa_cmp_w1, nsa_cmp_w2, hgrn_lb_logits,
           hgrn_norm_w, w_up_a, w_up_b, w_out, norm_mlp_w, w_mlp1, w_mlp2, final_norm_w):
    batch, seq, d = x.shape
    depth = ada_w.shape[0]
    lb_all = jnp.cumsum(jax.nn.softmax(hgrn_lb_logits.astype(F32), axis=0), axis=0)
    lb_all = lb_all - lb_all[0:1]
    mod = _ada_mod(c, ada_w, ada_b)
    tables = _rope_tables(positions)
    x2 = x.reshape(batch * seq, d)
    zeros = jnp.zeros((batch, d), F32)
    w_in_t = jnp.transpose(w_in, (0, 2, 1))
    for l in range(depth):
        sh1, sc1, g1, sh2, sc2, g2 = [mod[l][:, k * d:(k + 1) * d] for k in range(6)]
        h = _norm_mod(x2, norm_mix_w[l], sc1, sh1, seq, BF16)
        y = _hybrid_mixer(h, tables, l, w_in_t, nsa_cmp_pos[l], nsa_cmp_w1[l], nsa_cmp_w2[l], lb_all[l],
                          hgrn_norm_w[l], w_up_a, w_up_b, batch, seq)
        x2 = _matmul_resid(y, w_out, l, x2, g1, seq)
        h = _norm_mod(x2, norm_mlp_w[l], sc2, sh2, seq, BF16)
        u = _matmul(h, w_mlp1, l, BF16, relu2=True)
        x2 = _matmul_resid_ksplit(u, w_mlp2, l, x2, g2, seq)
    out = _norm_mod(x2, final_norm_w, zeros, zeros, seq, F32)
    return out.reshape(batch, seq, d)
```

```python
import functools
import math

import numpy as np
import jax
import jax.numpy as jnp
from jax import lax
from jax.experimental import pallas as pl
from jax.experimental.pallas import tpu as pltpu

D_MODEL = 4096
DEPTH = 2
NSA_HEADS = 16
NSA_KV_GROUPS = 4
NSA_HPG = NSA_HEADS // NSA_KV_GROUPS
HEAD_DIM = 128
CMP_LEN = 32
CMP_STRIDE = 16
SLC_BLOCK = 64
SLC_TOPK = 16
WINDOW = 512
ATTN_SCALE = HEAD_DIM ** -0.5
FORCED_BONUS = 1e6
NEG_INF = -1e30
HGRN_HEADS = 16
HGRN_DK = 128
HGRN_DV = 128
ROPE_THETA = 500000.0
ROT_DIM = HEAD_DIM // 4
D_FF = 4 * D_MODEL
NORM_EPS = 1e-6

NSA_WIDTH = NSA_HEADS * HEAD_DIM
KV_WIDTH = NSA_KV_GROUPS * HEAD_DIM
HGRN_KW = HGRN_HEADS * HGRN_DK
HGRN_VW = HGRN_HEADS * HGRN_DV
SEG_A = NSA_WIDTH + 6 * KV_WIDTH
SEG_G = 3 * NSA_HEADS
SEG_B = 2 * HGRN_KW + 2 * HGRN_VW + 2 * D_MODEL

LANES = 128
VMEM_LIMIT = 58 * 1024 * 1024

BF16 = jnp.bfloat16
F32 = jnp.float32


def _params(sem):
    return pltpu.CompilerParams(dimension_semantics=sem, vmem_limit_bytes=VMEM_LIMIT)


def _nt_dot(a, b, precision=None):
    return lax.dot_general(a, b, (((1,), (1,)), ((), ())), preferred_element_type=F32, precision=precision)


def _tn_dot(a, b, precision=None):
    return lax.dot_general(a, b, (((0,), (0,)), ((), ())), preferred_element_type=F32, precision=precision)


def _ada_kernel(cb_ref, w_ref, b_ref, o_ref, cact_ref, *, batch, tn):
    @pl.when((pl.program_id(0) == 0) & (pl.program_id(1) == 0))
    def _():
        cb = cb_ref[...]
        cact_ref[...] = cb * jax.nn.sigmoid(cb)

    for b in range(batch):
        cact = cact_ref[b]
        rows = []
        for s in range(tn // LANES):
            w = w_ref[:, s * LANES:(s + 1) * LANES]
            rows.append(jnp.sum(w * cact, axis=0, keepdims=True))
        o_ref[b:b + 1, :] = jnp.concatenate(rows, axis=1) + b_ref[...]


def _ada_mod(c, ada_w, ada_b):
    depth, k, n = ada_w.shape
    batch = c.shape[0]
    tn = 512
    cb = jnp.broadcast_to(c[:, :, None], (batch, k, LANES))
    out = pl.pallas_call(
        functools.partial(_ada_kernel, batch=batch, tn=tn),
        grid=(depth, n // tn),
        in_specs=[
            pl.BlockSpec((batch, k, LANES), lambda l, j: (0, 0, 0)),
            pl.BlockSpec((None, k, tn), lambda l, j: (l, 0, j)),
            pl.BlockSpec((None, 1, tn), lambda l, j: (l, 0, j)),
        ],
        out_specs=pl.BlockSpec((None, batch, tn), lambda l, j: (l, 0, j)),
        out_shape=jax.ShapeDtypeStruct((depth, batch, n), F32),
        scratch_shapes=[pltpu.VMEM((batch, k, LANES), F32)],
        compiler_params=_params(("arbitrary", "arbitrary")),
        name="ada_mod",
    )(cb, ada_w, ada_b.reshape(depth, 1, n))
    return out


def _norm_kernel(x_ref, w_ref, sc_ref, sh_ref, o_ref):
    x = x_ref[...]
    y = x * lax.rsqrt(jnp.mean(x * x, axis=-1, keepdims=True) + NORM_EPS)
    y = y * w_ref[...]
    y = y * (1.0 + sc_ref[...]) + sh_ref[...]
    o_ref[...] = y.astype(o_ref.dtype)


def _norm_mod(x2, w, sc, sh, seq, out_dtype):
    m, d = x2.shape
    batch = m // seq
    tm = 512
    per = seq // tm
    return pl.pallas_call(
        _norm_kernel,
        grid=(m // tm,),
        in_specs=[
            pl.BlockSpec((tm, d), lambda i: (i, 0)),
            pl.BlockSpec((1, d), lambda i: (0, 0)),
            pl.BlockSpec((None, 1, d), lambda i: (i // per, 0, 0)),
            pl.BlockSpec((None, 1, d), lambda i: (i // per, 0, 0)),
        ],
        out_specs=pl.BlockSpec((tm, d), lambda i: (i, 0)),
        out_shape=jax.ShapeDtypeStruct((m, d), out_dtype),
        compiler_params=_params(("parallel",)),
        name="norm_mod",
    )(x2, w.reshape(1, d), sc.reshape(batch, 1, d), sh.reshape(batch, 1, d))


MM_TM = 1024
MM_TN = 512
MM_TK = 4096
MM_TM_WIDE = 512
MM_TN_WIDE = 1024


def _mm_kernel(a_ref, w_ref, o_ref, wb_ref, *, relu2):
    @pl.when(pl.program_id(1) == 0)
    def _():
        wb_ref[...] = w_ref[...].astype(BF16)

    r = jnp.dot(a_ref[...], wb_ref[...], preferred_element_type=F32)
    if relu2:
        r = jnp.square(jnp.maximum(r, 0.0))
    o_ref[...] = r.astype(o_ref.dtype)


def _matmul(a, w, layer, out_dtype, relu2=False, tm=MM_TM_WIDE, tn=MM_TN_WIDE):
    m, kd = a.shape
    n = w.shape[2]
    tm = min(tm, m)
    assert n % tn == 0 and m % tm == 0
    return pl.pallas_call(
        functools.partial(_mm_kernel, relu2=relu2),
        grid=(n // tn, m // tm),
        in_specs=[
            pl.BlockSpec((tm, kd), lambda j, i: (i, 0)),
            pl.BlockSpec((None, kd, tn), lambda j, i: (layer, 0, j)),
        ],
        out_specs=pl.BlockSpec((tm, tn), lambda j, i: (i, j)),
        out_shape=jax.ShapeDtypeStruct((m, n), out_dtype),
        scratch_shapes=[pltpu.VMEM((kd, tn), BF16)],
        compiler_params=_params(("parallel", "arbitrary")),
        name="matmul",
    )(a, w)


def _mm_t_kernel(*refs, shift):
    if shift:
        a_ref, w_ref, wn_ref, o_ref, wb_ref = refs
    else:
        a_ref, w_ref, o_ref, wb_ref = refs

    @pl.when(pl.program_id(1) == 0)
    def _():
        tn = w_ref.shape[0]
        wb_ref[0:tn - shift, :] = w_ref[shift:tn, :].astype(BF16)
        if shift:
            wb_ref[tn - shift:tn, :] = wn_ref[0:shift, :].astype(BF16)

    o_ref[...] = _nt_dot(a_ref[...], wb_ref[...]).astype(o_ref.dtype)


BF16_SUBLANES = 16


def _matmul_t(a, wt, layer, row0, nrows, out_dtype, tm=MM_TM_WIDE, tn=MM_TN_WIDE):
    m, kd = a.shape
    tm = min(tm, m)
    shift = row0 % tn
    base = row0 - shift
    nxt = 64
    assert nrows % tn == 0 and m % tm == 0 and shift % BF16_SUBLANES == 0 and shift <= nxt and tn % nxt == 0
    in_specs = [
        pl.BlockSpec((tm, kd), lambda j, i: (i, 0)),
        pl.BlockSpec((None, tn, kd), lambda j, i: (layer, base // tn + j, 0)),
    ]
    args = [a, wt]
    if shift:
        in_specs.append(pl.BlockSpec((None, nxt, kd), lambda j, i: (layer, (base + tn * (j + 1)) // nxt, 0)))
        args.append(wt)
    return pl.pallas_call(
        functools.partial(_mm_t_kernel, shift=shift),
        grid=(nrows // tn, m // tm),
        in_specs=in_specs,
        out_specs=pl.BlockSpec((tm, tn), lambda j, i: (i, j)),
        out_shape=jax.ShapeDtypeStruct((m, nrows), out_dtype),
        scratch_shapes=[pltpu.VMEM((tn, kd), BF16)],
        compiler_params=_params(("parallel", "arbitrary")),
        name="matmul_t",
    )(*args)


def _mm_resid_kernel(a_ref, w_ref, x_ref, g_ref, o_ref, wb_ref):
    @pl.when(pl.program_id(1) == 0)
    def _():
        wb_ref[...] = w_ref[...].astype(BF16)

    o_ref[...] = x_ref[...] + g_ref[...] * jnp.dot(a_ref[...], wb_ref[...], preferred_element_type=F32)


def _matmul_resid(a, w, layer, x2, gate, seq, tn=MM_TN, tk=None, ks=0):
    m = a.shape[0]
    kd = tk if tk is not None else a.shape[1]
    n = w.shape[2]
    tm = min(MM_TM, seq)
    per = seq // tm
    batch = m // seq
    return pl.pallas_call(
        _mm_resid_kernel,
        grid=(n // tn, m // tm),
        in_specs=[
            pl.BlockSpec((tm, kd), lambda j, i: (i, ks)),
            pl.BlockSpec((None, kd, tn), lambda j, i: (layer, ks, j)),
            pl.BlockSpec((tm, tn), lambda j, i: (i, j)),
            pl.BlockSpec((None, 1, tn), lambda j, i: (i // per, 0, j)),
        ],
        out_specs=pl.BlockSpec((tm, tn), lambda j, i: (i, j)),
        out_shape=jax.ShapeDtypeStruct((m, n), F32),
        scratch_shapes=[pltpu.VMEM((kd, tn), BF16)],
        compiler_params=_params(("parallel", "arbitrary")),
        name="matmul_resid",
    )(a, w, x2, gate.reshape(batch, 1, n))


def _matmul_resid_ksplit(a, w, layer, x2, gate, seq, tk=MM_TK):
    kd = a.shape[1]
    for ks in range(kd // tk):
        x2 = _matmul_resid(a, w, layer, x2, gate, seq, tk=tk, ks=ks)
    return x2


def _mm_merge_kernel(a_ref, b_ref, wa_ref, wb_ref, ga_ref, gb_ref, o_ref, wab_ref, wbb_ref):
    @pl.when(pl.program_id(1) == 0)
    def _():
        wab_ref[...] = wa_ref[...].astype(BF16)
        wbb_ref[...] = wb_ref[...].astype(BF16)

    ya = jnp.dot(a_ref[...], wab_ref[...], preferred_element_type=F32)
    yb = jnp.dot(b_ref[...], wbb_ref[...], preferred_element_type=F32)
    ga, gb = ga_ref[...].astype(F32), gb_ref[...].astype(F32)
    o_ref[...] = (jax.nn.sigmoid(ga) * ya + jax.nn.sigmoid(gb) * yb).astype(o_ref.dtype)


def _matmul_merge(a, b, wa, wb, layer, pb, ga_col, gb_col, tn=MM_TN):
    m, kd = a.shape
    n = wa.shape[2]
    tm = min(MM_TM, m)
    return pl.pallas_call(
        _mm_merge_kernel,
        grid=(n // tn, m // tm),
        in_specs=[
            pl.BlockSpec((tm, kd), lambda j, i: (i, 0)),
            pl.BlockSpec((tm, kd), lambda j, i: (i, 0)),
            pl.BlockSpec((None, kd, tn), lambda j, i: (layer, 0, j)),
            pl.BlockSpec((None, kd, tn), lambda j, i: (layer, 0, j)),
            pl.BlockSpec((tm, tn), lambda j, i: (i, ga_col // tn + j)),
            pl.BlockSpec((tm, tn), lambda j, i: (i, gb_col // tn + j)),
        ],
        out_specs=pl.BlockSpec((tm, tn), lambda j, i: (i, j)),
        out_shape=jax.ShapeDtypeStruct((m, n), BF16),
        scratch_shapes=[pltpu.VMEM((kd, tn), BF16), pltpu.VMEM((kd, tn), BF16)],
        compiler_params=_params(("parallel", "arbitrary")),
        name="matmul_merge",
    )(a, b, wa, wb, pb, pb)


def _gelu_tanh(x):
    c = math.sqrt(2.0 / math.pi)
    return 0.5 * x * (1.0 + jnp.tanh(c * (x + 0.044715 * (x * x * x))))


def _split2(x):
    hi = x.astype(BF16)
    return hi, (x - hi.astype(F32)).astype(BF16)


def _dot3(a, b):
    ah, al = _split2(a)
    bh, bl = _split2(b)
    dot = functools.partial(jnp.dot, preferred_element_type=F32)
    return dot(ah, bh) + (dot(ah, bl) + dot(al, bh))


def _compress_one(x_ref, pos_ref, w1_ref, w2_ref, kv, nhalf):
    half = CMP_LEN // 2
    xs = [x_ref[pl.ds(l, nhalf, stride=CMP_STRIDE), :] for l in range(half)]
    lhs_a = jnp.concatenate([xs[l] + pos_ref[kv, l:l + 1, :] for l in range(half)], axis=1)
    lhs_b = jnp.concatenate([xs[l] + pos_ref[kv, half + l:half + l + 1, :] for l in range(half)], axis=1)
    w1 = w1_ref[kv]
    acc_a = _dot3(lhs_a, w1[:half].reshape(half * HEAD_DIM, HEAD_DIM))
    acc_b = _dot3(lhs_b, w1[half:].reshape(half * HEAD_DIM, HEAD_DIM))
    hid = acc_a + pltpu.roll(acc_b, nhalf - 1, 0)
    hid = _gelu_tanh(hid)
    return _dot3(hid, w2_ref[kv])


def _compress_kernel(xk_ref, xv_ref, pos_ref, w1_ref, w2_ref, kc_ref, vct_ref, *, nhalf):
    kc_ref[...] = _compress_one(xk_ref, pos_ref, w1_ref, w2_ref, 0, nhalf)
    vct_ref[...] = _compress_one(xv_ref, pos_ref, w1_ref, w2_ref, 1, nhalf).T


def _compress(pa, pos, w1, w2, batch, seq):
    nhalf = seq // CMP_STRIDE
    g = NSA_KV_GROUPS
    col0 = NSA_WIDTH // HEAD_DIM
    full = lambda shape: pl.BlockSpec(shape, lambda b, gi: (0,) * len(shape))
    return pl.pallas_call(
        functools.partial(_compress_kernel, nhalf=nhalf),
        grid=(batch, g),
        in_specs=[
            pl.BlockSpec((seq, HEAD_DIM), lambda b, gi: (b, col0 + gi)),
            pl.BlockSpec((seq, HEAD_DIM), lambda b, gi: (b, col0 + g + gi)),
            full(pos.shape), full(w1.shape), full(w2.shape),
        ],
        out_specs=[
            pl.BlockSpec((None, None, nhalf, HEAD_DIM), lambda b, gi: (b, gi, 0, 0)),
            pl.BlockSpec((None, None, HEAD_DIM, nhalf), lambda b, gi: (b, gi, 0, 0)),
        ],
        out_shape=[
            jax.ShapeDtypeStruct((batch, g, nhalf, HEAD_DIM), F32),
            jax.ShapeDtypeStruct((batch, g, HEAD_DIM, nhalf), F32),
        ],
        compiler_params=_params(("parallel", "parallel")),
        name="nsa_compress",
    )(pa, pa, pos, w1, w2)


def _cmp_slc_overlap_t(seq, ncp):
    nc = (seq - CMP_LEN) // CMP_STRIDE + 1
    nsel = seq // SLC_BLOCK
    cs = np.arange(nc) * CMP_STRIDE
    ce = cs + CMP_LEN - 1
    ss = np.arange(nsel) * SLC_BLOCK
    se = ss + SLC_BLOCK - 1
    ov = np.minimum(ce[:, None], se[None, :]) - np.maximum(cs[:, None], ss[None, :]) + 1
    ov = np.maximum(ov, 0).astype(np.float32)
    out = np.zeros((nsel, ncp), np.float32)
    out[:, :nc] = ov.T
    return out


def _heads_t(x_ref, fn=None):
    parts = []
    for j in range(NSA_HPG):
        x = x_ref[:, j * HEAD_DIM:(j + 1) * HEAD_DIM]
        if fn is not None:
            x = fn(x)
        parts.append(x.T.astype(BF16))
    return jnp.concatenate(parts, axis=1)


def _cmp_attn_kernel(q_ref, kc_ref, vct_ref, ovt_ref, o_ref, sel_ref, *, tq, ncp, nsel):
    i = pl.program_id(2)
    t0 = i * tq
    tpos = t0 + lax.broadcasted_iota(jnp.int32, (ncp, tq), 1)
    nidx = lax.broadcasted_iota(jnp.int32, (ncp, tq), 0)
    vis = (nidx * CMP_STRIDE + (CMP_LEN - 1)) <= tpos
    qt = _heads_t(q_ref)
    st = jnp.dot(kc_ref[...].astype(BF16), qt, preferred_element_type=F32) * ATTN_SCALE
    vct = vct_ref[...].astype(BF16)
    imp = jnp.zeros((ncp, tq), F32)
    for j in range(NSA_HPG):
        s = jnp.where(vis, st[:, j * tq:(j + 1) * tq], NEG_INF)
        e = jnp.exp(s - jnp.max(s, axis=0, keepdims=True))
        p = e / jnp.sum(e, axis=0, keepdims=True)
        p = jnp.where(vis, p, 0.0)
        o_ref[j * HEAD_DIM:(j + 1) * HEAD_DIM, :] = jnp.dot(vct, p.astype(BF16), preferred_element_type=F32)
        imp = imp + p
    pslc = jnp.dot(ovt_ref[...], imp, precision=lax.Precision.HIGHEST, preferred_element_type=F32)
    tl = t0 + lax.broadcasted_iota(jnp.int32, (nsel, tq), 1)
    mi = lax.broadcasted_iota(jnp.int32, (nsel, tq), 0)
    cur = tl // SLC_BLOCK
    valid = mi * SLC_BLOCK <= tl
    forced = (mi == 0) | (mi == cur) | (mi == cur - 1)
    score = jnp.where(valid, pslc + jnp.where(forced, FORCED_BONUS, 0.0), -jnp.inf)
    topk = min(SLC_TOPK, nsel)
    few = (t0 + tq) <= topk * SLC_BLOCK

    @pl.when(few)
    def _():
        sel_ref[...] = jnp.where(valid, 1.0, 0.0)

    @pl.when(jnp.logical_not(few))
    def _():
        rank = jnp.zeros((nsel, tq), F32)
        for mp in range(nsel):
            row = score[mp:mp + 1, :]
            before = (row > score) | ((row == score) & (mi > mp))
            rank = rank + jnp.where(before, 1.0, 0.0)
        sel_ref[...] = jnp.where(valid & (rank < float(topk)), 1.0, 0.0)


def _cmp_attn(pa, kc, vct, batch, seq, tq=256):
    ncp = seq // CMP_STRIDE
    nsel = seq // SLC_BLOCK
    g = NSA_KV_GROUPS
    nq = seq // tq
    gw = NSA_HPG * HEAD_DIM
    ovt = jnp.asarray(_cmp_slc_overlap_t(seq, ncp))
    return pl.pallas_call(
        functools.partial(_cmp_attn_kernel, tq=tq, ncp=ncp, nsel=nsel),
        grid=(batch, g, nq),
        in_specs=[
            pl.BlockSpec((tq, gw), lambda b, gi, i: (b * nq + i, gi)),
            pl.BlockSpec((None, None, ncp, HEAD_DIM), lambda b, gi, i: (b, gi, 0, 0)),
            pl.BlockSpec((None, None, HEAD_DIM, ncp), lambda b, gi, i: (b, gi, 0, 0)),
            pl.BlockSpec((nsel, ncp), lambda b, gi, i: (0, 0)),
        ],
        out_specs=[
            pl.BlockSpec((None, gw, tq), lambda b, gi, i: (b, gi, i)),
            pl.BlockSpec((None, None, nsel, tq), lambda b, gi, i: (b, gi, 0, i)),
        ],
        out_shape=[
            jax.ShapeDtypeStruct((batch, NSA_WIDTH, seq), F32),
            jax.ShapeDtypeStruct((batch, g, nsel, seq), F32),
        ],
        compiler_params=_params(("parallel", "parallel", "parallel")),
        name="nsa_cmp_attn",
    )(pa, kc, vct, ovt)


def _rope_tables(positions):
    half = ROT_DIM // 2
    inv = ROPE_THETA ** (-jnp.arange(0, ROT_DIM, 2, dtype=F32) / ROT_DIM)
    ang = positions.astype(F32)[..., None] * inv
    cos, sin = jnp.cos(ang), jnp.sin(ang)
    rest = HEAD_DIM - ROT_DIM
    b, s = positions.shape
    one = jnp.ones((b, s, rest), F32)
    zero = jnp.zeros((b, s, rest), F32)
    zh = jnp.zeros((b, s, half), F32)
    c = jnp.concatenate([cos, cos, one], axis=-1)
    s1 = jnp.concatenate([-sin, zh, zero], axis=-1)
    s2 = jnp.concatenate([zh, sin, zero], axis=-1)
    return [t.reshape(b * s, HEAD_DIM) for t in (c, s1, s2)]


def _rot(x, c, s1, s2):
    half = ROT_DIM // 2
    return x * c + pltpu.roll(x, HEAD_DIM - half, 1) * s1 + pltpu.roll(x, half, 1) * s2


def _kv_prep_kernel(xs_ref, xw_ref, c_ref, s1_ref, s2_ref, k_ref, vt_ref):
    c, s1, s2 = c_ref[...], s1_ref[...], s2_ref[...]
    for br, x_ref in enumerate((xs_ref, xw_ref)):
        for gi in range(NSA_KV_GROUPS):
            src = gi * HEAD_DIM
            dst = br * KV_WIDTH + gi * HEAD_DIM
            k_ref[:, dst:dst + HEAD_DIM] = _rot(x_ref[:, src:src + HEAD_DIM], c, s1, s2).astype(k_ref.dtype)
            vt_ref[dst:dst + HEAD_DIM, :] = x_ref[:, KV_WIDTH + src:KV_WIDTH + src + HEAD_DIM].T.astype(vt_ref.dtype)


def _kv_prep(pa, tables, batch, seq, tm=256):
    nt = seq // tm
    w = 2 * KV_WIDTH
    col = (NSA_WIDTH + 2 * KV_WIDTH) // w
    assert col * w == NSA_WIDTH + 2 * KV_WIDTH
    return pl.pallas_call(
        _kv_prep_kernel,
        grid=(batch, nt),
        in_specs=[pl.BlockSpec((tm, w), lambda b, i: (b * nt + i, col)),
                  pl.BlockSpec((tm, w), lambda b, i: (b * nt + i, col + 1))]
        + [pl.BlockSpec((tm, HEAD_DIM), lambda b, i: (b * nt + i, 0))] * 3,
        out_specs=[
            pl.BlockSpec((tm, 2 * KV_WIDTH), lambda b, i: (b * nt + i, 0)),
            pl.BlockSpec((None, 2 * KV_WIDTH, tm), lambda b, i: (b, 0, i)),
        ],
        out_shape=[
            jax.ShapeDtypeStruct((batch * seq, 2 * KV_WIDTH), BF16),
            jax.ShapeDtypeStruct((batch, 2 * KV_WIDTH, seq), BF16),
        ],
        compiler_params=_params(("parallel", "parallel")),
        name="nsa_kv_prep",
    )(pa, pa, *tables)


def _scores_t(k_ref, qt, kt, tk):
    k0 = pl.multiple_of(kt * tk, tk)
    return jnp.dot(k_ref[pl.ds(k0, tk), :], qt, preferred_element_type=F32)


def _flash_init(m_ref, l_ref, acc_ref):
    m_ref[...] = jnp.full(m_ref.shape, NEG_INF, F32)
    l_ref[...] = jnp.zeros(l_ref.shape, F32)
    acc_ref[...] = jnp.zeros(acc_ref.shape, F32)


def _flash_consume_t(s_ref, vt_ref, kt, mask, m_ref, l_ref, acc_ref, tk):
    k0 = pl.multiple_of(kt * tk, tk)
    vt = vt_ref[:, pl.ds(k0, tk)]
    mask = jnp.concatenate([mask] * NSA_HPG, axis=1)
    s = jnp.where(mask, s_ref[...], NEG_INF)
    m_old = m_ref[...]
    m_new = jnp.maximum(m_old, jnp.max(s, axis=0, keepdims=True))
    m_use = jnp.where(m_new > 0.5 * NEG_INF, m_new, 0.0)
    p = jnp.exp2(s - m_use)
    alpha = jnp.exp2(m_old - m_new)
    l_ref[...] = alpha * l_ref[...] + jnp.sum(p, axis=0, keepdims=True)
    acc_ref[...] = alpha * acc_ref[...] + jnp.dot(vt, p.astype(BF16), preferred_element_type=F32)
    m_ref[...] = m_new


def _sel_attn_kernel(q_ref, c_ref, s1_ref, s2_ref, ks_ref, kw_ref, vst_ref, vwt_ref, sel_ref, oct_ref, ng_ref, o_ref,
                     m_ref, l_ref, acc_ref, gt_ref, s_ref, *, tq, tk, tkw):
    gi = pl.program_id(1)
    i = pl.program_id(2)
    t0 = i * tq
    nj = NSA_HPG
    c, s1, s2 = c_ref[...], s1_ref[...], s2_ref[...]
    scale2 = ATTN_SCALE * math.log2(math.e)
    qt = _heads_t(q_ref, lambda x: _rot(x, c, s1, s2) * scale2)
    kpos = lax.broadcasted_iota(jnp.int32, (tk, tq), 0)
    tpos = t0 + lax.broadcasted_iota(jnp.int32, (tk, tq), 1)
    nb = tk // SLC_BLOCK

    def sel_mask(kt, diag):
        rows = sel_ref[pl.ds(pl.multiple_of(kt * nb, nb), nb), :]
        chosen = jnp.broadcast_to(rows[:, None, :], (nb, SLC_BLOCK, tq)).reshape(tk, tq) > 0.5
        return chosen & ((kt * tk + kpos) <= tpos) if diag else chosen

    kposw = lax.broadcasted_iota(jnp.int32, (tkw, tq), 0)
    tposw = t0 + lax.broadcasted_iota(jnp.int32, (tkw, tq), 1)

    def win_mask(kt):
        kp = kt * tkw + kposw
        return (kp <= tposw) & (kp > tposw - WINDOW)

    sel_st = (m_ref.at[0], l_ref.at[0], acc_ref.at[0])
    win_st = (m_ref.at[1], l_ref.at[1], acc_ref.at[1])
    _flash_init(*sel_st)
    _flash_init(*win_st)
    last = (t0 + tq - 1) // tk + 1
    lastw = (t0 + tq - 1) // tkw + 1
    first = jnp.maximum(t0 - (WINDOW - 1), 0) // tkw
    sw_ref = s_ref.at[0:tkw]

    s_ref[...] = _scores_t(ks_ref, qt, 0, tk)

    def sel_trip(kt, carry):
        s_next = _scores_t(ks_ref, qt, kt + 1, tk)
        _flash_consume_t(s_ref, vst_ref, kt, sel_mask(kt, False), *sel_st, tk)
        s_ref[...] = s_next
        return carry

    lax.fori_loop(0, last - 1, sel_trip, 0)
    s_next = _scores_t(kw_ref, qt, first, tkw)
    _flash_consume_t(s_ref, vst_ref, last - 1, sel_mask(last - 1, True), *sel_st, tk)
    sw_ref[...] = s_next

    def win_trip(kt, carry):
        s_next = _scores_t(kw_ref, qt, kt + 1, tkw)
        _flash_consume_t(sw_ref, vwt_ref, kt, win_mask(kt), *win_st, tkw)
        sw_ref[...] = s_next
        return carry

    lax.fori_loop(first, lastw - 1, win_trip, 0)
    _flash_consume_t(sw_ref, vwt_ref, lastw - 1, win_mask(lastw - 1), *win_st, tkw)
    o_slc = acc_ref[0] / l_ref[0]
    o_win = acc_ref[1] / l_ref[1]
    gt_ref[...] = jax.nn.sigmoid(ng_ref[...].T)
    for j in range(nj):
        gate = [gt_ref[pl.ds(br * NSA_HEADS + gi * nj + j, 1), :] for br in range(3)]
        ot = (gate[0] * oct_ref[j * HEAD_DIM:(j + 1) * HEAD_DIM, :] + gate[1] * o_slc[:, j * tq:(j + 1) * tq]
              + gate[2] * o_win[:, j * tq:(j + 1) * tq])
        o_ref[:, j * HEAD_DIM:(j + 1) * HEAD_DIM] = ot.T.astype(o_ref.dtype)


def _sel_attn(pa, tables, kn, vt, sel, o_cmp_t, png, batch, seq, tq=256, tk=512, tkw=256):
    g = NSA_KV_GROUPS
    nq = seq // tq
    gw = NSA_HPG * HEAD_DIM
    tk = min(tk, seq)
    assert tk % tq == 0 and seq % tk == 0
    nsel = seq // SLC_BLOCK
    n = NSA_HPG * tq
    tab = pl.BlockSpec((tq, HEAD_DIM), lambda b, gi, i: (b * nq + i, 0))
    return pl.pallas_call(
        functools.partial(_sel_attn_kernel, tq=tq, tk=tk, tkw=min(tkw, tk)),
        grid=(batch, g, nq),
        in_specs=[
            pl.BlockSpec((tq, gw), lambda b, gi, i: (b * nq + i, gi)),
            tab, tab, tab,
            pl.BlockSpec((seq, HEAD_DIM), lambda b, gi, i: (b, gi)),
            pl.BlockSpec((seq, HEAD_DIM), lambda b, gi, i: (b, g + gi)),
            pl.BlockSpec((None, HEAD_DIM, seq), lambda b, gi, i: (b, gi, 0)),
            pl.BlockSpec((None, HEAD_DIM, seq), lambda b, gi, i: (b, g + gi, 0)),
            pl.BlockSpec((None, None, nsel, tq), lambda b, gi, i: (b, gi, 0, i)),
            pl.BlockSpec((None, gw, tq), lambda b, gi, i: (b, gi, i)),
            pl.BlockSpec((tq, LANES), lambda b, gi, i: (b * nq + i, 0)),
        ],
        out_specs=pl.BlockSpec((tq, gw), lambda b, gi, i: (b * nq + i, gi)),
        out_shape=jax.ShapeDtypeStruct((batch * seq, NSA_WIDTH), BF16),
        scratch_shapes=[
            pltpu.VMEM((2, 1, n), F32),
            pltpu.VMEM((2, 1, n), F32),
            pltpu.VMEM((2, HEAD_DIM, n), F32),
            pltpu.VMEM((LANES, tq), F32),
            pltpu.VMEM((tk, n), F32),
        ],
        compiler_params=_params(("parallel", "parallel", "parallel")),
        name="nsa_sel_win_attn",
    )(pa, *tables, kn, kn, vt, vt, sel, o_cmp_t, png)


HG_TILE = 128
HG_GROUP = 8
HG_LEVELS = (64, 32, 16, 8)
HG_HEADS_PER_STEP = 8


def _split3(x):
    hi = x.astype(BF16)
    r = x - hi.astype(F32)
    mid = r.astype(BF16)
    lo = (r - mid.astype(F32)).astype(BF16)
    return hi, mid, lo


def _group_row(x, size, row):
    t, d = x.shape
    x3 = x.reshape(t // size, size, d)
    return jnp.broadcast_to(x3[:, row:row + 1, :], x3.shape).reshape(t, d)


def _hgrn_heads(qs, zs, vs, gs, lbs, nw, low, sts):
    t = HG_TILE
    nh = len(qs)
    heads = range(nh)
    log2e = math.log2(math.e)
    kks, l2fs = [], []
    for i in heads:
        z, lb = zs[i], lbs[i]
        ez = jnp.exp(-jnp.abs(z))
        r = 1.0 / (1.0 + ez)
        pos = z >= 0.0
        f = lb + (1.0 - lb) * jnp.where(pos, r, ez * r)
        kks.append((1.0 - lb) * jnp.where(pos, ez * r, r))
        l2fs.append(jnp.where(f > 0.0, jnp.log2(f), jnp.minimum(z, 0.0) * log2e))
    bs_ = []
    for i in heads:
        hi, mid, lo = _split3(l2fs[i])
        bs_.append(jnp.dot(low, hi, preferred_element_type=F32) + jnp.dot(low, mid, preferred_element_type=F32)
                   + jnp.dot(low, lo, preferred_element_type=F32))
    ti = lax.broadcasted_iota(jnp.int32, (t, t), 0)
    si = lax.broadcasted_iota(jnp.int32, (t, t), 1)
    atts = [jnp.zeros((t, t), F32) for _ in heads]
    for h in HG_LEVELS:
        pair = ((ti // (2 * h)) == (si // (2 * h))) & ((ti % (2 * h)) >= h) & ((si % (2 * h)) < h)
        for i in heads:
            b = bs_[i]
            bref = _group_row(b, 2 * h, h - 1)
            e = jnp.exp2(-jnp.abs(b - bref))
            atts[i] = atts[i] + jnp.where(pair, _nt_dot((qs[i] * e).astype(BF16), (kks[i] * e).astype(BF16)), 0.0)
    lane_d = lax.broadcasted_iota(jnp.int32, (HGRN_DK, t), 1)
    diags = [jnp.zeros((t, t), F32) for _ in heads]
    for sg in range(HG_GROUP):
        spread = jnp.where((lane_d % HG_GROUP) == sg, 1.0, 0.0).astype(BF16)
        for i in heads:
            bsg = _group_row(bs_[i], HG_GROUP, sg)
            ksg = _group_row(kks[i], HG_GROUP, sg)
            p = (qs[i] * jnp.exp2(jnp.minimum(bs_[i] - bsg, 0.0))) * ksg
            diags[i] = diags[i] + jnp.dot(p.astype(BF16), spread, preferred_element_type=F32)
    same = ((ti // HG_GROUP) == (si // HG_GROUP)) & (si <= ti)
    ys, st_new = [], []
    for i in heads:
        b = bs_[i]
        att = atts[i] + jnp.where(same, diags[i], 0.0)
        vb = vs[i].astype(BF16)
        o = jnp.dot(att.astype(BF16), vb, preferred_element_type=F32)
        o = o + _nt_dot((qs[i] * jnp.exp2(b)).astype(BF16), sts[i].astype(BF16))
        b_last = b[t - 1:t, :]
        kdec = (kks[i] * jnp.exp2(b_last - b)).astype(BF16)
        st_new.append(sts[i] * jnp.exp2(b_last) + _tn_dot(vb, kdec))
        y = o * lax.rsqrt(jnp.mean(o * o, axis=-1, keepdims=True) + NORM_EPS) * nw
        g = gs[i]
        ys.append(y * (g * jax.nn.sigmoid(g)))
    return ys, st_new


def _hgrn_kernel(q_ref, z_ref, v_ref, g_ref, lb_ref, nw_ref, low_ref, o_ref, st_ref, *, nh):
    @pl.when(pl.program_id(2) == 0)
    def _():
        st_ref[...] = jnp.zeros_like(st_ref)

    cols = [slice(hd * HGRN_DK, (hd + 1) * HGRN_DK) for hd in range(nh)]
    load = lambda ref: [ref[:, c].astype(F32) for c in cols]
    ys, st_new = _hgrn_heads(load(q_ref), load(z_ref), load(v_ref), load(g_ref), [lb_ref[hd:hd + 1, :] for hd in range(nh)],
                             nw_ref[...], low_ref[...], [st_ref[hd] for hd in range(nh)])
    for hd in range(nh):
        st_ref[hd] = st_new[hd]
        o_ref[:, cols[hd]] = ys[hd].astype(o_ref.dtype)


def _hgrn(pb, lb, norm_w, batch, seq):
    hh = HGRN_HEADS
    nh = HG_HEADS_PER_STEP
    nt = seq // HG_TILE
    low = jnp.asarray(np.tril(np.ones((HG_TILE, HG_TILE), np.float32)), dtype=BF16)
    spec = lambda part: pl.BlockSpec((HG_TILE, nh * HGRN_DK), lambda b, h, i: (b * nt + i, part * (hh // nh) + h))
    return pl.pallas_call(
        functools.partial(_hgrn_kernel, nh=nh),
        grid=(batch, hh // nh, nt),
        in_specs=[
            spec(0), spec(1), spec(2), spec(3),
            pl.BlockSpec((None, nh, HGRN_DK), lambda b, h, i: (h, 0, 0)),
            pl.BlockSpec((1, HGRN_DV), lambda b, h, i: (0, 0)),
            pl.BlockSpec(low.shape, lambda b, h, i: (0, 0)),
        ],
        out_specs=pl.BlockSpec((HG_TILE, nh * HGRN_DV), lambda b, h, i: (b * nt + i, h)),
        out_shape=jax.ShapeDtypeStruct((batch * seq, HGRN_VW), BF16),
        scratch_shapes=[pltpu.VMEM((nh, HGRN_DV, HGRN_DK), F32)],
        compiler_params=_params(("parallel", "parallel", "arbitrary")),
        name="hgrn2",
    )(pb, pb, pb, pb, lb.reshape(hh // nh, nh, HGRN_DK), norm_w.reshape(1, HGRN_DV), low)


def _hybrid_mixer(h, tables, layer, w_in_t, cmp_pos, cmp_w1, cmp_w2, lb, g_norm_w, w_up_a, w_up_b, batch, seq):
    pa = _matmul_t(h, w_in_t, layer, 0, SEG_A, F32)
    png = _matmul_t(h, w_in_t, layer, SEG_A, LANES, F32, tn=LANES)
    pb = _matmul_t(h, w_in_t, layer, SEG_A + SEG_G, SEG_B, F32)
    kc, vct = _compress(pa, cmp_pos, cmp_w1, cmp_w2, batch, seq)
    o_cmp_t, sel = _cmp_attn(pa, kc, vct, batch, seq)
    kn, vt = _kv_prep(pa, tables, batch, seq)
    o_nsa = _sel_attn(pa, tables, kn, vt, sel, o_cmp_t, png, batch, seq)
    o_hgrn = _hgrn(pb, lb, g_norm_w, batch, seq)
    return _matmul_merge(o_nsa, o_hgrn, w_up_a, w_up_b, layer, pb, 2 * HGRN_KW + 2 * HGRN_VW,
                         2 * HGRN_KW + 2 * HGRN_VW + D_MODEL)


def kernel(x, c, positions, ada_w, ada_b, norm_mix_w, w_in, nsa_cmp_pos, nsa_cmp_w1, nsa_cmp_w2, hgrn_lb_logits,
           hgrn_norm_w, w_up_a, w_up_b, w_out, norm_mlp_w, w_mlp1, w_mlp2, final_norm_w):
    batch, seq, d = x.shape
    depth = ada_w.shape[0]
    lb_all = jnp.cumsum(jax.nn.softmax(hgrn_lb_logits.astype(F32), axis=0), axis=0)
    lb_all = lb_all - lb_all[0:1]
    mod = _ada_mod(c, ada_w, ada_b)
    tables = _rope_tables(positions)
    x2 = x.reshape(batch * seq, d)
    zeros = jnp.zeros((batch, d), F32)
    w_in_t = jnp.transpose(w_in, (0, 2, 1))
    for l in range(depth):
        sh1, sc1, g1, sh2, sc2, g2 = [mod[l][:, k * d:(k + 1) * d] for k in range(6)]
        h = _norm_mod(x2, norm_mix_w[l], sc1, sh1, seq, BF16)
        y = _hybrid_mixer(h, tables, l, w_in_t, nsa_cmp_pos[l], nsa_cmp_w1[l], nsa_cmp_w2[l], lb_all[l],
                          hgrn_norm_w[l], w_up_a, w_up_b, batch, seq)
        x2 = _matmul_resid(y, w_out, l, x2, g1, seq)
        h = _norm_mod(x2, norm_mlp_w[l], sc2, sh2, seq, BF16)
        u = _matmul(h, w_mlp1, l, BF16, relu2=True)
        x2 = _matmul_resid_ksplit(u, w_mlp2, l, x2, g2, seq)
    out = _norm_mod(x2, final_norm_w, zeros, zeros, seq, F32)
    return out.reshape(batch, seq, d)
```

```python
import functools
import math

import numpy as np
import jax
import jax.numpy as jnp
from jax import lax
from jax.experimental import pallas as pl
from jax.experimental.pallas import tpu as pltpu

D_MODEL = 4096
DEPTH = 2
NSA_HEADS = 16
NSA_KV_GROUPS = 4
NSA_HPG = NSA_HEADS // NSA_KV_GROUPS
HEAD_DIM = 128
CMP_LEN = 32
CMP_STRIDE = 16
SLC_BLOCK = 64
SLC_TOPK = 16
WINDOW = 512
ATTN_SCALE = HEAD_DIM ** -0.5
FORCED_BONUS = 1e6
NEG_INF = -1e30
HGRN_HEADS = 16
HGRN_DK = 128
HGRN_DV = 128
ROPE_THETA = 500000.0
ROT_DIM = HEAD_DIM // 4
D_FF = 4 * D_MODEL
NORM_EPS = 1e-6

NSA_WIDTH = NSA_HEADS * HEAD_DIM
KV_WIDTH = NSA_KV_GROUPS * HEAD_DIM
HGRN_KW = HGRN_HEADS * HGRN_DK
HGRN_VW = HGRN_HEADS * HGRN_DV
SEG_A = NSA_WIDTH + 6 * KV_WIDTH
SEG_G = 3 * NSA_HEADS
SEG_B = 2 * HGRN_KW + 2 * HGRN_VW + 2 * D_MODEL

LANES = 128
VMEM_LIMIT = 58 * 1024 * 1024

BF16 = jnp.bfloat16
F32 = jnp.float32


def _params(sem):
    return pltpu.CompilerParams(dimension_semantics=sem, vmem_limit_bytes=VMEM_LIMIT)


def _nt_dot(a, b, precision=None):
    return lax.dot_general(a, b, (((1,), (1,)), ((), ())), preferred_element_type=F32, precision=precision)


def _tn_dot(a, b, precision=None):
    return lax.dot_general(a, b, (((0,), (0,)), ((), ())), preferred_element_type=F32, precision=precision)


def _ada_kernel(cb_ref, w_ref, b_ref, o_ref, cact_ref, *, batch, tn):
    @pl.when((pl.program_id(0) == 0) & (pl.program_id(1) == 0))
    def _():
        cb = cb_ref[...]
        cact_ref[...] = cb * jax.nn.sigmoid(cb)

    for b in range(batch):
        cact = cact_ref[b]
        rows = []
        for s in range(tn // LANES):
            w = w_ref[:, s * LANES:(s + 1) * LANES]
            rows.append(jnp.sum(w * cact, axis=0, keepdims=True))
        o_ref[b:b + 1, :] = jnp.concatenate(rows, axis=1) + b_ref[...]


def _ada_mod(c, ada_w, ada_b):
    depth, k, n = ada_w.shape
    batch = c.shape[0]
    tn = 512
    cb = jnp.broadcast_to(c[:, :, None], (batch, k, LANES))
    out = pl.pallas_call(
        functools.partial(_ada_kernel, batch=batch, tn=tn),
        grid=(depth, n // tn),
        in_specs=[
            pl.BlockSpec((batch, k, LANES), lambda l, j: (0, 0, 0)),
            pl.BlockSpec((None, k, tn), lambda l, j: (l, 0, j)),
            pl.BlockSpec((None, 1, tn), lambda l, j: (l, 0, j)),
        ],
        out_specs=pl.BlockSpec((None, batch, tn), lambda l, j: (l, 0, j)),
        out_shape=jax.ShapeDtypeStruct((depth, batch, n), F32),
        scratch_shapes=[pltpu.VMEM((batch, k, LANES), F32)],
        compiler_params=_params(("arbitrary", "arbitrary")),
        name="ada_mod",
    )(cb, ada_w, ada_b.reshape(depth, 1, n))
    return out


def _norm_kernel(x_ref, w_ref, sc_ref, sh_ref, o_ref):
    x = x_ref[...]
    y = x * lax.rsqrt(jnp.mean(x * x, axis=-1, keepdims=True) + NORM_EPS)
    y = y * w_ref[...]
    y = y * (1.0 + sc_ref[...]) + sh_ref[...]
    o_ref[...] = y.astype(o_ref.dtype)


def _norm_mod(x2, w, sc, sh, seq, out_dtype):
    m, d = x2.shape
    batch = m // seq
    tm = 512
    per = seq // tm
    return pl.pallas_call(
        _norm_kernel,
        grid=(m // tm,),
        in_specs=[
            pl.BlockSpec((tm, d), lambda i: (i, 0)),
            pl.BlockSpec((1, d), lambda i: (0, 0)),
            pl.BlockSpec((None, 1, d), lambda i: (i // per, 0, 0)),
            pl.BlockSpec((None, 1, d), lambda i: (i // per, 0, 0)),
        ],
        out_specs=pl.BlockSpec((tm, d), lambda i: (i, 0)),
        out_shape=jax.ShapeDtypeStruct((m, d), out_dtype),
        compiler_params=_params(("parallel",)),
        name="norm_mod",
    )(x2, w.reshape(1, d), sc.reshape(batch, 1, d), sh.reshape(batch, 1, d))


MM_TM = 1024
MM_TN = 512
MM_TK = 4096
MM_TM_WIDE = 512
MM_TN_WIDE = 1024


def _mm_kernel(a_ref, w_ref, o_ref, wb_ref, *, relu2):
    @pl.when(pl.program_id(1) == 0)
    def _():
        wb_ref[...] = w_ref[...].astype(BF16)

    r = jnp.dot(a_ref[...], wb_ref[...], preferred_element_type=F32)
    if relu2:
        r = jnp.square(jnp.maximum(r, 0.0))
    o_ref[...] = r.astype(o_ref.dtype)


def _matmul(a, w, layer, out_dtype, relu2=False, tm=MM_TM_WIDE, tn=MM_TN_WIDE):
    m, kd = a.shape
    n = w.shape[2]
    tm = min(tm, m)
    assert n % tn == 0 and m % tm == 0
    return pl.pallas_call(
        functools.partial(_mm_kernel, relu2=relu2),
        grid=(n // tn, m // tm),
        in_specs=[
            pl.BlockSpec((tm, kd), lambda j, i: (i, 0)),
            pl.BlockSpec((None, kd, tn), lambda j, i: (layer, 0, j)),
        ],
        out_specs=pl.BlockSpec((tm, tn), lambda j, i: (i, j)),
        out_shape=jax.ShapeDtypeStruct((m, n), out_dtype),
        scratch_shapes=[pltpu.VMEM((kd, tn), BF16)],
        compiler_params=_params(("parallel", "arbitrary")),
        name="matmul",
    )(a, w)


def _mm_t_kernel(*refs, shift):
    if shift:
        a_ref, w_ref, wn_ref, o_ref, wb_ref = refs
    else:
        a_ref, w_ref, o_ref, wb_ref = refs

    @pl.when(pl.program_id(1) == 0)
    def _():
        tn = w_ref.shape[0]
        wb_ref[0:tn - shift, :] = w_ref[shift:tn, :].astype(BF16)
        if shift:
            wb_ref[tn - shift:tn, :] = wn_ref[0:shift, :].astype(BF16)

    o_ref[...] = _nt_dot(a_ref[...], wb_ref[...]).astype(o_ref.dtype)


BF16_SUBLANES = 16


def _matmul_t(a, wt, layer, row0, nrows, out_dtype, tm=MM_TM_WIDE, tn=MM_TN_WIDE):
    m, kd = a.shape
    tm = min(tm, m)
    shift = row0 % tn
    base = row0 - shift
    nxt = 64
    assert nrows % tn == 0 and m % tm == 0 and shift % BF16_SUBLANES == 0 and shift <= nxt and tn % nxt == 0
    in_specs = [
        pl.BlockSpec((tm, kd), lambda j, i: (i, 0)),
        pl.BlockSpec((None, tn, kd), lambda j, i: (layer, base // tn + j, 0)),
    ]
    args = [a, wt]
    if shift:
        in_specs.append(pl.BlockSpec((None, nxt, kd), lambda j, i: (layer, (base + tn * (j + 1)) // nxt, 0)))
        args.append(wt)
    return pl.pallas_call(
        functools.partial(_mm_t_kernel, shift=shift),
        grid=(nrows // tn, m // tm),
        in_specs=in_specs,
        out_specs=pl.BlockSpec((tm, tn), lambda j, i: (i, j)),
        out_shape=jax.ShapeDtypeStruct((m, nrows), out_dtype),
        scratch_shapes=[pltpu.VMEM((tn, kd), BF16)],
        compiler_params=_params(("parallel", "arbitrary")),
        name="matmul_t",
    )(*args)


def _mm_resid_kernel(a_ref, w_ref, x_ref, g_ref, o_ref, wb_ref):
    @pl.when(pl.program_id(1) == 0)
    def _():
        wb_ref[...] = w_ref[...].astype(BF16)

    o_ref[...] = x_ref[...] + g_ref[...] * jnp.dot(a_ref[...], wb_ref[...], preferred_element_type=F32)


def _matmul_resid(a, w, layer, x2, gate, seq, tn=MM_TN, tk=None, ks=0):
    m = a.shape[0]
    kd = tk if tk is not None else a.shape[1]
    n = w.shape[2]
    tm = min(MM_TM, seq)
    per = seq // tm
    batch = m // seq
    return pl.pallas_call(
        _mm_resid_kernel,
        grid=(n // tn, m // tm),
        in_specs=[
            pl.BlockSpec((tm, kd), lambda j, i: (i, ks)),
            pl.BlockSpec((None, kd, tn), lambda j, i: (layer, ks, j)),
            pl.BlockSpec((tm, tn), lambda j, i: (i, j)),
            pl.BlockSpec((None, 1, tn), lambda j, i: (i // per, 0, j)),
        ],
        out_specs=pl.BlockSpec((tm, tn), lambda j, i: (i, j)),
        out_shape=jax.ShapeDtypeStruct((m, n), F32),
        scratch_shapes=[pltpu.VMEM((kd, tn), BF16)],
        compiler_params=_params(("parallel", "arbitrary")),
        name="matmul_resid",
    )(a, w, x2, gate.reshape(batch, 1, n))


def _matmul_resid_ksplit(a, w, layer, x2, gate, seq, tk=MM_TK):
    kd = a.shape[1]
    for ks in range(kd // tk):
        x2 = _matmul_resid(a, w, layer, x2, gate, seq, tk=tk, ks=ks)
    return x2


def _mm_merge_kernel(a_ref, b_ref, wa_ref, wb_ref, ga_ref, gb_ref, o_ref, wab_ref, wbb_ref):
    @pl.when(pl.program_id(1) == 0)
    def _():
        wab_ref[...] = wa_ref[...].astype(BF16)
        wbb_ref[...] = wb_ref[...].astype(BF16)

    ya = jnp.dot(a_ref[...], wab_ref[...], preferred_element_type=F32)
    yb = jnp.dot(b_ref[...], wbb_ref[...], preferred_element_type=F32)
    ga, gb = ga_ref[...].astype(F32), gb_ref[...].astype(F32)
    o_ref[...] = (jax.nn.sigmoid(ga) * ya + jax.nn.sigmoid(gb) * yb).astype(o_ref.dtype)


def _matmul_merge(a, b, wa, wb, layer, pb, ga_col, gb_col, tn=MM_TN):
    m, kd = a.shape
    n = wa.shape[2]
    tm = min(MM_TM, m)
    return pl.pallas_call(
        _mm_merge_kernel,
        grid=(n // tn, m // tm),
        in_specs=[
            pl.BlockSpec((tm, kd), lambda j, i: (i, 0)),
            pl.BlockSpec((tm, kd), lambda j, i: (i, 0)),
            pl.BlockSpec((None, kd, tn), lambda j, i: (layer, 0, j)),
            pl.BlockSpec((None, kd, tn), lambda j, i: (layer, 0, j)),
            pl.BlockSpec((tm, tn), lambda j, i: (i, ga_col // tn + j)),
            pl.BlockSpec((tm, tn), lambda j, i: (i, gb_col // tn + j)),
        ],
        out_specs=pl.BlockSpec((tm, tn), lambda j, i: (i, j)),
        out_shape=jax.ShapeDtypeStruct((m, n), BF16),
        scratch_shapes=[pltpu.VMEM((kd, tn), BF16), pltpu.VMEM((kd, tn), BF16)],
        compiler_params=_params(("parallel", "arbitrary")),
        name="matmul_merge",
    )(a, b, wa, wb, pb, pb)


def _gelu_tanh(x):
    c = math.sqrt(2.0 / math.pi)
    return 0.5 * x * (1.0 + jnp.tanh(c * (x + 0.044715 * (x * x * x))))


def _split2(x):
    hi = x.astype(BF16)
    return hi, (x - hi.astype(F32)).astype(BF16)


def _dot3(a, b):
    ah, al = _split2(a)
    bh, bl = _split2(b)
    dot = functools.partial(jnp.dot, preferred_element_type=F32)
    return dot(ah, bh) + (dot(ah, bl) + dot(al, bh))


def _compress_one(x_ref, pos_ref, w1_ref, w2_ref, kv, nhalf):
    half = CMP_LEN // 2
    xs = [x_ref[pl.ds(l, nhalf, stride=CMP_STRIDE), :] for l in range(half)]
    lhs_a = jnp.concatenate([xs[l] + pos_ref[kv, l:l + 1, :] for l in range(half)], axis=1)
    lhs_b = jnp.concatenate([xs[l] + pos_ref[kv, half + l:half + l + 1, :] for l in range(half)], axis=1)
    w1 = w1_ref[kv]
    acc_a = _dot3(lhs_a, w1[:half].reshape(half * HEAD_DIM, HEAD_DIM))
    acc_b = _dot3(lhs_b, w1[half:].reshape(half * HEAD_DIM, HEAD_DIM))
    hid = acc_a + pltpu.roll(acc_b, nhalf - 1, 0)
    hid = _gelu_tanh(hid)
    return _dot3(hid, w2_ref[kv])


def _compress_kernel(xk_ref, xv_ref, pos_ref, w1_ref, w2_ref, kc_ref, vct_ref, *, nhalf):
    kc_ref[...] = _compress_one(xk_ref, pos_ref, w1_ref, w2_ref, 0, nhalf)
    vct_ref[...] = _compress_one(xv_ref, pos_ref, w1_ref, w2_ref, 1, nhalf).T


def _compress(pa, pos, w1, w2, batch, seq):
    nhalf = seq // CMP_STRIDE
    g = NSA_KV_GROUPS
    col0 = NSA_WIDTH // HEAD_DIM
    full = lambda shape: pl.BlockSpec(shape, lambda b, gi: (0,) * len(shape))
    return pl.pallas_call(
        functools.partial(_compress_kernel, nhalf=nhalf),
        grid=(batch, g),
        in_specs=[
            pl.BlockSpec((seq, HEAD_DIM), lambda b, gi: (b, col0 + gi)),
            pl.BlockSpec((seq, HEAD_DIM), lambda b, gi: (b, col0 + g + gi)),
            full(pos.shape), full(w1.shape), full(w2.shape),
        ],
        out_specs=[
            pl.BlockSpec((None, None, nhalf, HEAD_DIM), lambda b, gi: (b, gi, 0, 0)),
            pl.BlockSpec((None, None, HEAD_DIM, nhalf), lambda b, gi: (b, gi, 0, 0)),
        ],
        out_shape=[
            jax.ShapeDtypeStruct((batch, g, nhalf, HEAD_DIM), F32),
            jax.ShapeDtypeStruct((batch, g, HEAD_DIM, nhalf), F32),
        ],
        compiler_params=_params(("parallel", "parallel")),
        name="nsa_compress",
    )(pa, pa, pos, w1, w2)


def _cmp_slc_overlap_t(seq, ncp):
    nc = (seq - CMP_LEN) // CMP_STRIDE + 1
    nsel = seq // SLC_BLOCK
    cs = np.arange(nc) * CMP_STRIDE
    ce = cs + CMP_LEN - 1
    ss = np.arange(nsel) * SLC_BLOCK
    se = ss + SLC_BLOCK - 1
    ov = np.minimum(ce[:, None], se[None, :]) - np.maximum(cs[:, None], ss[None, :]) + 1
    ov = np.maximum(ov, 0).astype(np.float32)
    out = np.zeros((nsel, ncp), np.float32)
    out[:, :nc] = ov.T
    return out


def _heads_t(x_ref, fn=None):
    parts = []
    for j in range(NSA_HPG):
        x = x_ref[:, j * HEAD_DIM:(j + 1) * HEAD_DIM]
        if fn is not None:
            x = fn(x)
        parts.append(x.T.astype(BF16))
    return jnp.concatenate(parts, axis=1)


def _cmp_attn_kernel(q_ref, kc_ref, vct_ref, ovt_ref, o_ref, sel_ref, *, tq, ncp, nsel):
    i = pl.program_id(2)
    t0 = i * tq
    tpos = t0 + lax.broadcasted_iota(jnp.int32, (ncp, tq), 1)
    nidx = lax.broadcasted_iota(jnp.int32, (ncp, tq), 0)
    vis = (nidx * CMP_STRIDE + (CMP_LEN - 1)) <= tpos
    qt = _heads_t(q_ref)
    st = jnp.dot(kc_ref[...].astype(BF16), qt, preferred_element_type=F32) * ATTN_SCALE
    vct = vct_ref[...].astype(BF16)
    imp = jnp.zeros((ncp, tq), F32)
    for j in range(NSA_HPG):
        s = jnp.where(vis, st[:, j * tq:(j + 1) * tq], NEG_INF)
        e = jnp.exp(s - jnp.max(s, axis=0, keepdims=True))
        p = e / jnp.sum(e, axis=0, keepdims=True)
        p = jnp.where(vis, p, 0.0)
        o_ref[j * HEAD_DIM:(j + 1) * HEAD_DIM, :] = jnp.dot(vct, p.astype(BF16), preferred_element_type=F32)
        imp = imp + p
    pslc = jnp.dot(ovt_ref[...], imp, precision=lax.Precision.HIGHEST, preferred_element_type=F32)
    tl = t0 + lax.broadcasted_iota(jnp.int32, (nsel, tq), 1)
    mi = lax.broadcasted_iota(jnp.int32, (nsel, tq), 0)
    cur = tl // SLC_BLOCK
    valid = mi * SLC_BLOCK <= tl
    forced = (mi == 0) | (mi == cur) | (mi == cur - 1)
    score = jnp.where(valid, pslc + jnp.where(forced, FORCED_BONUS, 0.0), -jnp.inf)
    topk = min(SLC_TOPK, nsel)
    few = (t0 + tq) <= topk * SLC_BLOCK

    @pl.when(few)
    def _():
        sel_ref[...] = jnp.where(valid, 1.0, 0.0)

    @pl.when(jnp.logical_not(few))
    def _():
        rank = jnp.zeros((nsel, tq), F32)
        for mp in range(nsel):
            row = score[mp:mp + 1, :]
            before = (row > score) | ((row == score) & (mi > mp))
            rank = rank + jnp.where(before, 1.0, 0.0)
        sel_ref[...] = jnp.where(valid & (rank < float(topk)), 1.0, 0.0)


def _cmp_attn(pa, kc, vct, batch, seq, tq=1024):
    ncp = seq // CMP_STRIDE
    nsel = seq // SLC_BLOCK
    g = NSA_KV_GROUPS
    nq = seq // tq
    gw = NSA_HPG * HEAD_DIM
    ovt = jnp.asarray(_cmp_slc_overlap_t(seq, ncp))
    return pl.pallas_call(
        functools.partial(_cmp_attn_kernel, tq=tq, ncp=ncp, nsel=nsel),
        grid=(batch, g, nq),
        in_specs=[
            pl.BlockSpec((tq, gw), lambda b, gi, i: (b * nq + i, gi)),
            pl.BlockSpec((None, None, ncp, HEAD_DIM), lambda b, gi, i: (b, gi, 0, 0)),
            pl.BlockSpec((None, None, HEAD_DIM, ncp), lambda b, gi, i: (b, gi, 0, 0)),
            pl.BlockSpec((nsel, ncp), lambda b, gi, i: (0, 0)),
        ],
        out_specs=[
            pl.BlockSpec((None, gw, tq), lambda b, gi, i: (b, gi, i)),
            pl.BlockSpec((None, None, nsel, tq), lambda b, gi, i: (b, gi, 0, i)),
        ],
        out_shape=[
            jax.ShapeDtypeStruct((batch, NSA_WIDTH, seq), F32),
            jax.ShapeDtypeStruct((batch, g, nsel, seq), F32),
        ],
        compiler_params=_params(("parallel", "parallel", "parallel")),
        name="nsa_cmp_attn",
    )(pa, kc, vct, ovt)


def _rope_tables(positions):
    half = ROT_DIM // 2
    inv = ROPE_THETA ** (-jnp.arange(0, ROT_DIM, 2, dtype=F32) / ROT_DIM)
    ang = positions.astype(F32)[..., None] * inv
    cos, sin = jnp.cos(ang), jnp.sin(ang)
    rest = HEAD_DIM - ROT_DIM
    b, s = positions.shape
    one = jnp.ones((b, s, rest), F32)
    zero = jnp.zeros((b, s, rest), F32)
    zh = jnp.zeros((b, s, half), F32)
    c = jnp.concatenate([cos, cos, one], axis=-1)
    s1 = jnp.concatenate([-sin, zh, zero], axis=-1)
    s2 = jnp.concatenate([zh, sin, zero], axis=-1)
    return [t.reshape(b * s, HEAD_DIM) for t in (c, s1, s2)]


def _rot(x, c, s1, s2):
    half = ROT_DIM // 2
    return x * c + pltpu.roll(x, HEAD_DIM - half, 1) * s1 + pltpu.roll(x, half, 1) * s2


def _kv_prep_kernel(xs_ref, xw_ref, c_ref, s1_ref, s2_ref, k_ref, vt_ref):
    c, s1, s2 = c_ref[...], s1_ref[...], s2_ref[...]
    for br, x_ref in enumerate((xs_ref, xw_ref)):
        for gi in range(NSA_KV_GROUPS):
            src = gi * HEAD_DIM
            dst = br * KV_WIDTH + gi * HEAD_DIM
            k_ref[:, dst:dst + HEAD_DIM] = _rot(x_ref[:, src:src + HEAD_DIM], c, s1, s2).astype(k_ref.dtype)
            vt_ref[dst:dst + HEAD_DIM, :] = x_ref[:, KV_WIDTH + src:KV_WIDTH + src + HEAD_DIM].T.astype(vt_ref.dtype)


def _kv_prep(pa, tables, batch, seq, tm=256):
    nt = seq // tm
    w = 2 * KV_WIDTH
    col = (NSA_WIDTH + 2 * KV_WIDTH) // w
    assert col * w == NSA_WIDTH + 2 * KV_WIDTH
    return pl.pallas_call(
        _kv_prep_kernel,
        grid=(batch, nt),
        in_specs=[pl.BlockSpec((tm, w), lambda b, i: (b * nt + i, col)),
                  pl.BlockSpec((tm, w), lambda b, i: (b * nt + i, col + 1))]
        + [pl.BlockSpec((tm, HEAD_DIM), lambda b, i: (b * nt + i, 0))] * 3,
        out_specs=[
            pl.BlockSpec((tm, 2 * KV_WIDTH), lambda b, i: (b * nt + i, 0)),
            pl.BlockSpec((None, 2 * KV_WIDTH, tm), lambda b, i: (b, 0, i)),
        ],
        out_shape=[
            jax.ShapeDtypeStruct((batch * seq, 2 * KV_WIDTH), BF16),
            jax.ShapeDtypeStruct((batch, 2 * KV_WIDTH, seq), BF16),
        ],
        compiler_params=_params(("parallel", "parallel")),
        name="nsa_kv_prep",
    )(pa, pa, *tables)


def _scores_t(k_ref, qt, kt, tk):
    k0 = pl.multiple_of(kt * tk, tk)
    return jnp.dot(k_ref[pl.ds(k0, tk), :], qt, preferred_element_type=F32)


def _flash_init(m_ref, l_ref, acc_ref):
    m_ref[...] = jnp.full(m_ref.shape, NEG_INF, F32)
    l_ref[...] = jnp.zeros(l_ref.shape, F32)
    acc_ref[...] = jnp.zeros(acc_ref.shape, F32)


def _flash_consume_t(s_ref, vt_ref, kt, mask, m_ref, l_ref, acc_ref, tk):
    k0 = pl.multiple_of(kt * tk, tk)
    vt = vt_ref[:, pl.ds(k0, tk)]
    mask = jnp.concatenate([mask] * NSA_HPG, axis=1)
    s = jnp.where(mask, s_ref[...], NEG_INF)
    m_old = m_ref[...]
    m_new = jnp.maximum(m_old, jnp.max(s, axis=0, keepdims=True))
    m_use = jnp.where(m_new > 0.5 * NEG_INF, m_new, 0.0)
    p = jnp.exp2(s - m_use)
    alpha = jnp.exp2(m_old - m_new)
    l_ref[...] = alpha * l_ref[...] + jnp.sum(p, axis=0, keepdims=True)
    acc_ref[...] = alpha * acc_ref[...] + jnp.dot(vt, p.astype(BF16), preferred_element_type=F32)
    m_ref[...] = m_new


def _sel_attn_kernel(q_ref, c_ref, s1_ref, s2_ref, ks_ref, kw_ref, vst_ref, vwt_ref, sel_ref, oct_ref, ng_ref, o_ref,
                     m_ref, l_ref, acc_ref, gt_ref, s_ref, *, tq, tk, tkw):
    gi = pl.program_id(1)
    i = pl.program_id(2)
    t0 = i * tq
    nj = NSA_HPG
    c, s1, s2 = c_ref[...], s1_ref[...], s2_ref[...]
    scale2 = ATTN_SCALE * math.log2(math.e)
    qt = _heads_t(q_ref, lambda x: _rot(x, c, s1, s2) * scale2)
    kpos = lax.broadcasted_iota(jnp.int32, (tk, tq), 0)
    tpos = t0 + lax.broadcasted_iota(jnp.int32, (tk, tq), 1)
    nb = tk // SLC_BLOCK

    def sel_mask(kt, diag):
        rows = sel_ref[pl.ds(pl.multiple_of(kt * nb, nb), nb), :]
        chosen = jnp.broadcast_to(rows[:, None, :], (nb, SLC_BLOCK, tq)).reshape(tk, tq) > 0.5
        return chosen & ((kt * tk + kpos) <= tpos) if diag else chosen

    kposw = lax.broadcasted_iota(jnp.int32, (tkw, tq), 0)
    tposw = t0 + lax.broadcasted_iota(jnp.int32, (tkw, tq), 1)

    def win_mask(kt):
        kp = kt * tkw + kposw
        return (kp <= tposw) & (kp > tposw - WINDOW)

    sel_st = (m_ref.at[0], l_ref.at[0], acc_ref.at[0])
    win_st = (m_ref.at[1], l_ref.at[1], acc_ref.at[1])
    _flash_init(*sel_st)
    _flash_init(*win_st)
    last = (t0 + tq - 1) // tk + 1
    lastw = (t0 + tq - 1) // tkw + 1
    first = jnp.maximum(t0 - (WINDOW - 1), 0) // tkw
    sw_ref = s_ref.at[0:tkw]

    s_ref[...] = _scores_t(ks_ref, qt, 0, tk)

    def sel_trip(kt, carry):
        s_next = _scores_t(ks_ref, qt, kt + 1, tk)
        _flash_consume_t(s_ref, vst_ref, kt, sel_mask(kt, False), *sel_st, tk)
        s_ref[...] = s_next
        return carry

    lax.fori_loop(0, last - 1, sel_trip, 0)
    s_next = _scores_t(kw_ref, qt, first, tkw)
    _flash_consume_t(s_ref, vst_ref, last - 1, sel_mask(last - 1, True), *sel_st, tk)
    sw_ref[...] = s_next

    def win_trip(kt, carry):
        s_next = _scores_t(kw_ref, qt, kt + 1, tkw)
        _flash_consume_t(sw_ref, vwt_ref, kt, win_mask(kt), *win_st, tkw)
        sw_ref[...] = s_next
        return carry

    lax.fori_loop(first, lastw - 1, win_trip, 0)
    _flash_consume_t(sw_ref, vwt_ref, lastw - 1, win_mask(lastw - 1), *win_st, tkw)
    o_slc = acc_ref[0] / l_ref[0]
    o_win = acc_ref[1] / l_ref[1]
    gt_ref[...] = jax.nn.sigmoid(ng_ref[...].T)
    for j in range(nj):
        gate = [gt_ref[pl.ds(br * NSA_HEADS + gi * nj + j, 1), :] for br in range(3)]
        ot = (gate[0] * oct_ref[j * HEAD_DIM:(j + 1) * HEAD_DIM, :] + gate[1] * o_slc[:, j * tq:(j + 1) * tq]
              + gate[2] * o_win[:, j * tq:(j + 1) * tq])
        o_ref[:, j * HEAD_DIM:(j + 1) * HEAD_DIM] = ot.T.astype(o_ref.dtype)


def _sel_attn(pa, tables, kn, vt, sel, o_cmp_t, png, batch, seq, tq=256, tk=512, tkw=256):
    g = NSA_KV_GROUPS
    nq = seq // tq
    gw = NSA_HPG * HEAD_DIM
    tk = min(tk, seq)
    assert tk % tq == 0 and seq % tk == 0
    nsel = seq // SLC_BLOCK
    n = NSA_HPG * tq
    tab = pl.BlockSpec((tq, HEAD_DIM), lambda b, gi, i: (b * nq + i, 0))
    return pl.pallas_call(
        functools.partial(_sel_attn_kernel, tq=tq, tk=tk, tkw=min(tkw, tk)),
        grid=(batch, g, nq),
        in_specs=[
            pl.BlockSpec((tq, gw), lambda b, gi, i: (b * nq + i, gi)),
            tab, tab, tab,
            pl.BlockSpec((seq, HEAD_DIM), lambda b, gi, i: (b, gi)),
            pl.BlockSpec((seq, HEAD_DIM), lambda b, gi, i: (b, g + gi)),
            pl.BlockSpec((None, HEAD_DIM, seq), lambda b, gi, i: (b, gi, 0)),
            pl.BlockSpec((None, HEAD_DIM, seq), lambda b, gi, i: (b, g + gi, 0)),
            pl.BlockSpec((None, None, nsel, tq), lambda b, gi, i: (b, gi, 0, i)),
            pl.BlockSpec((None, gw, tq), lambda b, gi, i: (b, gi, i)),
            pl.BlockSpec((tq, LANES), lambda b, gi, i: (b * nq + i, 0)),
        ],
        out_specs=pl.BlockSpec((tq, gw), lambda b, gi, i: (b * nq + i, gi)),
        out_shape=jax.ShapeDtypeStruct((batch * seq, NSA_WIDTH), BF16),
        scratch_shapes=[
            pltpu.VMEM((2, 1, n), F32),
            pltpu.VMEM((2, 1, n), F32),
            pltpu.VMEM((2, HEAD_DIM, n), F32),
            pltpu.VMEM((LANES, tq), F32),
            pltpu.VMEM((tk, n), F32),
        ],
        compiler_params=_params(("parallel", "parallel", "parallel")),
        name="nsa_sel_win_attn",
    )(pa, *tables, kn, kn, vt, vt, sel, o_cmp_t, png)


HG_TILE = 128
HG_GROUP = 8
HG_LEVELS = (64, 32, 16, 8)
HG_HEADS_PER_STEP = 8


def _split3(x):
    hi = x.astype(BF16)
    r = x - hi.astype(F32)
    mid = r.astype(BF16)
    lo = (r - mid.astype(F32)).astype(BF16)
    return hi, mid, lo


def _group_row(x, size, row):
    t, d = x.shape
    x3 = x.reshape(t // size, size, d)
    return jnp.broadcast_to(x3[:, row:row + 1, :], x3.shape).reshape(t, d)


def _hgrn_heads(qs, zs, vs, gs, lbs, nw, low, sts):
    t = HG_TILE
    nh = len(qs)
    heads = range(nh)
    log2e = math.log2(math.e)
    kks, l2fs = [], []
    for i in heads:
        z, lb = zs[i], lbs[i]
        ez = jnp.exp(-jnp.abs(z))
        r = 1.0 / (1.0 + ez)
        pos = z >= 0.0
        f = lb + (1.0 - lb) * jnp.where(pos, r, ez * r)
        kks.append((1.0 - lb) * jnp.where(pos, ez * r, r))
        l2fs.append(jnp.where(f > 0.0, jnp.log2(f), jnp.minimum(z, 0.0) * log2e))
    bs_ = []
    for i in heads:
        hi, mid, lo = _split3(l2fs[i])
        bs_.append(jnp.dot(low, hi, preferred_element_type=F32) + jnp.dot(low, mid, preferred_element_type=F32)
                   + jnp.dot(low, lo, preferred_element_type=F32))
    ti = lax.broadcasted_iota(jnp.int32, (t, t), 0)
    si = lax.broadcasted_iota(jnp.int32, (t, t), 1)
    atts = [jnp.zeros((t, t), F32) for _ in heads]
    for h in HG_LEVELS:
        pair = ((ti // (2 * h)) == (si // (2 * h))) & ((ti % (2 * h)) >= h) & ((si % (2 * h)) < h)
        for i in heads:
            b = bs_[i]
            bref = _group_row(b, 2 * h, h - 1)
            e = jnp.exp2(-jnp.abs(b - bref))
            atts[i] = atts[i] + jnp.where(pair, _nt_dot((qs[i] * e).astype(BF16), (kks[i] * e).astype(BF16)), 0.0)
    lane_d = lax.broadcasted_iota(jnp.int32, (HGRN_DK, t), 1)
    diags = [jnp.zeros((t, t), F32) for _ in heads]
    for sg in range(HG_GROUP):
        spread = jnp.where((lane_d % HG_GROUP) == sg, 1.0, 0.0).astype(BF16)
        for i in heads:
            bsg = _group_row(bs_[i], HG_GROUP, sg)
            ksg = _group_row(kks[i], HG_GROUP, sg)
            p = (qs[i] * jnp.exp2(jnp.minimum(bs_[i] - bsg, 0.0))) * ksg
            diags[i] = diags[i] + jnp.dot(p.astype(BF16), spread, preferred_element_type=F32)
    same = ((ti // HG_GROUP) == (si // HG_GROUP)) & (si <= ti)
    ys, st_new = [], []
    for i in heads:
        b = bs_[i]
        att = atts[i] + jnp.where(same, diags[i], 0.0)
        vb = vs[i].astype(BF16)
        o = jnp.dot(att.astype(BF16), vb, preferred_element_type=F32)
        o = o + _nt_dot((qs[i] * jnp.exp2(b)).astype(BF16), sts[i].astype(BF16))
        b_last = b[t - 1:t, :]
        kdec = (kks[i] * jnp.exp2(b_last - b)).astype(BF16)
        st_new.append(sts[i] * jnp.exp2(b_last) + _tn_dot(vb, kdec))
        y = o * lax.rsqrt(jnp.mean(o * o, axis=-1, keepdims=True) + NORM_EPS) * nw
        g = gs[i]
        ys.append(y * (g * jax.nn.sigmoid(g)))
    return ys, st_new


def _hgrn_kernel(q_ref, z_ref, v_ref, g_ref, lb_ref, nw_ref, low_ref, o_ref, st_ref, *, nh):
    @pl.when(pl.program_id(2) == 0)
    def _():
        st_ref[...] = jnp.zeros_like(st_ref)

    cols = [slice(hd * HGRN_DK, (hd + 1) * HGRN_DK) for hd in range(nh)]
    load = lambda ref: [ref[:, c].astype(F32) for c in cols]
    ys, st_new = _hgrn_heads(load(q_ref), load(z_ref), load(v_ref), load(g_ref), [lb_ref[hd:hd + 1, :] for hd in range(nh)],
                             nw_ref[...], low_ref[...], [st_ref[hd] for hd in range(nh)])
    for hd in range(nh):
        st_ref[hd] = st_new[hd]
        o_ref[:, cols[hd]] = ys[hd].astype(o_ref.dtype)


def _hgrn(pb, lb, norm_w, batch, seq):
    hh = HGRN_HEADS
    nh = HG_HEADS_PER_STEP
    nt = seq // HG_TILE
    low = jnp.asarray(np.tril(np.ones((HG_TILE, HG_TILE), np.float32)), dtype=BF16)
    spec = lambda part: pl.BlockSpec((HG_TILE, nh * HGRN_DK), lambda b, h, i: (b * nt + i, part * (hh // nh) + h))
    return pl.pallas_call(
        functools.partial(_hgrn_kernel, nh=nh),
        grid=(batch, hh // nh, nt),
        in_specs=[
            spec(0), spec(1), spec(2), spec(3),
            pl.BlockSpec((None, nh, HGRN_DK), lambda b, h, i: (h, 0, 0)),
            pl.BlockSpec((1, HGRN_DV), lambda b, h, i: (0, 0)),
            pl.BlockSpec(low.shape, lambda b, h, i: (0, 0)),
        ],
        out_specs=pl.BlockSpec((HG_TILE, nh * HGRN_DV), lambda b, h, i: (b * nt + i, h)),
        out_shape=jax.ShapeDtypeStruct((batch * seq, HGRN_VW), BF16),
        scratch_shapes=[pltpu.VMEM((nh, HGRN_DV, HGRN_DK), F32)],
        compiler_params=_params(("parallel", "parallel", "arbitrary")),
        name="hgrn2",
    )(pb, pb, pb, pb, lb.reshape(hh // nh, nh, HGRN_DK), norm_w.reshape(1, HGRN_DV), low)


def _hybrid_mixer(h, tables, layer, w_in_t, cmp_pos, cmp_w1, cmp_w2, lb, g_norm_w, w_up_a, w_up_b, batch, seq):
    pa = _matmul_t(h, w_in_t, layer, 0, SEG_A, F32)
    png = _matmul_t(h, w_in_t, layer, SEG_A, LANES, F32, tn=LANES)
    pb = _matmul_t(h, w_in_t, layer, SEG_A + SEG_G, SEG_B, F32)
    kc, vct = _compress(pa, cmp_pos, cmp_w1, cmp_w2, batch, seq)
    o_cmp_t, sel = _cmp_attn(pa, kc, vct, batch, seq)
    kn, vt = _kv_prep(pa, tables, batch, seq)
    o_nsa = _sel_attn(pa, tables, kn, vt, sel, o_cmp_t, png, batch, seq)
    o_hgrn = _hgrn(pb, lb, g_norm_w, batch, seq)
    return _matmul_merge(o_nsa, o_hgrn, w_up_a, w_up_b, layer, pb, 2 * HGRN_KW + 2 * HGRN_VW,
                         2 * HGRN_KW + 2 * HGRN_VW + D_MODEL)


def kernel(x, c, positions, ada_w, ada_b, norm_mix_w, w_in, nsa_cmp_pos, nsa_cmp_w1, nsa_cmp_w2, hgrn_lb_logits,
           hgrn_norm_w, w_up_a, w_up_b, w_out, norm_mlp_w, w_mlp1, w_mlp2, final_norm_w):
    batch, seq, d = x.shape
    depth = ada_w.shape[0]
    lb_all = jnp.cumsum(jax.nn.softmax(hgrn_lb_logits.astype(F32), axis=0), axis=0)
    lb_all = lb_all - lb_all[0:1]
    mod = _ada_mod(c, ada_w, ada_b)
    tables = _rope_tables(positions)
    x2 = x.reshape(batch * seq, d)
    zeros = jnp.zeros((batch, d), F32)
    w_in_t = jnp.transpose(w_in, (0, 2, 1))
    for l in range(depth):
        sh1, sc1, g1, sh2, sc2, g2 = [mod[l][:, k * d:(k + 1) * d] for k in range(6)]
        h = _norm_mod(x2, norm_mix_w[l], sc1, sh1, seq, BF16)
        y = _hybrid_mixer(h, tables, l, w_in_t, nsa_cmp_pos[l], nsa_cmp_w1[l], nsa_cmp_w2[l], lb_all[l],
                          hgrn_norm_w[l], w_up_a, w_up_b, batch, seq)
        x2 = _matmul_resid(y, w_out, l, x2, g1, seq)
        h = _norm_mod(x2, norm_mlp_w[l], sc2, sh2, seq, BF16)
        u = _matmul(h, w_mlp1, l, BF16, relu2=True)
        x2 = _matmul_resid_ksplit(u, w_mlp2, l, x2, g2, seq)
    out = _norm_mod(x2, final_norm_w, zeros, zeros, seq, F32)
    return out.reshape(batch, seq, d)
```

```python
import functools
import math

import numpy as np
import jax
import jax.numpy as jnp
from jax import lax
from jax.experimental import pallas as pl
from jax.experimental.pallas import tpu as pltpu

D_MODEL = 4096
DEPTH = 2
NSA_HEADS = 16
NSA_KV_GROUPS = 4
NSA_HPG = NSA_HEADS // NSA_KV_GROUPS
HEAD_DIM = 128
CMP_LEN = 32
CMP_STRIDE = 16
SLC_BLOCK = 64
SLC_TOPK = 16
WINDOW = 512
ATTN_SCALE = HEAD_DIM ** -0.5
FORCED_BONUS = 1e6
NEG_INF = -1e30
HGRN_HEADS = 16
HGRN_DK = 128
HGRN_DV = 128
ROPE_THETA = 500000.0
ROT_DIM = HEAD_DIM // 4
D_FF = 4 * D_MODEL
NORM_EPS = 1e-6

NSA_WIDTH = NSA_HEADS * HEAD_DIM
KV_WIDTH = NSA_KV_GROUPS * HEAD_DIM
HGRN_KW = HGRN_HEADS * HGRN_DK
HGRN_VW = HGRN_HEADS * HGRN_DV
SEG_A = NSA_WIDTH + 6 * KV_WIDTH
SEG_G = 3 * NSA_HEADS
SEG_B = 2 * HGRN_KW + 2 * HGRN_VW + 2 * D_MODEL

LANES = 128
VMEM_LIMIT = 58 * 1024 * 1024

BF16 = jnp.bfloat16
F32 = jnp.float32


def _params(sem):
    return pltpu.CompilerParams(dimension_semantics=sem, vmem_limit_bytes=VMEM_LIMIT)


def _nt_dot(a, b, precision=None):
    return lax.dot_general(a, b, (((1,), (1,)), ((), ())), preferred_element_type=F32, precision=precision)


def _tn_dot(a, b, precision=None):
    return lax.dot_general(a, b, (((0,), (0,)), ((), ())), preferred_element_type=F32, precision=precision)


def _ada_kernel(cb_ref, w_ref, b_ref, o_ref, cact_ref, *, batch, tn):
    @pl.when((pl.program_id(0) == 0) & (pl.program_id(1) == 0))
    def _():
        cb = cb_ref[...]
        cact_ref[...] = cb * jax.nn.sigmoid(cb)

    for b in range(batch):
        cact = cact_ref[b]
        rows = []
        for s in range(tn // LANES):
            w = w_ref[:, s * LANES:(s + 1) * LANES]
            rows.append(jnp.sum(w * cact, axis=0, keepdims=True))
        o_ref[b:b + 1, :] = jnp.concatenate(rows, axis=1) + b_ref[...]


def _ada_mod(c, ada_w, ada_b):
    depth, k, n = ada_w.shape
    batch = c.shape[0]
    tn = 512
    cb = jnp.broadcast_to(c[:, :, None], (batch, k, LANES))
    out = pl.pallas_call(
        functools.partial(_ada_kernel, batch=batch, tn=tn),
        grid=(depth, n // tn),
        in_specs=[
            pl.BlockSpec((batch, k, LANES), lambda l, j: (0, 0, 0)),
            pl.BlockSpec((None, k, tn), lambda l, j: (l, 0, j)),
            pl.BlockSpec((None, 1, tn), lambda l, j: (l, 0, j)),
        ],
        out_specs=pl.BlockSpec((None, batch, tn), lambda l, j: (l, 0, j)),
        out_shape=jax.ShapeDtypeStruct((depth, batch, n), F32),
        scratch_shapes=[pltpu.VMEM((batch, k, LANES), F32)],
        compiler_params=_params(("arbitrary", "arbitrary")),
        name="ada_mod",
    )(cb, ada_w, ada_b.reshape(depth, 1, n))
    return out


def _norm_kernel(x_ref, w_ref, sc_ref, sh_ref, o_ref):
    x = x_ref[...]
    y = x * lax.rsqrt(jnp.mean(x * x, axis=-1, keepdims=True) + NORM_EPS)
    y = y * w_ref[...]
    y = y * (1.0 + sc_ref[...]) + sh_ref[...]
    o_ref[...] = y.astype(o_ref.dtype)


def _norm_mod(x2, w, sc, sh, seq, out_dtype):
    m, d = x2.shape
    batch = m // seq
    tm = 512
    per = seq // tm
    return pl.pallas_call(
        _norm_kernel,
        grid=(m // tm,),
        in_specs=[
            pl.BlockSpec((tm, d), lambda i: (i, 0)),
            pl.BlockSpec((1, d), lambda i: (0, 0)),
            pl.BlockSpec((None, 1, d), lambda i: (i // per, 0, 0)),
            pl.BlockSpec((None, 1, d), lambda i: (i // per, 0, 0)),
        ],
        out_specs=pl.BlockSpec((tm, d), lambda i: (i, 0)),
        out_shape=jax.ShapeDtypeStruct((m, d), out_dtype),
        compiler_params=_params(("parallel",)),
        name="norm_mod",
    )(x2, w.reshape(1, d), sc.reshape(batch, 1, d), sh.reshape(batch, 1, d))


MM_TM = 1024
MM_TN = 512
MM_TK = 4096
MM_TM_WIDE = 512
MM_TN_WIDE = 1024


def _mm_kernel(a_ref, w_ref, o_ref, wb_ref, *, relu2):
    @pl.when(pl.program_id(1) == 0)
    def _():
        wb_ref[...] = w_ref[...].astype(BF16)

    r = jnp.dot(a_ref[...], wb_ref[...], preferred_element_type=F32)
    if relu2:
        r = jnp.square(jnp.maximum(r, 0.0))
    o_ref[...] = r.astype(o_ref.dtype)


def _matmul(a, w, layer, out_dtype, relu2=False, tm=MM_TM_WIDE, tn=MM_TN_WIDE):
    m, kd = a.shape
    n = w.shape[2]
    tm = min(tm, m)
    assert n % tn == 0 and m % tm == 0
    return pl.pallas_call(
        functools.partial(_mm_kernel, relu2=relu2),
        grid=(n // tn, m // tm),
        in_specs=[
            pl.BlockSpec((tm, kd), lambda j, i: (i, 0)),
            pl.BlockSpec((None, kd, tn), lambda j, i: (layer, 0, j)),
        ],
        out_specs=pl.BlockSpec((tm, tn), lambda j, i: (i, j)),
        out_shape=jax.ShapeDtypeStruct((m, n), out_dtype),
        scratch_shapes=[pltpu.VMEM((kd, tn), BF16)],
        compiler_params=_params(("parallel", "arbitrary")),
        name="matmul",
    )(a, w)


def _mm_t_kernel(*refs, shift):
    if shift:
        a_ref, w_ref, wn_ref, o_ref, wb_ref = refs
    else:
        a_ref, w_ref, o_ref, wb_ref = refs

    @pl.when(pl.program_id(1) == 0)
    def _():
        tn = w_ref.shape[0]
        wb_ref[0:tn - shift, :] = w_ref[shift:tn, :].astype(BF16)
        if shift:
            wb_ref[tn - shift:tn, :] = wn_ref[0:shift, :].astype(BF16)

    o_ref[...] = _nt_dot(a_ref[...], wb_ref[...]).astype(o_ref.dtype)


BF16_SUBLANES = 16


def _matmul_t(a, wt, layer, row0, nrows, out_dtype, tm=MM_TM_WIDE, tn=MM_TN_WIDE):
    m, kd = a.shape
    tm = min(tm, m)
    shift = row0 % tn
    base = row0 - shift
    nxt = 64
    assert nrows % tn == 0 and m % tm == 0 and shift % BF16_SUBLANES == 0 and shift <= nxt and tn % nxt == 0
    in_specs = [
        pl.BlockSpec((tm, kd), lambda j, i: (i, 0)),
        pl.BlockSpec((None, tn, kd), lambda j, i: (layer, base // tn + j, 0)),
    ]
    args = [a, wt]
    if shift:
        in_specs.append(pl.BlockSpec((None, nxt, kd), lambda j, i: (layer, (base + tn * (j + 1)) // nxt, 0)))
        args.append(wt)
    return pl.pallas_call(
        functools.partial(_mm_t_kernel, shift=shift),
        grid=(nrows // tn, m // tm),
        in_specs=in_specs,
        out_specs=pl.BlockSpec((tm, tn), lambda j, i: (i, j)),
        out_shape=jax.ShapeDtypeStruct((m, nrows), out_dtype),
        scratch_shapes=[pltpu.VMEM((tn, kd), BF16)],
        compiler_params=_params(("parallel", "arbitrary")),
        name="matmul_t",
    )(*args)


def _mm_resid_kernel(a_ref, w_ref, x_ref, g_ref, o_ref, wb_ref):
    @pl.when(pl.program_id(1) == 0)
    def _():
        wb_ref[...] = w_ref[...].astype(BF16)

    o_ref[...] = x_ref[...] + g_ref[...] * jnp.dot(a_ref[...], wb_ref[...], preferred_element_type=F32)


def _matmul_resid(a, w, layer, x2, gate, seq, tn=MM_TN, tk=None, ks=0):
    m = a.shape[0]
    kd = tk if tk is not None else a.shape[1]
    n = w.shape[2]
    tm = min(MM_TM, seq)
    per = seq // tm
    batch = m // seq
    return pl.pallas_call(
        _mm_resid_kernel,
        grid=(n // tn, m // tm),
        in_specs=[
            pl.BlockSpec((tm, kd), lambda j, i: (i, ks)),
            pl.BlockSpec((None, kd, tn), lambda j, i: (layer, ks, j)),
            pl.BlockSpec((tm, tn), lambda j, i: (i, j)),
            pl.BlockSpec((None, 1, tn), lambda j, i: (i // per, 0, j)),
        ],
        out_specs=pl.BlockSpec((tm, tn), lambda j, i: (i, j)),
        out_shape=jax.ShapeDtypeStruct((m, n), F32),
        scratch_shapes=[pltpu.VMEM((kd, tn), BF16)],
        compiler_params=_params(("parallel", "arbitrary")),
        name="matmul_resid",
    )(a, w, x2, gate.reshape(batch, 1, n))


def _matmul_resid_ksplit(a, w, layer, x2, gate, seq, tk=MM_TK):
    kd = a.shape[1]
    for ks in range(kd // tk):
        x2 = _matmul_resid(a, w, layer, x2, gate, seq, tk=tk, ks=ks)
    return x2


def _mm_merge_kernel(a_ref, b_ref, wa_ref, wb_ref, ga_ref, gb_ref, o_ref, wab_ref, wbb_ref):
    @pl.when(pl.program_id(1) == 0)
    def _():
        wab_ref[...] = wa_ref[...].astype(BF16)
        wbb_ref[...] = wb_ref[...].astype(BF16)

    ya = jnp.dot(a_ref[...], wab_ref[...], preferred_element_type=F32)
    yb = jnp.dot(b_ref[...], wbb_ref[...], preferred_element_type=F32)
    ga, gb = ga_ref[...].astype(F32), gb_ref[...].astype(F32)
    o_ref[...] = (jax.nn.sigmoid(ga) * ya + jax.nn.sigmoid(gb) * yb).astype(o_ref.dtype)


def _matmul_merge(a, b, wa, wb, layer, pb, ga_col, gb_col, tn=MM_TN):
    m, kd = a.shape
    n = wa.shape[2]
    tm = min(MM_TM, m)
    return pl.pallas_call(
        _mm_merge_kernel,
        grid=(n // tn, m // tm),
        in_specs=[
            pl.BlockSpec((tm, kd), lambda j, i: (i, 0)),
            pl.BlockSpec((tm, kd), lambda j, i: (i, 0)),
            pl.BlockSpec((None, kd, tn), lambda j, i: (layer, 0, j)),
            pl.BlockSpec((None, kd, tn), lambda j, i: (layer, 0, j)),
            pl.BlockSpec((tm, tn), lambda j, i: (i, ga_col // tn + j)),
            pl.BlockSpec((tm, tn), lambda j, i: (i, gb_col // tn + j)),
        ],
        out_specs=pl.BlockSpec((tm, tn), lambda j, i: (i, j)),
        out_shape=jax.ShapeDtypeStruct((m, n), BF16),
        scratch_shapes=[pltpu.VMEM((kd, tn), BF16), pltpu.VMEM((kd, tn), BF16)],
        compiler_params=_params(("parallel", "arbitrary")),
        name="matmul_merge",
    )(a, b, wa, wb, pb, pb)


def _gelu_tanh(x):
    c = math.sqrt(2.0 / math.pi)
    return 0.5 * x * (1.0 + jnp.tanh(c * (x + 0.044715 * (x * x * x))))


def _split2(x):
    hi = x.astype(BF16)
    return hi, (x - hi.astype(F32)).astype(BF16)


def _dot3(a, b):
    ah, al = _split2(a)
    bh, bl = _split2(b)
    dot = functools.partial(jnp.dot, preferred_element_type=F32)
    return dot(ah, bh) + (dot(ah, bl) + dot(al, bh))


def _compress_one(x_ref, pos_ref, w1_ref, w2_ref, kv, nhalf):
    half = CMP_LEN // 2
    xs = [x_ref[pl.ds(l, nhalf, stride=CMP_STRIDE), :] for l in range(half)]
    lhs_a = jnp.concatenate([xs[l] + pos_ref[kv, l:l + 1, :] for l in range(half)], axis=1)
    lhs_b = jnp.concatenate([xs[l] + pos_ref[kv, half + l:half + l + 1, :] for l in range(half)], axis=1)
    w1 = w1_ref[kv]
    acc_a = _dot3(lhs_a, w1[:half].reshape(half * HEAD_DIM, HEAD_DIM))
    acc_b = _dot3(lhs_b, w1[half:].reshape(half * HEAD_DIM, HEAD_DIM))
    hid = acc_a + pltpu.roll(acc_b, nhalf - 1, 0)
    hid = _gelu_tanh(hid)
    return _dot3(hid, w2_ref[kv])


def _compress_kernel(xk_ref, xv_ref, pos_ref, w1_ref, w2_ref, kc_ref, vct_ref, *, nhalf):
    kc_ref[...] = _compress_one(xk_ref, pos_ref, w1_ref, w2_ref, 0, nhalf)
    vct_ref[...] = _compress_one(xv_ref, pos_ref, w1_ref, w2_ref, 1, nhalf).T


def _compress(pa, pos, w1, w2, batch, seq):
    nhalf = seq // CMP_STRIDE
    g = NSA_KV_GROUPS
    col0 = NSA_WIDTH // HEAD_DIM
    full = lambda shape: pl.BlockSpec(shape, lambda b, gi: (0,) * len(shape))
    return pl.pallas_call(
        functools.partial(_compress_kernel, nhalf=nhalf),
        grid=(batch, g),
        in_specs=[
            pl.BlockSpec((seq, HEAD_DIM), lambda b, gi: (b, col0 + gi)),
            pl.BlockSpec((seq, HEAD_DIM), lambda b, gi: (b, col0 + g + gi)),
            full(pos.shape), full(w1.shape), full(w2.shape),
        ],
        out_specs=[
            pl.BlockSpec((None, None, nhalf, HEAD_DIM), lambda b, gi: (b, gi, 0, 0)),
            pl.BlockSpec((None, None, HEAD_DIM, nhalf), lambda b, gi: (b, gi, 0, 0)),
        ],
        out_shape=[
            jax.ShapeDtypeStruct((batch, g, nhalf, HEAD_DIM), F32),
            jax.ShapeDtypeStruct((batch, g, HEAD_DIM, nhalf), F32),
        ],
        compiler_params=_params(("parallel", "parallel")),
        name="nsa_compress",
    )(pa, pa, pos, w1, w2)


def _cmp_slc_overlap_t(seq, ncp):
    nc = (seq - CMP_LEN) // CMP_STRIDE + 1
    nsel = seq // SLC_BLOCK
    cs = np.arange(nc) * CMP_STRIDE
    ce = cs + CMP_LEN - 1
    ss = np.arange(nsel) * SLC_BLOCK
    se = ss + SLC_BLOCK - 1
    ov = np.minimum(ce[:, None], se[None, :]) - np.maximum(cs[:, None], ss[None, :]) + 1
    ov = np.maximum(ov, 0).astype(np.float32)
    out = np.zeros((nsel, ncp), np.float32)
    out[:, :nc] = ov.T
    return out


def _heads_t(x_ref, fn=None):
    parts = []
    for j in range(NSA_HPG):
        x = x_ref[:, j * HEAD_DIM:(j + 1) * HEAD_DIM]
        if fn is not None:
            x = fn(x)
        parts.append(x.T.astype(BF16))
    return jnp.concatenate(parts, axis=1)


def _cmp_attn_kernel(q_ref, kc_ref, vct_ref, ovt_ref, o_ref, sel_ref, *, tq, ncp, nsel):
    i = pl.program_id(2)
    t0 = i * tq
    tpos = t0 + lax.broadcasted_iota(jnp.int32, (ncp, tq), 1)
    nidx = lax.broadcasted_iota(jnp.int32, (ncp, tq), 0)
    vis = (nidx * CMP_STRIDE + (CMP_LEN - 1)) <= tpos
    qt = _heads_t(q_ref)
    st = jnp.dot(kc_ref[...].astype(BF16), qt, preferred_element_type=F32) * ATTN_SCALE
    vct = vct_ref[...].astype(BF16)
    imp = jnp.zeros((ncp, tq), F32)
    for j in range(NSA_HPG):
        s = jnp.where(vis, st[:, j * tq:(j + 1) * tq], NEG_INF)
        e = jnp.exp(s - jnp.max(s, axis=0, keepdims=True))
        p = e / jnp.sum(e, axis=0, keepdims=True)
        p = jnp.where(vis, p, 0.0)
        o_ref[j * HEAD_DIM:(j + 1) * HEAD_DIM, :] = jnp.dot(vct, p.astype(BF16), preferred_element_type=F32)
        imp = imp + p
    pslc = jnp.dot(ovt_ref[...], imp, precision=lax.Precision.HIGHEST, preferred_element_type=F32)
    tl = t0 + lax.broadcasted_iota(jnp.int32, (nsel, tq), 1)
    mi = lax.broadcasted_iota(jnp.int32, (nsel, tq), 0)
    cur = tl // SLC_BLOCK
    valid = mi * SLC_BLOCK <= tl
    forced = (mi == 0) | (mi == cur) | (mi == cur - 1)
    score = jnp.where(valid, pslc + jnp.where(forced, FORCED_BONUS, 0.0), -jnp.inf)
    topk = min(SLC_TOPK, nsel)
    few = (t0 + tq) <= topk * SLC_BLOCK

    @pl.when(few)
    def _():
        sel_ref[...] = jnp.where(valid, 1.0, 0.0)

    @pl.when(jnp.logical_not(few))
    def _():
        rank = jnp.zeros((nsel, tq), F32)
        for mp in range(nsel):
            row = score[mp:mp + 1, :]
            before = (row > score) | ((row == score) & (mi > mp))
            rank = rank + jnp.where(before, 1.0, 0.0)
        sel_ref[...] = jnp.where(valid & (rank < float(topk)), 1.0, 0.0)


def _cmp_attn(pa, kc, vct, batch, seq, tq=1024):
    ncp = seq // CMP_STRIDE
    nsel = seq // SLC_BLOCK
    g = NSA_KV_GROUPS
    nq = seq // tq
    gw = NSA_HPG * HEAD_DIM
    ovt = jnp.asarray(_cmp_slc_overlap_t(seq, ncp))
    return pl.pallas_call(
        functools.partial(_cmp_attn_kernel, tq=tq, ncp=ncp, nsel=nsel),
        grid=(batch, g, nq),
        in_specs=[
            pl.BlockSpec((tq, gw), lambda b, gi, i: (b * nq + i, gi)),
            pl.BlockSpec((None, None, ncp, HEAD_DIM), lambda b, gi, i: (b, gi, 0, 0)),
            pl.BlockSpec((None, None, HEAD_DIM, ncp), lambda b, gi, i: (b, gi, 0, 0)),
            pl.BlockSpec((nsel, ncp), lambda b, gi, i: (0, 0)),
        ],
        out_specs=[
            pl.BlockSpec((None, gw, tq), lambda b, gi, i: (b, gi, i)),
            pl.BlockSpec((None, None, nsel, tq), lambda b, gi, i: (b, gi, 0, i)),
        ],
        out_shape=[
            jax.ShapeDtypeStruct((batch, NSA_WIDTH, seq), F32),
            jax.ShapeDtypeStruct((batch, g, nsel, seq), F32),
        ],
        compiler_params=_params(("parallel", "parallel", "parallel")),
        name="nsa_cmp_attn",
    )(pa, kc, vct, ovt)


def _rope_tables(positions):
    half = ROT_DIM // 2
    inv = ROPE_THETA ** (-jnp.arange(0, ROT_DIM, 2, dtype=F32) / ROT_DIM)
    ang = positions.astype(F32)[..., None] * inv
    cos, sin = jnp.cos(ang), jnp.sin(ang)
    rest = HEAD_DIM - ROT_DIM
    b, s = positions.shape
    one = jnp.ones((b, s, rest), F32)
    zero = jnp.zeros((b, s, rest), F32)
    zh = jnp.zeros((b, s, half), F32)
    c = jnp.concatenate([cos, cos, one], axis=-1)
    s1 = jnp.concatenate([-sin, zh, zero], axis=-1)
    s2 = jnp.concatenate([zh, sin, zero], axis=-1)
    return [t.reshape(b * s, HEAD_DIM) for t in (c, s1, s2)]


def _rot(x, c, s1, s2):
    half = ROT_DIM // 2
    return x * c + pltpu.roll(x, HEAD_DIM - half, 1) * s1 + pltpu.roll(x, half, 1) * s2


def _kv_prep_kernel(xs_ref, xw_ref, c_ref, s1_ref, s2_ref, k_ref, vt_ref):
    c, s1, s2 = c_ref[...], s1_ref[...], s2_ref[...]
    for br, x_ref in enumerate((xs_ref, xw_ref)):
        for gi in range(NSA_KV_GROUPS):
            src = gi * HEAD_DIM
            dst = br * KV_WIDTH + gi * HEAD_DIM
            k_ref[:, dst:dst + HEAD_DIM] = _rot(x_ref[:, src:src + HEAD_DIM], c, s1, s2).astype(k_ref.dtype)
            vt_ref[dst:dst + HEAD_DIM, :] = x_ref[:, KV_WIDTH + src:KV_WIDTH + src + HEAD_DIM].T.astype(vt_ref.dtype)


def _kv_prep(pa, tables, batch, seq, tm=256):
    nt = seq // tm
    w = 2 * KV_WIDTH
    col = (NSA_WIDTH + 2 * KV_WIDTH) // w
    assert col * w == NSA_WIDTH + 2 * KV_WIDTH
    return pl.pallas_call(
        _kv_prep_kernel,
        grid=(batch, nt),
        in_specs=[pl.BlockSpec((tm, w), lambda b, i: (b * nt + i, col)),
                  pl.BlockSpec((tm, w), lambda b, i: (b * nt + i, col + 1))]
        + [pl.BlockSpec((tm, HEAD_DIM), lambda b, i: (b * nt + i, 0))] * 3,
        out_specs=[
            pl.BlockSpec((tm, 2 * KV_WIDTH), lambda b, i: (b * nt + i, 0)),
            pl.BlockSpec((None, 2 * KV_WIDTH, tm), lambda b, i: (b, 0, i)),
        ],
        out_shape=[
            jax.ShapeDtypeStruct((batch * seq, 2 * KV_WIDTH), BF16),
            jax.ShapeDtypeStruct((batch, 2 * KV_WIDTH, seq), BF16),
        ],
        compiler_params=_params(("parallel", "parallel")),
        name="nsa_kv_prep",
    )(pa, pa, *tables)


def _scores_t(k_ref, qt, kt, tk):
    k0 = pl.multiple_of(kt * tk, tk)
    return jnp.dot(k_ref[pl.ds(k0, tk), :], qt, preferred_element_type=F32)


def _flash_init(m_ref, l_ref, acc_ref):
    m_ref[...] = jnp.full(m_ref.shape, NEG_INF, F32)
    l_ref[...] = jnp.zeros(l_ref.shape, F32)
    acc_ref[...] = jnp.zeros(acc_ref.shape, F32)


def _flash_consume_t(s_ref, vt_ref, kt, mask, m_ref, l_ref, acc_ref, tk):
    k0 = pl.multiple_of(kt * tk, tk)
    vt = vt_ref[:, pl.ds(k0, tk)]
    mask = jnp.concatenate([mask] * NSA_HPG, axis=1)
    s = jnp.where(mask, s_ref[...], NEG_INF)
    m_old = m_ref[...]
    m_new = jnp.maximum(m_old, jnp.max(s, axis=0, keepdims=True))
    m_use = jnp.where(m_new > 0.5 * NEG_INF, m_new, 0.0)
    p = jnp.exp2(s - m_use)
    alpha = jnp.exp2(m_old - m_new)
    l_ref[...] = alpha * l_ref[...] + jnp.sum(p, axis=0, keepdims=True)
    acc_ref[...] = alpha * acc_ref[...] + jnp.dot(vt, p.astype(BF16), preferred_element_type=F32)
    m_ref[...] = m_new


def _sel_attn_kernel(q_ref, c_ref, s1_ref, s2_ref, ks_ref, kw_ref, vst_ref, vwt_ref, sel_ref, oct_ref, ng_ref, o_ref,
                     m_ref, l_ref, acc_ref, gt_ref, s_ref, *, tq, tk, tkw):
    gi = pl.program_id(1)
    i = pl.program_id(2)
    t0 = i * tq
    nj = NSA_HPG
    c, s1, s2 = c_ref[...], s1_ref[...], s2_ref[...]
    scale2 = ATTN_SCALE * math.log2(math.e)
    qt = _heads_t(q_ref, lambda x: _rot(x, c, s1, s2) * scale2)
    kpos = lax.broadcasted_iota(jnp.int32, (tk, tq), 0)
    tpos = t0 + lax.broadcasted_iota(jnp.int32, (tk, tq), 1)
    nb = tk // SLC_BLOCK

    def sel_mask(kt, diag):
        rows = sel_ref[pl.ds(pl.multiple_of(kt * nb, nb), nb), :]
        chosen = jnp.broadcast_to(rows[:, None, :], (nb, SLC_BLOCK, tq)).reshape(tk, tq) > 0.5
        return chosen & ((kt * tk + kpos) <= tpos) if diag else chosen

    kposw = lax.broadcasted_iota(jnp.int32, (tkw, tq), 0)
    tposw = t0 + lax.broadcasted_iota(jnp.int32, (tkw, tq), 1)

    def win_mask(kt):
        kp = kt * tkw + kposw
        return (kp <= tposw) & (kp > tposw - WINDOW)

    sel_st = (m_ref.at[0], l_ref.at[0], acc_ref.at[0])
    win_st = (m_ref.at[1], l_ref.at[1], acc_ref.at[1])
    _flash_init(*sel_st)
    _flash_init(*win_st)
    last = (t0 + tq - 1) // tk + 1
    lastw = (t0 + tq - 1) // tkw + 1
    first = jnp.maximum(t0 - (WINDOW - 1), 0) // tkw
    sw_ref = s_ref.at[0:tkw]

    s_ref[...] = _scores_t(ks_ref, qt, 0, tk)

    def sel_trip(kt, carry):
        s_next = _scores_t(ks_ref, qt, kt + 1, tk)
        _flash_consume_t(s_ref, vst_ref, kt, sel_mask(kt, False), *sel_st, tk)
        s_ref[...] = s_next
        return carry

    lax.fori_loop(0, last - 1, sel_trip, 0)
    s_next = _scores_t(kw_ref, qt, first, tkw)
    _flash_consume_t(s_ref, vst_ref, last - 1, sel_mask(last - 1, True), *sel_st, tk)
    sw_ref[...] = s_next

    def win_trip(kt, carry):
        s_next = _scores_t(kw_ref, qt, kt + 1, tkw)
        _flash_consume_t(sw_ref, vwt_ref, kt, win_mask(kt), *win_st, tkw)
        sw_ref[...] = s_next
        return carry

    lax.fori_loop(first, lastw - 1, win_trip, 0)
    _flash_consume_t(sw_ref, vwt_ref, lastw - 1, win_mask(lastw - 1), *win_st, tkw)
    o_slc = acc_ref[0] / l_ref[0]
    o_win = acc_ref[1] / l_ref[1]
    gt_ref[...] = jax.nn.sigmoid(ng_ref[...].T)
    for j in range(nj):
        gate = [gt_ref[pl.ds(br * NSA_HEADS + gi * nj + j, 1), :] for br in range(3)]
        ot = (gate[0] * oct_ref[j * HEAD_DIM:(j + 1) * HEAD_DIM, :] + gate[1] * o_slc[:, j * tq:(j + 1) * tq]
              + gate[2] * o_win[:, j * tq:(j + 1) * tq])
        o_ref[:, j * HEAD_DIM:(j + 1) * HEAD_DIM] = ot.T.astype(o_ref.dtype)


def _sel_attn(pa, tables, kn, vt, sel, o_cmp_t, png, batch, seq, tq=256, tk=512, tkw=256):
    g = NSA_KV_GROUPS
    nq = seq // tq
    gw = NSA_HPG * HEAD_DIM
    tk = min(tk, seq)
    assert tk % tq == 0 and seq % tk == 0
    nsel = seq // SLC_BLOCK
    n = NSA_HPG * tq
    tab = pl.BlockSpec((tq, HEAD_DIM), lambda b, gi, i: (b * nq + i, 0))
    return pl.pallas_call(
        functools.partial(_sel_attn_kernel, tq=tq, tk=tk, tkw=min(tkw, tk)),
        grid=(batch, g, nq),
        in_specs=[
            pl.BlockSpec((tq, gw), lambda b, gi, i: (b * nq + i, gi)),
            tab, tab, tab,
            pl.BlockSpec((seq, HEAD_DIM), lambda b, gi, i: (b, gi)),
            pl.BlockSpec((seq, HEAD_DIM), lambda b, gi, i: (b, g + gi)),
            pl.BlockSpec((None, HEAD_DIM, seq), lambda b, gi, i: (b, gi, 0)),
            pl.BlockSpec((None, HEAD_DIM, seq), lambda b, gi, i: (b, g + gi, 0)),
            pl.BlockSpec((None, None, nsel, tq), lambda b, gi, i: (b, gi, 0, i)),
            pl.BlockSpec((None, gw, tq), lambda b, gi, i: (b, gi, i)),
            pl.BlockSpec((tq, LANES), lambda b, gi, i: (b * nq + i, 0)),
        ],
        out_specs=pl.BlockSpec((tq, gw), lambda b, gi, i: (b * nq + i, gi)),
        out_shape=jax.ShapeDtypeStruct((batch * seq, NSA_WIDTH), BF16),
        scratch_shapes=[
            pltpu.VMEM((2, 1, n), F32),
            pltpu.VMEM((2, 1, n), F32),
            pltpu.VMEM((2, HEAD_DIM, n), F32),
            pltpu.VMEM((LANES, tq), F32),
            pltpu.VMEM((tk, n), F32),
        ],
        compiler_params=_params(("parallel", "parallel", "parallel")),
        name="nsa_sel_win_attn",
    )(pa, *tables, kn, kn, vt, vt, sel, o_cmp_t, png)


HG_TILE = 128
HG_GROUP = 8
HG_LEVELS = (64, 32, 16, 8, 4, 2, 1)
HG_HEADS_PER_STEP = 8


def _split3(x):
    hi = x.astype(BF16)
    r = x - hi.astype(F32)
    mid = r.astype(BF16)
    lo = (r - mid.astype(F32)).astype(BF16)
    return hi, mid, lo


def _group_row(x, size, row):
    t, d = x.shape
    x3 = x.reshape(t // size, size, d)
    return jnp.broadcast_to(x3[:, row:row + 1, :], x3.shape).reshape(t, d)


def _hgrn_heads(qs, zs, vs, gs, lbs, nw, low, sts):
    t = HG_TILE
    nh = len(qs)
    heads = range(nh)
    log2e = math.log2(math.e)
    kks, l2fs = [], []
    for i in heads:
        z, lb = zs[i], lbs[i]
        ez = jnp.exp(-jnp.abs(z))
        r = 1.0 / (1.0 + ez)
        pos = z >= 0.0
        f = lb + (1.0 - lb) * jnp.where(pos, r, ez * r)
        kks.append((1.0 - lb) * jnp.where(pos, ez * r, r))
        l2fs.append(jnp.where(f > 0.0, jnp.log2(f), jnp.minimum(z, 0.0) * log2e))
    bs_ = []
    for i in heads:
        hi, mid, lo = _split3(l2fs[i])
        bs_.append(jnp.dot(low, hi, preferred_element_type=F32) + jnp.dot(low, mid, preferred_element_type=F32)
                   + jnp.dot(low, lo, preferred_element_type=F32))
    ti = lax.broadcasted_iota(jnp.int32, (t, t), 0)
    si = lax.broadcasted_iota(jnp.int32, (t, t), 1)
    atts = [jnp.zeros((t, t), F32) for _ in heads]
    row8 = lax.broadcasted_iota(jnp.int32, (t, HGRN_DK), 0) % HG_GROUP
    for h in HG_LEVELS:
        pair = ((ti // (2 * h)) == (si // (2 * h))) & ((ti % (2 * h)) >= h) & ((si % (2 * h)) < h)
        for i in heads:
            b = bs_[i]
            if 2 * h >= HG_GROUP:
                bref = _group_row(b, 2 * h, h - 1)
            elif h == 2:
                bref = jnp.where(row8 < 4, _group_row(b, HG_GROUP, 1), _group_row(b, HG_GROUP, 5))
            else:
                bref = jnp.where(row8 % 2 == 1, pltpu.roll(b, 1, 0), b)
            e = jnp.exp2(-jnp.abs(b - bref))
            atts[i] = atts[i] + jnp.where(pair, _nt_dot((qs[i] * e).astype(BF16), (kks[i] * e).astype(BF16)), 0.0)
    ones = jnp.ones((HGRN_DK, t), BF16)
    eye = ti == si
    for i in heads:
        atts[i] = atts[i] + jnp.where(eye, jnp.dot((qs[i] * kks[i]).astype(BF16), ones, preferred_element_type=F32), 0.0)
    ys, st_new = [], []
    for i in heads:
        b = bs_[i]
        att = atts[i]
        vb = vs[i].astype(BF16)
        o = jnp.dot(att.astype(BF16), vb, preferred_element_type=F32)
        o = o + _nt_dot((qs[i] * jnp.exp2(b)).astype(BF16), sts[i].astype(BF16))
        b_last = b[t - 1:t, :]
        kdec = (kks[i] * jnp.exp2(b_last - b)).astype(BF16)
        st_new.append(sts[i] * jnp.exp2(b_last) + _tn_dot(vb, kdec))
        y = o * lax.rsqrt(jnp.mean(o * o, axis=-1, keepdims=True) + NORM_EPS) * nw
        g = gs[i]
        ys.append(y * (g * jax.nn.sigmoid(g)))
    return ys, st_new


def _hgrn_kernel(q_ref, z_ref, v_ref, g_ref, lb_ref, nw_ref, low_ref, o_ref, st_ref, *, nh):
    @pl.when(pl.program_id(2) == 0)
    def _():
        st_ref[...] = jnp.zeros_like(st_ref)

    cols = [slice(hd * HGRN_DK, (hd + 1) * HGRN_DK) for hd in range(nh)]
    load = lambda ref: [ref[:, c].astype(F32) for c in cols]
    ys, st_new = _hgrn_heads(load(q_ref), load(z_ref), load(v_ref), load(g_ref), [lb_ref[hd:hd + 1, :] for hd in range(nh)],
                             nw_ref[...], low_ref[...], [st_ref[hd] for hd in range(nh)])
    for hd in range(nh):
        st_ref[hd] = st_new[hd]
        o_ref[:, cols[hd]] = ys[hd].astype(o_ref.dtype)


def _hgrn(pb, lb, norm_w, batch, seq):
    hh = HGRN_HEADS
    nh = HG_HEADS_PER_STEP
    nt = seq // HG_TILE
    low = jnp.asarray(np.tril(np.ones((HG_TILE, HG_TILE), np.float32)), dtype=BF16)
    spec = lambda part: pl.BlockSpec((HG_TILE, nh * HGRN_DK), lambda b, h, i: (b * nt + i, part * (hh // nh) + h))
    return pl.pallas_call(
        functools.partial(_hgrn_kernel, nh=nh),
        grid=(batch, hh // nh, nt),
        in_specs=[
            spec(0), spec(1), spec(2), spec(3),
            pl.BlockSpec((None, nh, HGRN_DK), lambda b, h, i: (h, 0, 0)),
            pl.BlockSpec((1, HGRN_DV), lambda b, h, i: (0, 0)),
            pl.BlockSpec(low.shape, lambda b, h, i: (0, 0)),
        ],
        out_specs=pl.BlockSpec((HG_TILE, nh * HGRN_DV), lambda b, h, i: (b * nt + i, h)),
        out_shape=jax.ShapeDtypeStruct((batch * seq, HGRN_VW), BF16),
        scratch_shapes=[pltpu.VMEM((nh, HGRN_DV, HGRN_DK), F32)],
        compiler_params=_params(("parallel", "parallel", "arbitrary")),
        name="hgrn2",
    )(pb, pb, pb, pb, lb.reshape(hh // nh, nh, HGRN_DK), norm_w.reshape(1, HGRN_DV), low)


def _hybrid_mixer(h, tables, layer, w_in_t, cmp_pos, cmp_w1, cmp_w2, lb, g_norm_w, w_up_a, w_up_b, batch, seq):
    pa = _matmul_t(h, w_in_t, layer, 0, SEG_A, F32)
    png = _matmul_t(h, w_in_t, layer, SEG_A, LANES, F32, tn=LANES)
    pb = _matmul_t(h, w_in_t, layer, SEG_A + SEG_G, SEG_B, F32)
    kc, vct = _compress(pa, cmp_pos, cmp_w1, cmp_w2, batch, seq)
    o_cmp_t, sel = _cmp_attn(pa, kc, vct, batch, seq)
    kn, vt = _kv_prep(pa, tables, batch, seq)
    o_nsa = _sel_attn(pa, tables, kn, vt, sel, o_cmp_t, png, batch, seq)
    o_hgrn = _hgrn(pb, lb, g_norm_w, batch, seq)
    return _matmul_merge(o_nsa, o_hgrn, w_up_a, w_up_b, layer, pb, 2 * HGRN_KW + 2 * HGRN_VW,
                         2 * HGRN_KW + 2 * HGRN_VW + D_MODEL)


def kernel(x, c, positions, ada_w, ada_b, norm_mix_w, w_in, nsa_cmp_pos, nsa_cmp_w1, nsa_cmp_w2, hgrn_lb_logits,
           hgrn_norm_w, w_up_a, w_up_b, w_out, norm_mlp_w, w_mlp1, w_mlp2, final_norm_w):
    batch, seq, d = x.shape
    depth = ada_w.shape[0]
    lb_all = jnp.cumsum(jax.nn.softmax(hgrn_lb_logits.astype(F32), axis=0), axis=0)
    lb_all = lb_all - lb_all[0:1]
    mod = _ada_mod(c, ada_w, ada_b)
    tables = _rope_tables(positions)
    x2 = x.reshape(batch * seq, d)
    zeros = jnp.zeros((batch, d), F32)
    w_in_t = jnp.transpose(w_in, (0, 2, 1))
    for l in range(depth):
        sh1, sc1, g1, sh2, sc2, g2 = [mod[l][:, k * d:(k + 1) * d] for k in range(6)]
        h = _norm_mod(x2, norm_mix_w[l], sc1, sh1, seq, BF16)
        y = _hybrid_mixer(h, tables, l, w_in_t, nsa_cmp_pos[l], nsa_cmp_w1[l], nsa_cmp_w2[l], lb_all[l],
                          hgrn_norm_w[l], w_up_a, w_up_b, batch, seq)
        x2 = _matmul_resid(y, w_out, l, x2, g1, seq)
        h = _norm_mod(x2, norm_mlp_w[l], sc2, sh2, seq, BF16)
        u = _matmul(h, w_mlp1, l, BF16, relu2=True)
        x2 = _matmul_resid_ksplit(u, w_mlp2, l, x2, g2, seq)
    out = _norm_mod(x2, final_norm_w, zeros, zeros, seq, F32)
    return out.reshape(batch, seq, d)
```
